```python
import jax
import jax.numpy as jnp
from jax import lax
import numpy as np

D_MODEL = 1024
BATCH = 32
SEQ = 256
DEPTH = 4
DEC_BATCH = 4
DEC_SEQ = 2048
PAST_LEN = 512

GRID_W = 64
N_MIXERS = 4
N_PER_MIXER = DEPTH // N_MIXERS
EPS = 1e-6
ADA_CHUNKS = 6

MLA_HEADS = 8
Q_LORA_RANK = 384
KV_LORA_RANK = 256
QK_NOPE_DIM = 128
QK_ROPE_DIM = 64
V_HEAD_DIM = 128
ROPE_THETA = 10000.0
Q_BLOCK = 128

D_RNN = D_MODEL
RG_BLOCKS = 4
RG_BLOCK_W = D_RNN // RG_BLOCKS
RG_CONV_W = 4
RG_C = 8.0

FNET_GROUPS = 4
FNET_GROUP_W = D_MODEL // FNET_GROUPS

CONF_CONV_W = 31

D_FF = -(-(8 * D_MODEL) // (3 * 256)) * 256

kernel_name = 'hybrid_diffusion_trunk_step'


def rms_norm(x, g):
    xf = x.astype(jnp.float32)
    y = xf * lax.rsqrt(jnp.mean(xf * xf, axis=-1, keepdims=True) + EPS)
    return (y * g.astype(jnp.float32)).astype(x.dtype)


def layer_norm(x, g, b):
    xf = x.astype(jnp.float32)
    mu = jnp.mean(xf, axis=-1, keepdims=True)
    var = jnp.mean(jnp.square(xf - mu), axis=-1, keepdims=True)
    y = (xf - mu) * lax.rsqrt(var + EPS) * g.astype(jnp.float32) + b.astype(jnp.float32)
    return y.astype(x.dtype)


def modulate(h, shift, scale):
    return (h * (1.0 + scale) + shift).astype(h.dtype)


def grid_rope_tables(n_tokens):
    rows = n_tokens // GRID_W
    row_pos = jnp.repeat(jnp.arange(rows, dtype=jnp.float32), GRID_W)
    col_pos = jnp.tile(jnp.arange(GRID_W, dtype=jnp.float32), rows)
    n_freq = QK_ROPE_DIM // 4
    inv_freq = ROPE_THETA ** (-jnp.arange(n_freq, dtype=jnp.float32) / n_freq)
    ang = jnp.concatenate([row_pos[:, None] * inv_freq, col_pos[:, None] * inv_freq], axis=-1)
    return jnp.cos(ang), jnp.sin(ang)


def apply_rope(x, cos, sin):
    half = QK_ROPE_DIM // 2
    xf = x.astype(jnp.float32)
    x1, x2 = xf[..., :half], xf[..., half:]
    return jnp.concatenate([x1 * cos - x2 * sin, x1 * sin + x2 * cos], axis=-1).astype(x.dtype)


def depthwise_conv(x, w, b):
    width = w.shape[0]
    pad_left = (width - 1) // 2
    y = lax.conv_general_dilated(x, w[:, None, :], window_strides=(1,),
                                 padding=[(pad_left, width - 1 - pad_left)],
                                 dimension_numbers=('NWC', 'WIO', 'NWC'),
                                 feature_group_count=x.shape[-1])
    return y + b


def swiglu(h, w_gate, w_up, w_down):
    return (jax.nn.silu(h @ w_gate) * (h @ w_up)) @ w_down


def mla_q(h, w_dq, q_norm, w_uq):
    cq = rms_norm(h @ w_dq, q_norm)
    q = (cq @ w_uq).reshape(h.shape[0], h.shape[1], MLA_HEADS, QK_NOPE_DIM + QK_ROPE_DIM)
    return q[..., :QK_NOPE_DIM], q[..., QK_NOPE_DIM:]


def mla_compress_kv(h, w_dkv, kv_norm):
    z = h @ w_dkv
    return rms_norm(z[..., :KV_LORA_RANK], kv_norm), z[..., KV_LORA_RANK:]


def mla_expand_kv(ckv, w_ukv):
    kv = (ckv @ w_ukv).reshape(ckv.shape[0], ckv.shape[1], MLA_HEADS, QK_NOPE_DIM + V_HEAD_DIM)
    return kv[..., :QK_NOPE_DIM], kv[..., QK_NOPE_DIM:]


def mla_attend(q_nope, q_rope, k_nope, k_rope, v):
    bsz, sq = q_nope.shape[0], q_nope.shape[1]
    bq = min(Q_BLOCK, sq)
    nb = sq // bq
    scale = (QK_NOPE_DIM + QK_ROPE_DIM) ** -0.5

    def to_blocks(t):
        return jnp.swapaxes(t.reshape(bsz, nb, bq, *t.shape[2:]), 0, 1)

    def one_block(qs):
        qn, qr = qs
        s = (jnp.einsum('bqhd,bkhd->bhqk', qn, k_nope).astype(jnp.float32)
             + jnp.einsum('bqhd,bkd->bhqk', qr, k_rope).astype(jnp.float32)) * scale
        p = jax.nn.softmax(s, axis=-1).astype(v.dtype)
        return jnp.einsum('bhqk,bkhd->bqhd', p, v)

    o = lax.map(one_block, (to_blocks(q_nope), to_blocks(q_rope)))
    return jnp.swapaxes(o, 0, 1).reshape(bsz, sq, MLA_HEADS * V_HEAD_DIM)


def mla_context(h, w_dq, q_norm, w_uq, w_dkv, kv_norm, w_ukv, w_o):
    q_nope, q_rope = mla_q(h, w_dq, q_norm, w_uq)
    ckv, k_rope = mla_compress_kv(h, w_dkv, kv_norm)
    k_nope, v = mla_expand_kv(ckv, w_ukv)
    o = mla_attend(q_nope, q_rope, k_nope, k_rope, v)
    return o @ w_o, ckv, k_rope


def mla_latent(h, ckv_ctx, krope_ctx, cos, sin, w_dq, q_norm, w_uq, w_dkv, kv_norm, w_ukv, w_o):
    q_nope, q_rope = mla_q(h, w_dq, q_norm, w_uq)
    q_rope = apply_rope(q_rope, cos[None, :, None, :], sin[None, :, None, :])
    ckv, k_rope = mla_compress_kv(h, w_dkv, kv_norm)
    k_rope = apply_rope(k_rope, cos[None], sin[None])
    k_nope, v = mla_expand_kv(jnp.concatenate([ckv_ctx, ckv], axis=1), w_ukv)
    k_rope_all = jnp.concatenate([krope_ctx, k_rope], axis=1)
    o = mla_attend(q_nope, q_rope, k_nope, k_rope_all, v)
    return o @ w_o


def block_diag(x, w, b):
    xb = x.reshape(x.shape[0], x.shape[1], RG_BLOCKS, RG_BLOCK_W)
    return jnp.einsum('bsnc,ncd->bsnd', xb, w).reshape(x.shape) + b


def rg_lru_scan(xc, h0, w_a, b_a, w_i, b_i, lam, reverse):
    r = jax.nn.sigmoid(block_diag(xc, w_a, b_a).astype(jnp.float32))
    i = jax.nn.sigmoid(block_diag(xc, w_i, b_i).astype(jnp.float32))
    log_a = -RG_C * r * jax.nn.softplus(-lam.astype(jnp.float32))
    a = jnp.exp(log_a)
    u = jnp.sqrt(-jnp.expm1(2.0 * log_a)) * (i * xc.astype(jnp.float32))

    def step(h, au):
        a_t, u_t = au
        h = a_t * h + u_t
        return h, h

    h_last, hs = lax.scan(step, h0.astype(jnp.float32),
                          (jnp.swapaxes(a, 0, 1), jnp.swapaxes(u, 0, 1)), reverse=reverse)
    return jnp.swapaxes(hs, 0, 1), h_last


def rglru_mixer(h, h0, w_x, w_y, conv_w, conv_b, w_a, b_a, w_i, b_i, lam, w_o):
    gate = jax.nn.gelu(h @ w_y)
    xc = depthwise_conv(h @ w_x, conv_w, conv_b)
    y_f, h_f = rg_lru_scan(xc, h0[:, 0], w_a[0], b_a[0], w_i[0], b_i[0], lam[0], reverse=False)
    y_b, h_b = rg_lru_scan(xc, h0[:, 1], w_a[1], b_a[1], w_i[1], b_i[1], lam[1], reverse=True)
    out = ((y_f + y_b).astype(h.dtype) * gate) @ w_o
    return out, jnp.stack([h_f, h_b], axis=1).astype(h.dtype)


def fourier_mixer(h, w_o, b_o):
    bsz, s, _ = h.shape
    hg = h.astype(jnp.float32).reshape(bsz, s, FNET_GROUPS, FNET_GROUP_W)
    f = jnp.fft.fft2(hg, axes=(1, 3), norm='ortho').real
    return f.reshape(bsz, s, D_MODEL).astype(h.dtype) @ w_o + b_o


def conformer_conv(h, w_pw1, b_pw1, dw_w, dw_b, ln_g, ln_b, w_pw2, b_pw2):
    z = h @ w_pw1 + b_pw1
    z = z[..., :D_MODEL] * jax.nn.sigmoid(z[..., D_MODEL:])
    z = depthwise_conv(z, dw_w, dw_b)
    z = jax.nn.silu(layer_norm(z, ln_g, ln_b))
    return z @ w_pw2 + b_pw2


def setup_inputs(seed: int = 0) -> dict:
    keys = iter(jax.random.split(jax.random.key(seed), 64))

    def nrm(shape, scale):
        return jax.random.normal(next(keys), shape, jnp.float32) * scale

    def gain(shape):
        return 1.0 + nrm(shape, 0.02)

    L = N_PER_MIXER
    D = D_MODEL
    u = jax.random.uniform(next(keys), (L, 2, D_RNN), jnp.float32, 0.9, 0.999)
    a_base = u ** (1.0 / RG_C)
    rg_lam = jnp.log(a_base) - jnp.log1p(-a_base)
    return {
        'x_prompt': nrm((BATCH, SEQ, D), 1.0),
        'x_sample': nrm((DEC_BATCH, DEC_SEQ, D), 1.0),
        'cache_mla_ckv': nrm((DEC_BATCH, L, PAST_LEN, KV_LORA_RANK), 1.0),
        'cache_mla_krope': nrm((DEC_BATCH, L, PAST_LEN, QK_ROPE_DIM), 1.0),
        'state_rglru': nrm((DEC_BATCH, L, 2, D_RNN), 0.5),
        'c': nrm((DEC_BATCH, D), 1.0),
        'c_ctx': nrm((D,), 1.0),
        'ada_w': nrm((DEPTH, D, ADA_CHUNKS * D), 0.5 * D ** -0.5),
        'ada_b': nrm((DEPTH, ADA_CHUNKS * D), 0.02),
        'norm_mix': gain((DEPTH, D)),
        'norm_ffn': gain((DEPTH, D)),
        'mla_w_dq': nrm((L, D, Q_LORA_RANK), D ** -0.5),
        'mla_q_norm': gain((L, Q_LORA_RANK)),
        'mla_w_uq': nrm((L, Q_LORA_RANK, MLA_HEADS * (QK_NOPE_DIM + QK_ROPE_DIM)), Q_LORA_RANK ** -0.5),
        'mla_w_dkv': nrm((L, D, KV_LORA_RANK + QK_ROPE_DIM), D ** -0.5),
        'mla_kv_norm': gain((L, KV_LORA_RANK)),
        'mla_w_ukv': nrm((L, KV_LORA_RANK, MLA_HEADS * (QK_NOPE_DIM + V_HEAD_DIM)), KV_LORA_RANK ** -0.5),
        'mla_w_o': nrm((L, MLA_HEADS * V_HEAD_DIM, D), (MLA_HEADS * V_HEAD_DIM) ** -0.5),
        'rg_w_x': nrm((L, D, D_RNN), D ** -0.5),
        'rg_w_y': nrm((L, D, D_RNN), D ** -0.5),
        'rg_conv_w': nrm((L, RG_CONV_W, D_RNN), RG_CONV_W ** -0.5),
        'rg_conv_b': nrm((L, D_RNN), 0.02),
        'rg_w_a': nrm((L, 2, RG_BLOCKS, RG_BLOCK_W, RG_BLOCK_W), RG_BLOCK_W ** -0.5),
        'rg_b_a': nrm((L, 2, D_RNN), 0.02),
        'rg_w_i': nrm((L, 2, RG_BLOCKS, RG_BLOCK_W, RG_BLOCK_W), RG_BLOCK_W ** -0.5),
        'rg_b_i': nrm((L, 2, D_RNN), 0.02),
        'rg_lam': rg_lam,
        'rg_w_o': nrm((L, D_RNN, D), D_RNN ** -0.5),
        'fn_w_o': nrm((L, D, D), D ** -0.5),
        'fn_b_o': nrm((L, D), 0.02),
        'cf_w_pw1': nrm((L, D, 2 * D), D ** -0.5),
        'cf_b_pw1': nrm((L, 2 * D), 0.02),
        'cf_dw_w': nrm((L, CONF_CONV_W, D), CONF_CONV_W ** -0.5),
        'cf_dw_b': nrm((L, D), 0.02),
        'cf_ln_g': gain((L, D)),
        'cf_ln_b': nrm((L, D), 0.02),
        'cf_w_pw2': nrm((L, D, D), D ** -0.5),
        'cf_b_pw2': nrm((L, D), 0.02),
        'ffn_w_gate': nrm((DEPTH, D, D_FF), D ** -0.5),
        'ffn_w_up': nrm((DEPTH, D, D_FF), D ** -0.5),
        'ffn_w_down': nrm((DEPTH, D_FF, D), D_FF ** -0.5),
        'final_norm': gain((D,)),
    }


def reference(x_prompt, x_sample, cache_mla_ckv, cache_mla_krope, state_rglru, c, c_ctx,
              ada_w, ada_b, norm_mix, norm_ffn,
              mla_w_dq, mla_q_norm, mla_w_uq, mla_w_dkv, mla_kv_norm, mla_w_ukv, mla_w_o,
              rg_w_x, rg_w_y, rg_conv_w, rg_conv_b, rg_w_a, rg_b_a, rg_w_i, rg_b_i, rg_lam, rg_w_o,
              fn_w_o, fn_b_o,
              cf_w_pw1, cf_b_pw1, cf_dw_w, cf_dw_b, cf_ln_g, cf_ln_b, cf_w_pw2, cf_b_pw2,
              ffn_w_gate, ffn_w_up, ffn_w_down, final_norm):
    xp, xs = x_prompt, x_sample
    cos, sin = grid_rope_tables(xs.shape[1])
    new_ckv, new_krope, new_rg = [], [], []
    for layer in range(DEPTH):
        kind, inst = layer % N_MIXERS, layer // N_MIXERS
        mod_p = jax.nn.silu(c_ctx) @ ada_w[layer] + ada_b[layer]
        mod_s = (jax.nn.silu(c) @ ada_w[layer] + ada_b[layer])[:, None, :]
        p_sh1, p_sc1, p_g1, p_sh2, p_sc2, p_g2 = jnp.split(mod_p, ADA_CHUNKS, axis=-1)
        s_sh1, s_sc1, s_g1, s_sh2, s_sc2, s_g2 = jnp.split(mod_s, ADA_CHUNKS, axis=-1)
        hp = modulate(rms_norm(xp, norm_mix[layer]), p_sh1, p_sc1)
        hs = modulate(rms_norm(xs, norm_mix[layer]), s_sh1, s_sc1)
        if kind == 0:
            w = (mla_w_dq[inst], mla_q_norm[inst], mla_w_uq[inst], mla_w_dkv[inst],
                 mla_kv_norm[inst], mla_w_ukv[inst], mla_w_o[inst])
            op, ckv_p, krope_p = mla_context(hp, *w)
            os_ = mla_latent(hs, cache_mla_ckv[:, inst], cache_mla_krope[:, inst], cos, sin, *w)
            new_ckv.append(ckv_p)
            new_krope.append(krope_p)
        elif kind == 1:
            w = (rg_w_x[inst], rg_w_y[inst], rg_conv_w[inst], rg_conv_b[inst], rg_w_a[inst],
                 rg_b_a[inst], rg_w_i[inst], rg_b_i[inst], rg_lam[inst], rg_w_o[inst])
            h0p = jnp.zeros((xp.shape[0], 2, D_RNN), xp.dtype)
            op, st_p = rglru_mixer(hp, h0p, *w)
            os_, _ = rglru_mixer(hs, state_rglru[:, inst], *w)
            new_rg.append(st_p)
        elif kind == 2:
            op = fourier_mixer(hp, fn_w_o[inst], fn_b_o[inst])
            os_ = fourier_mixer(hs, fn_w_o[inst], fn_b_o[inst])
        else:
            w = (cf_w_pw1[inst], cf_b_pw1[inst], cf_dw_w[inst], cf_dw_b[inst], cf_ln_g[inst],
                 cf_ln_b[inst], cf_w_pw2[inst], cf_b_pw2[inst])
            op = conformer_conv(hp, *w)
            os_ = conformer_conv(hs, *w)
        xp = xp + (p_g1 * op).astype(xp.dtype)
        xs = xs + (s_g1 * os_).astype(xs.dtype)
        fp = swiglu(modulate(rms_norm(xp, norm_ffn[layer]), p_sh2, p_sc2),
                    ffn_w_gate[layer], ffn_w_up[layer], ffn_w_down[layer])
        fs = swiglu(modulate(rms_norm(xs, norm_ffn[layer]), s_sh2, s_sc2),
                    ffn_w_gate[layer], ffn_w_up[layer], ffn_w_down[layer])
        xp = xp + (p_g2 * fp).astype(xp.dtype)
        xs = xs + (s_g2 * fs).astype(xs.dtype)
    y_prompt = rms_norm(xp, final_norm)
    y_sample = rms_norm(xs, final_norm)
    state_mla_ckv = jnp.stack(new_ckv, axis=1)
    state_mla_krope = jnp.stack(new_krope, axis=1)
    state_rglru_new = jnp.stack(new_rg, axis=1)
    return (y_prompt, y_sample, state_mla_ckv, state_mla_krope, state_rglru_new)
```

```python
import functools

import numpy as np
import jax
import jax.numpy as jnp
from jax import lax
from jax.experimental import pallas as pl
from jax.experimental.pallas import tpu as pltpu

F32 = jnp.float32
BF16 = jnp.bfloat16

EPS = 1e-6
GRID_W = 64
MLA_HEADS = 8
KV_LORA_RANK = 256
QK_NOPE_DIM = 128
QK_ROPE_DIM = 64
V_HEAD_DIM = 128
ROPE_THETA = 10000.0
RG_BLOCKS = 4
RG_CONV_W = 4
RG_C = 8.0
FNET_GROUPS = 4
CONF_CONV_W = 31

LANES = 128
SUBLANES = 8
HEAD_W = 2 * LANES
SEQ_TILE = 256
FFN_TILE = 512
ATTN_Q_TILE = 512
FNET_ROW_TILE = 512
FFN_CHUNK = 256
CONF_HALO = 16
VMEM_LIMIT = 52 * 1024 * 1024


def _dot(a, b):
    return jnp.dot(a, b, preferred_element_type=F32)


def _rms(x, g):
    return x * lax.rsqrt(jnp.mean(x * x, axis=-1, keepdims=True) + EPS) * g


def _norm_mod(x, g, shift, scale):
    return _rms(x, g) * (1.0 + scale) + shift


def _resident(shape):
    nd = len(shape)
    return pl.BlockSpec(shape, lambda *_: (0,) * nd, pipeline_mode=pl.Buffered(1))


def _params(n_axes, semantics="parallel"):
    return pltpu.CompilerParams(dimension_semantics=(semantics,) * n_axes, vmem_limit_bytes=VMEM_LIMIT)


class _Layout:
    def __init__(self, n_prompt_seq, prompt_len, n_sample_seq, sample_len):
        self.bp, self.sp, self.bs, self.ss = n_prompt_seq, prompt_len, n_sample_seq, sample_len
        self.tp = n_prompt_seq * prompt_len
        self.ts = n_sample_seq * sample_len
        self.t = self.tp + self.ts

    def prompt_tiles(self, tile):
        assert self.tp % tile == 0 and self.ss % tile == 0
        return self.tp // tile

    def sample_tiles_per_seq(self, tile):
        return self.ss // tile

    def mod_row(self, tile):
        npt, spb = self.prompt_tiles(tile), self.sample_tiles_per_seq(tile)
        return lambda i: jnp.where(i < npt, 0, 1 + (i - npt) // spb)

    def mod_spec(self, tile, chunk, d, order=lambda i: i):
        row = self.mod_row(tile)
        return pl.BlockSpec((None, 1, d), lambda i: (row(order(i)), 0, chunk))

    def row_spec(self, tile, order=lambda i: i):
        row = self.mod_row(tile)
        return lambda d: pl.BlockSpec((None, 1, d), lambda i: (row(order(i)), 0, 0))


def _ada_kernel(c_ref, w_ref, b_ref, o_ref):
    c = c_ref[...]
    s = (c * jax.nn.sigmoid(c)).astype(BF16)
    o_ref[...] = _dot(s, w_ref[...].astype(BF16)) + b_ref[...]


def _ada_tables(cvec, ada_w, ada_b):
    depth, d, n = ada_w.shape
    tn = n // 4
    return pl.pallas_call(
        _ada_kernel,
        out_shape=jax.ShapeDtypeStruct((depth, SUBLANES, n), F32),
        grid=(depth, n // tn),
        in_specs=[
            pl.BlockSpec((SUBLANES, d), lambda l, j: (0, 0)),
            pl.BlockSpec((None, d, tn), lambda l, j: (l, 0, j)),
            pl.BlockSpec((None, 1, tn), lambda l, j: (l, 0, j)),
        ],
        out_specs=pl.BlockSpec((None, SUBLANES, tn), lambda l, j: (l, 0, j)),
        compiler_params=_params(2),
        name="ada_tables",
    )(cvec, ada_w, ada_b.reshape(depth, 1, n))


def _ffn_kernel(x_ref, m_ref, wo_ref, bo_ref, g1_ref, sh_ref, sc_ref, g2_ref, nf_ref,
                wg_ref, wu_ref, wd_ref, *rest, n_chunks, final):
    if final:
        fin_ref, o_ref, a_ref = rest
    else:
        o_ref, a_ref = rest
    x1 = x_ref[...] + g1_ref[...] * (_dot(m_ref[...], wo_ref[...]) + bo_ref[...])
    h = _norm_mod(x1, nf_ref[...], sh_ref[...], sc_ref[...]).astype(BF16)
    for c in range(n_chunks):
        sl = slice(c * FFN_CHUNK, (c + 1) * FFN_CHUNK)
        g = _dot(h, wg_ref[:, sl])
        u = _dot(h, wu_ref[:, sl])
        a_ref[:, sl] = (g * jax.nn.sigmoid(g) * u).astype(BF16)
    x2 = x1 + g2_ref[...] * _dot(a_ref[...], wd_ref[...])
    if final:
        x2 = _rms(x2, fin_ref[...])
    o_ref[...] = x2


def _ffn_layer(lay, x, m, w_out, b_out, mod, norm_ffn, w_gate, w_up, w_down, final_norm=None):
    t, d = x.shape
    dff = w_gate.shape[1]
    assert dff % FFN_CHUNK == 0
    tm = FFN_TILE
    final = final_norm is not None
    tile = pl.BlockSpec((tm, d), lambda i: (i, 0))
    in_specs = [tile, tile, _resident((d, d)), _resident((1, d)),
                lay.mod_spec(tm, 2, d), lay.mod_spec(tm, 3, d), lay.mod_spec(tm, 4, d), lay.mod_spec(tm, 5, d),
                _resident((1, d)), _resident((d, dff)), _resident((d, dff)), _resident((dff, d))]
    args = [x, m, w_out, b_out, mod, mod, mod, mod, norm_ffn, w_gate, w_up, w_down]
    if final:
        in_specs.append(_resident((1, d)))
        args.append(final_norm)
    return pl.pallas_call(
        functools.partial(_ffn_kernel, n_chunks=dff // FFN_CHUNK, final=final),
        out_shape=jax.ShapeDtypeStruct((t, d), F32),
        grid=(t // tm,),
        in_specs=in_specs,
        out_specs=tile,
        scratch_shapes=[pltpu.VMEM((tm, dff), BF16)],
        compiler_params=_params(1),
        name="ffn",
    )(*args)


def _mla_qkv_kernel(x_ref, nm_ref, sh_ref, sc_ref, cos_ref, sin_ref, wdq_ref, qn_ref, wqn_ref, wqr_ref, wqx_ref,
                    wdkv_ref, wdkvx_ref, kvn_ref, q_ref, z_ref, *, scale):
    h = _norm_mod(x_ref[...], nm_ref[...], sh_ref[...], sc_ref[...]).astype(BF16)
    cos, sin = cos_ref[...], sin_ref[...]
    cq = _rms(_dot(h, wdq_ref[...]), qn_ref[...]).astype(BF16)
    q_nope = _dot(cq, wqn_ref[...])
    q_rope = _dot(cq, wqr_ref[...])
    q_swap = _dot(cq, wqx_ref[...])
    for hd in range(MLA_HEADS):
        sl = slice(hd * LANES, (hd + 1) * LANES)
        q_ref[:, hd * HEAD_W:hd * HEAD_W + LANES] = (q_nope[:, sl] * scale).astype(BF16)
        q_ref[:, hd * HEAD_W + LANES:(hd + 1) * HEAD_W] = ((q_rope[:, sl] * cos + q_swap[:, sl] * sin) * scale).astype(BF16)
    z = _dot(h, wdkv_ref[...])
    z_swap = _dot(h, wdkvx_ref[...])
    z_ref[:, :KV_LORA_RANK] = _rms(z[:, :KV_LORA_RANK], kvn_ref[...])
    z_ref[:, KV_LORA_RANK:] = z[:, KV_LORA_RANK:] * cos + z_swap * sin


def _kv_expand_kernel(z_ref, wk_ref, wv_ref, k_ref, v_ref):
    z = z_ref[...]
    ckv = z[:, :KV_LORA_RANK].astype(BF16)
    k_rope = z[:, KV_LORA_RANK:].astype(BF16)
    k_nope = _dot(ckv, wk_ref[...])
    for hd in range(MLA_HEADS):
        k_ref[:, hd * HEAD_W:hd * HEAD_W + LANES] = k_nope[:, hd * LANES:(hd + 1) * LANES].astype(BF16)
        k_ref[:, hd * HEAD_W + LANES:(hd + 1) * HEAD_W] = k_rope
    v_ref[...] = _dot(ckv, wv_ref[...]).astype(BF16)


def _kv_expand(z, wk, wv):
    n, zw = z.shape
    tm = SEQ_TILE
    return pl.pallas_call(
        _kv_expand_kernel,
        out_shape=(jax.ShapeDtypeStruct((n, MLA_HEADS * HEAD_W), BF16),
                   jax.ShapeDtypeStruct((n, MLA_HEADS * V_HEAD_DIM), BF16)),
        grid=(n // tm,),
        in_specs=[pl.BlockSpec((tm, zw), lambda i: (i, 0)), _resident(wk.shape), _resident(wv.shape)],
        out_specs=(pl.BlockSpec((tm, MLA_HEADS * HEAD_W), lambda i: (i, 0)),
                   pl.BlockSpec((tm, MLA_HEADS * V_HEAD_DIM), lambda i: (i, 0))),
        compiler_params=_params(1),
        name="mla_kv_expand",
    )(z, wk, wv)


def _attn_kernel(*refs, heads, aliased):
    q_ref, k_ref, v_ref = refs[:3]
    o_ref = refs[-1]
    for hd in range(heads):
        q = q_ref[:, hd * HEAD_W:(hd + 1) * HEAD_W]
        k = k_ref[:, hd * HEAD_W:(hd + 1) * HEAD_W]
        s = lax.dot_general(q, k, (((1,), (1,)), ((), ())), preferred_element_type=F32)
        p = jnp.exp(s - jnp.max(s, axis=-1, keepdims=True))
        den = jnp.sum(p, axis=-1, keepdims=True)
        o = _dot(p.astype(BF16), v_ref[:, hd * V_HEAD_DIM:(hd + 1) * V_HEAD_DIM])
        o_ref[:, hd * V_HEAD_DIM:(hd + 1) * V_HEAD_DIM] = (o / den).astype(BF16)


def _attention(q, k, v, out, *, n_seq, q_len, k_len, q_row0, tq, heads):
    t = q.shape[0]
    hg = MLA_HEADS // heads
    assert q_row0 % tq == 0 and q_len % tq == 0
    qb0, qt = q_row0 // tq, q_len // tq
    in_specs = [pl.BlockSpec((tq, heads * HEAD_W), lambda b, g, j: (qb0 + b * qt + j, g)),
                pl.BlockSpec((k_len, heads * HEAD_W), lambda b, g, j: (b, g)),
                pl.BlockSpec((k_len, heads * V_HEAD_DIM), lambda b, g, j: (b, g))]
    args = [q, k, v]
    aliases = {}
    if out is not None:
        in_specs.append(pl.BlockSpec(memory_space=pl.ANY))
        args.append(out)
        aliases = {3: 0}
    return pl.pallas_call(
        functools.partial(_attn_kernel, heads=heads, aliased=out is not None),
        out_shape=jax.ShapeDtypeStruct((t, MLA_HEADS * V_HEAD_DIM), BF16),
        grid=(n_seq, hg, qt),
        in_specs=in_specs,
        out_specs=pl.BlockSpec((tq, heads * V_HEAD_DIM), lambda b, g, j: (qb0 + b * qt + j, g)),
        input_output_aliases=aliases,
        compiler_params=_params(3),
        name="mla_attention",
    )(*args)


def _rope_tables(lay, tile):
    pos = np.arange(lay.ss)
    n_freq = QK_ROPE_DIM // 4
    inv_freq = ROPE_THETA ** (-np.arange(n_freq, dtype=np.float64) / n_freq)
    ang = np.concatenate([(pos // GRID_W)[:, None] * inv_freq, (pos % GRID_W)[:, None] * inv_freq], axis=-1)
    pad = np.zeros((lay.ss, LANES - QK_ROPE_DIM))
    cos = np.concatenate([np.cos(ang), np.cos(ang), pad], axis=-1)
    sin = np.concatenate([np.sin(ang), np.sin(ang), pad], axis=-1)
    ident_cos = np.concatenate([np.ones((tile, QK_ROPE_DIM)), np.zeros((tile, LANES - QK_ROPE_DIM))], axis=-1)
    cos = np.concatenate([ident_cos, cos], axis=0)
    sin = np.concatenate([np.zeros((tile, LANES)), sin], axis=0)
    return jnp.asarray(cos, F32), jnp.asarray(sin, F32)


def _swap_halves(w):
    half = QK_ROPE_DIM // 2
    return jnp.concatenate([-w[..., half:], w[..., :half]], axis=-1)


def _mla_layer(lay, x, mod, norm_mix, cache_ckv, cache_krope, w_dq, q_norm, w_uq, w_dkv, kv_norm, w_ukv):
    t, d = x.shape
    tm = SEQ_TILE
    rq = w_dq.shape[1]
    qk = QK_NOPE_DIM + QK_ROPE_DIM
    zw = KV_LORA_RANK + LANES
    rope_pad = [(0, 0)] * 2 + [(0, LANES - QK_ROPE_DIM)]

    wq = w_uq.reshape(rq, MLA_HEADS, qk)
    wq_nope = wq[:, :, :QK_NOPE_DIM].reshape(rq, -1).astype(BF16)
    wq_rope = jnp.pad(wq[:, :, QK_NOPE_DIM:], rope_pad).reshape(rq, -1).astype(BF16)
    wq_swap = jnp.pad(_swap_halves(wq[:, :, QK_NOPE_DIM:]), rope_pad).reshape(rq, -1).astype(BF16)
    wdkv = jnp.pad(w_dkv, [(0, 0), (0, LANES - QK_ROPE_DIM)]).astype(BF16)
    wdkv_swap = jnp.pad(_swap_halves(w_dkv[:, KV_LORA_RANK:]), [(0, 0), (0, LANES - QK_ROPE_DIM)]).astype(BF16)
    wkv = w_ukv.reshape(KV_LORA_RANK, MLA_HEADS, QK_NOPE_DIM + V_HEAD_DIM)
    wk = wkv[:, :, :QK_NOPE_DIM].reshape(KV_LORA_RANK, -1).astype(BF16)
    wv = wkv[:, :, QK_NOPE_DIM:].reshape(KV_LORA_RANK, -1).astype(BF16)

    cos, sin = _rope_tables(lay, tm)
    npt, spb = lay.prompt_tiles(tm), lay.sample_tiles_per_seq(tm)
    rope_spec = pl.BlockSpec((tm, LANES), lambda i: (jnp.where(i < npt, 0, 1 + (i - npt) % spb), 0))
    q, z = pl.pallas_call(
        functools.partial(_mla_qkv_kernel, scale=qk ** -0.5),
        out_shape=(jax.ShapeDtypeStruct((t, MLA_HEADS * HEAD_W), BF16), jax.ShapeDtypeStruct((t, zw), F32)),
        grid=(t // tm,),
        in_specs=[pl.BlockSpec((tm, d), lambda i: (i, 0)), _resident((1, d)),
                  lay.mod_spec(tm, 0, d), lay.mod_spec(tm, 1, d), rope_spec, rope_spec,
                  _resident(w_dq.shape), _resident((1, rq)), _resident(wq_nope.shape), _resident(wq_rope.shape),
                  _resident(wq_swap.shape), _resident(wdkv.shape), _resident(wdkv_swap.shape),
                  _resident((1, KV_LORA_RANK))],
        out_specs=(pl.BlockSpec((tm, MLA_HEADS * HEAD_W), lambda i: (i, 0)), pl.BlockSpec((tm, zw), lambda i: (i, 0))),
        compiler_params=_params(1),
        name="mla_qkv",
    )(x, norm_mix, mod, mod, cos, sin, w_dq.astype(BF16), q_norm, wq_nope, wq_rope, wq_swap, wdkv, wdkv_swap, kv_norm)

    past = cache_ckv.shape[1]
    z_cache = jnp.concatenate(
        [cache_ckv, cache_krope, jnp.zeros((lay.bs, past, LANES - QK_ROPE_DIM), F32)], axis=-1)
    z_sample = jnp.concatenate([z_cache, z[lay.tp:].reshape(lay.bs, lay.ss, zw)], axis=1)
    k_len = past + lay.ss
    kp, vp = _kv_expand(z[:lay.tp], wk, wv)
    ks, vs = _kv_expand(z_sample.reshape(lay.bs * k_len, zw), wk, wv)

    o = _attention(q, kp, vp, None, n_seq=lay.bp, q_len=lay.sp, k_len=lay.sp, q_row0=0, tq=lay.sp, heads=MLA_HEADS)
    o = _attention(q, ks, vs, o, n_seq=lay.bs, q_len=lay.ss, k_len=k_len, q_row0=lay.tp,
                   tq=min(ATTN_Q_TILE, lay.ss), heads=1)
    ckv_state = z[:lay.tp, :KV_LORA_RANK].reshape(lay.bp, 1, lay.sp, KV_LORA_RANK)
    krope_state = z[:lay.tp, KV_LORA_RANK:KV_LORA_RANK + QK_ROPE_DIM].reshape(lay.bp, 1, lay.sp, QK_ROPE_DIM)
    return o, ckv_state, krope_state


def _seq_position(lay, tile, t):
    npt, spb = lay.prompt_tiles(tile), lay.sample_tiles_per_seq(tile)
    ppb = lay.sp // tile
    r = jnp.where(t < npt, t % ppb, (t - npt) % spb)
    n = jnp.where(t < npt, ppb, spb)
    return r, n


def _softplus(x):
    return jnp.maximum(x, 0.0) + jnp.log1p(jnp.exp(-jnp.abs(x)))


def _rglru_kernel(*refs, lay, n_tiles, reverse):
    (xp_ref, xc_ref, xn_ref, nm_ref, sh_ref, sc_ref, h0_ref, wx_ref, cw_ref, cb_ref,
     wa_ref, ba_ref, wi_ref, bi_ref, lam_ref) = refs[:15]
    if reverse:
        wy_ref, yf_ref, o_ref, hl_ref, xw_scr, a_scr, u_scr, y_scr, carry_scr = refs[15:]
    else:
        o_ref, hl_ref, xw_scr, a_scr, u_scr, carry_scr = refs[15:]
    tm, d = xc_ref.shape
    groups = tm // SUBLANES
    bw = d // RG_BLOCKS
    i = pl.program_id(0)
    t = n_tiles - 1 - i if reverse else i
    r, n = _seq_position(lay, tm, t)
    has_prev, has_next = r > 0, r < n - 1

    @pl.when(i == 0)
    def _():
        carry_scr[...] = jnp.zeros_like(carry_scr)

    def pre(ref):
        return _norm_mod(ref[...], nm_ref[...], sh_ref[...], sc_ref[...]).astype(BF16)

    hc = pre(xc_ref)
    wx = wx_ref[...]
    xw_scr[0:SUBLANES] = jnp.where(has_prev, _dot(pre(xp_ref), wx), 0.0)
    xw_scr[SUBLANES:SUBLANES + tm] = _dot(hc, wx)
    xw_scr[SUBLANES + tm:] = jnp.where(has_next, _dot(pre(xn_ref), wx), 0.0)
    xc = cb_ref[...]
    for k in range(RG_CONV_W):
        xc = xc + cw_ref[k:k + 1, :] * xw_scr[pl.ds(SUBLANES - 1 + k, tm), :]

    row = lax.broadcasted_iota(jnp.int32, (groups, SUBLANES, bw), 1)
    for nb in range(RG_BLOCKS):
        sl = slice(nb * bw, (nb + 1) * bw)
        xcn = xc[:, sl]
        xcb = xcn.astype(BF16)
        rg = jax.nn.sigmoid(_dot(xcb, wa_ref[nb]) + ba_ref[:, sl])
        ig = jax.nn.sigmoid(_dot(xcb, wi_ref[nb]) + bi_ref[:, sl])
        log_a = -RG_C * rg * _softplus(-lam_ref[:, sl])
        a = jnp.exp(log_a)
        u = (jnp.sqrt(-jnp.tanh(log_a) * (a * a + 1.0)) * (ig * xcn)).reshape(groups, SUBLANES, bw)
        a = a.reshape(groups, SUBLANES, bw)
        for k in (1, 2, 4):
            shift = SUBLANES - k if reverse else k
            keep = row < SUBLANES - k if reverse else row >= k
            a_nb = pltpu.roll(a, shift, 1)
            u_nb = pltpu.roll(u, shift, 1)
            u = jnp.where(keep, a * u_nb + u, u)
            a = jnp.where(keep, a * a_nb, a)
        a_scr[:, :, sl] = a
        u_scr[:, :, sl] = u

    is_start = r == n - 1 if reverse else r == 0
    h_init = jnp.where(is_start, h0_ref[...], carry_scr[...])
    dst = y_scr if reverse else o_ref
    edge = 0 if reverse else SUBLANES - 1

    def step(g, h):
        gg = groups - 1 - g if reverse else g
        y = a_scr[gg] * h + u_scr[gg]
        dst[pl.ds(pl.multiple_of(gg * SUBLANES, SUBLANES), SUBLANES), :] = y
        return y[edge:edge + 1, :]

    h_end = lax.fori_loop(0, groups, step, h_init)
    carry_scr[...] = h_end
    hl_ref[...] = h_end
    if reverse:
        gate = jax.nn.gelu(_dot(hc, wy_ref[...]))
        o_ref[...] = ((yf_ref[...] + y_scr[...]) * gate).astype(BF16)


def _rglru_layer(lay, x, mod, norm_mix, state, w_x, w_y, conv_w, conv_b, w_a, b_a, w_i, b_i, lam):
    t, d = x.shape
    tm = SEQ_TILE
    n_tiles = t // tm
    hb = tm // SUBLANES
    n_halo = t // SUBLANES
    bw = d // RG_BLOCKS
    wx, wy = w_x.astype(BF16), w_y.astype(BF16)

    def run(reverse, y_fwd=None):
        dr = int(reverse)
        order = (lambda i: n_tiles - 1 - i) if reverse else (lambda i: i)
        h0 = jnp.concatenate([jnp.zeros((1, d), F32), state[:, dr], jnp.zeros((SUBLANES - 1 - lay.bs, d), F32)])
        row_spec = lay.row_spec(tm, order)
        tile = pl.BlockSpec((tm, d), lambda i: (order(i), 0))
        in_specs = [pl.BlockSpec((SUBLANES, d), lambda i: (jnp.maximum(order(i) * hb - 1, 0), 0)),
                    tile,
                    pl.BlockSpec((SUBLANES, d), lambda i: (jnp.minimum((order(i) + 1) * hb, n_halo - 1), 0)),
                    _resident((1, d)), lay.mod_spec(tm, 0, d, order), lay.mod_spec(tm, 1, d, order), row_spec(d),
                    _resident((d, d)), _resident((RG_CONV_W, d)), _resident((1, d)),
                    _resident((RG_BLOCKS, bw, bw)), _resident((1, d)), _resident((RG_BLOCKS, bw, bw)), _resident((1, d)),
                    _resident((1, d))]
        args = [x, x, x, norm_mix, mod, mod, h0.reshape(SUBLANES, 1, d), wx, conv_w, conv_b,
                w_a[dr].astype(BF16), b_a[dr:dr + 1], w_i[dr].astype(BF16), b_i[dr:dr + 1], lam[dr:dr + 1]]
        scratch = [pltpu.VMEM((tm + 2 * SUBLANES, d), F32), pltpu.VMEM((hb, SUBLANES, d), F32),
                   pltpu.VMEM((hb, SUBLANES, d), F32)]
        if reverse:
            in_specs += [_resident((d, d)), tile]
            args += [wy, y_fwd]
            scratch.append(pltpu.VMEM((tm, d), F32))
        scratch.append(pltpu.VMEM((1, d), F32))
        return pl.pallas_call(
            functools.partial(_rglru_kernel, lay=lay, n_tiles=n_tiles, reverse=reverse),
            out_shape=(jax.ShapeDtypeStruct((t, d), BF16 if reverse else F32),
                       jax.ShapeDtypeStruct((n_tiles, 1, d), F32)),
            grid=(n_tiles,),
            in_specs=in_specs,
            out_specs=(tile, pl.BlockSpec((None, 1, d), lambda i: (order(i), 0, 0))),
            scratch_shapes=scratch,
            compiler_params=_params(1, "arbitrary"),
            name="rglru_bwd" if reverse else "rglru_fwd",
        )(*args)

    y_fwd, last_f = run(False)
    m, last_b = run(True, y_fwd)
    ppb = lay.sp // tm
    tail = last_f[:lay.prompt_tiles(tm)].reshape(lay.bp, ppb, d)[:, ppb - 1]
    head = last_b[:lay.prompt_tiles(tm)].reshape(lay.bp, ppb, d)[:, 0]
    return m, jnp.stack([tail, head], axis=1)[:, None]


def _dft_tables(n, scale):
    jk = np.outer(np.arange(n), np.arange(n)) % n
    ang = 2.0 * np.pi * jk / n
    return np.cos(ang) * scale, np.sin(ang) * scale


def _fnet_channel_kernel(x_ref, nm_ref, sh_ref, sc_ref, w_ref, zc_ref, zs_ref):
    h = _norm_mod(x_ref[...], nm_ref[...], sh_ref[...], sc_ref[...]).astype(BF16)
    gw = w_ref.shape[0]
    for g in range(FNET_GROUPS):
        sl = slice(g * gw, (g + 1) * gw)
        f = _dot(h[:, sl], w_ref[...])
        zc_ref[:, sl] = f[:, :gw].astype(BF16)
        zs_ref[:, sl] = f[:, gw:].astype(BF16)


def _fnet_position_kernel(*refs):
    wc_ref, ws_ref, zc_ref, zs_ref = refs[:4]
    o_ref = refs[-1]
    o_ref[...] = (_dot(wc_ref[...], zc_ref[...]) - _dot(ws_ref[...], zs_ref[...])).astype(BF16)


def _fnet_position(zc, zs, out, *, n_seq, seq_len, row0, tm):
    t, d = zc.shape
    assert row0 % seq_len == 0 and seq_len % tm == 0
    cos, sin = _dft_tables(seq_len, seq_len ** -0.5)
    wc, ws = jnp.asarray(cos, F32).astype(BF16), jnp.asarray(sin, F32).astype(BF16)
    sb0, rt = row0 // seq_len, seq_len // tm
    w_spec = pl.BlockSpec((tm, seq_len), lambda b, j: (j, 0))
    z_spec = pl.BlockSpec((seq_len, d), lambda b, j: (sb0 + b, 0))
    in_specs = [w_spec, w_spec, z_spec, z_spec]
    args = [wc, ws, zc, zs]
    aliases = {}
    if out is not None:
        in_specs.append(pl.BlockSpec(memory_space=pl.ANY))
        args.append(out)
        aliases = {4: 0}
    return pl.pallas_call(
        _fnet_position_kernel,
        out_shape=jax.ShapeDtypeStruct((t, d), BF16),
        grid=(n_seq, rt),
        in_specs=in_specs,
        out_specs=pl.BlockSpec((tm, d), lambda b, j: ((sb0 + b) * rt + j, 0)),
        input_output_aliases=aliases,
        compiler_params=_params(2),
        name="fnet_position",
    )(*args)


def _fnet_layer(lay, x, mod, norm_mix):
    t, d = x.shape
    tm = SEQ_TILE
    gw = d // FNET_GROUPS
    cos, sin = _dft_tables(gw, gw ** -0.5)
    w = jnp.asarray(np.concatenate([cos, sin], axis=1), F32).astype(BF16)
    tile = pl.BlockSpec((tm, d), lambda i: (i, 0))
    zc, zs = pl.pallas_call(
        _fnet_channel_kernel,
        out_shape=(jax.ShapeDtypeStruct((t, d), BF16), jax.ShapeDtypeStruct((t, d), BF16)),
        grid=(t // tm,),
        in_specs=[tile, _resident((1, d)), lay.mod_spec(tm, 0, d), lay.mod_spec(tm, 1, d), _resident(w.shape)],
        out_specs=(tile, tile),
        compiler_params=_params(1),
        name="fnet_channel",
    )(x, norm_mix, mod, mod, w)
    m = _fnet_position(zc, zs, None, n_seq=lay.bp, seq_len=lay.sp, row0=0, tm=lay.sp)
    return _fnet_position(zc, zs, m, n_seq=lay.bs, seq_len=lay.ss, row0=lay.tp, tm=min(FNET_ROW_TILE, lay.ss))


def _conformer_kernel(xp_ref, xc_ref, xn_ref, nm_ref, sh_ref, sc_ref, w1_ref, b1_ref, dw_ref, db_ref, lg_ref, lb_ref,
                      o_ref, glu_scr, *, lay):
    tm, d = xc_ref.shape
    r, n = _seq_position(lay, tm, pl.program_id(0))
    has_prev, has_next = r > 0, r < n - 1

    def glu(ref):
        h = _norm_mod(ref[...], nm_ref[...], sh_ref[...], sc_ref[...]).astype(BF16)
        z = _dot(h, w1_ref[...]) + b1_ref[...]
        return z[:, :d] * jax.nn.sigmoid(z[:, d:])

    glu_scr[0:CONF_HALO] = jnp.where(has_prev, glu(xp_ref), 0.0)
    glu_scr[CONF_HALO:CONF_HALO + tm] = glu(xc_ref)
    glu_scr[CONF_HALO + tm:] = jnp.where(has_next, glu(xn_ref), 0.0)
    pad_left = (CONF_CONV_W - 1) // 2
    acc = db_ref[...]
    for k in range(CONF_CONV_W):
        acc = acc + dw_ref[k:k + 1, :] * glu_scr[pl.ds(CONF_HALO - pad_left + k, tm), :]
    mu = jnp.mean(acc, axis=-1, keepdims=True)
    cen = acc - mu
    var = jnp.mean(cen * cen, axis=-1, keepdims=True)
    y = cen * lax.rsqrt(var + EPS) * lg_ref[...] + lb_ref[...]
    o_ref[...] = (y * jax.nn.sigmoid(y)).astype(BF16)


def _conformer_layer(lay, x, mod, norm_mix, w_pw1, b_pw1, dw_w, dw_b, ln_g, ln_b):
    t, d = x.shape
    tm = SEQ_TILE
    hb = tm // CONF_HALO
    n_halo = t // CONF_HALO
    tile = pl.BlockSpec((tm, d), lambda i: (i, 0))
    return pl.pallas_call(
        functools.partial(_conformer_kernel, lay=lay),
        out_shape=jax.ShapeDtypeStruct((t, d), BF16),
        grid=(t // tm,),
        in_specs=[pl.BlockSpec((CONF_HALO, d), lambda i: (jnp.maximum(i * hb - 1, 0), 0)),
                  tile,
                  pl.BlockSpec((CONF_HALO, d), lambda i: (jnp.minimum((i + 1) * hb, n_halo - 1), 0)),
                  _resident((1, d)), lay.mod_spec(tm, 0, d), lay.mod_spec(tm, 1, d),
                  _resident((d, 2 * d)), _resident((1, 2 * d)), _resident((CONF_CONV_W, d)), _resident((1, d)),
                  _resident((1, d)), _resident((1, d))],
        out_specs=tile,
        scratch_shapes=[pltpu.VMEM((tm + 2 * CONF_HALO, d), F32)],
        compiler_params=_params(1),
        name="conformer",
    )(x, x, x, norm_mix, mod, mod, w_pw1.astype(BF16), b_pw1, dw_w, dw_b, ln_g, ln_b)


def kernel(x_prompt, x_sample, cache_mla_ckv, cache_mla_krope, state_rglru, c, c_ctx, ada_w, ada_b, norm_mix, norm_ffn, mla_w_dq, mla_q_norm, mla_w_uq, mla_w_dkv, mla_kv_norm, mla_w_ukv, mla_w_o, rg_w_x, rg_w_y, rg_conv_w, rg_conv_b, rg_w_a, rg_b_a, rg_w_i, rg_b_i, rg_lam, rg_w_o, fn_w_o, fn_b_o, cf_w_pw1, cf_b_pw1, cf_dw_w, cf_dw_b, cf_ln_g, cf_ln_b, cf_w_pw2, cf_b_pw2, ffn_w_gate, ffn_w_up, ffn_w_down, final_norm):
    bp, sp, d = x_prompt.shape
    bs, ss, _ = x_sample.shape
    depth = ada_w.shape[0]
    assert depth == 4 and bs < SUBLANES and sp == SEQ_TILE and ss % GRID_W == 0
    lay = _Layout(bp, sp, bs, ss)
    x = jnp.concatenate([x_prompt.reshape(lay.tp, d), x_sample.reshape(lay.ts, d)], axis=0)

    cvec = jnp.concatenate([c_ctx[None], c, jnp.zeros((SUBLANES - 1 - bs, d), F32)], axis=0)
    mods = _ada_tables(cvec, ada_w, ada_b).reshape(depth, SUBLANES, 1, -1)
    zero_bias = jnp.zeros((1, d), F32)

    def ffn(layer, x, m, w_out, b_out, final=None):
        return _ffn_layer(lay, x, m, w_out.astype(BF16), b_out, mods[layer], norm_ffn[layer:layer + 1],
                          ffn_w_gate[layer].astype(BF16), ffn_w_up[layer].astype(BF16),
                          ffn_w_down[layer].astype(BF16), final)

    m, ckv_state, krope_state = _mla_layer(
        lay, x, mods[0], norm_mix[0:1], cache_mla_ckv[:, 0], cache_mla_krope[:, 0], mla_w_dq[0], mla_q_norm[0:1],
        mla_w_uq[0], mla_w_dkv[0], mla_kv_norm[0:1], mla_w_ukv[0])
    x = ffn(0, x, m, mla_w_o[0], zero_bias)

    m, rg_state = _rglru_layer(lay, x, mods[1], norm_mix[1:2], state_rglru[:, 0], rg_w_x[0], rg_w_y[0], rg_conv_w[0],
                               rg_conv_b[0:1], rg_w_a[0], rg_b_a[0], rg_w_i[0], rg_b_i[0], rg_lam[0])
    x = ffn(1, x, m, rg_w_o[0], zero_bias)

    m = _fnet_layer(lay, x, mods[2], norm_mix[2:3])
    x = ffn(2, x, m, fn_w_o[0], fn_b_o[0:1])

    m = _conformer_layer(lay, x, mods[3], norm_mix[3:4], cf_w_pw1[0], cf_b_pw1[0:1], cf_dw_w[0], cf_dw_b[0:1],
                         cf_ln_g[0:1], cf_ln_b[0:1])
    y = ffn(3, x, m, cf_w_pw2[0], cf_b_pw2[0:1], final_norm[None])

    return (y[:lay.tp].reshape(bp, sp, d), y[lay.tp:].reshape(bs, ss, d), ckv_state, krope_state, rg_state)
```

```python
import functools

import numpy as np
import jax
import jax.numpy as jnp
from jax import lax
from jax.experimental import pallas as pl
from jax.experimental.pallas import tpu as pltpu

F32 = jnp.float32
BF16 = jnp.bfloat16

EPS = 1e-6
GRID_W = 64
MLA_HEADS = 8
KV_LORA_RANK = 256
QK_NOPE_DIM = 128
QK_ROPE_DIM = 64
V_HEAD_DIM = 128
ROPE_THETA = 10000.0
RG_BLOCKS = 4
RG_CONV_W = 4
RG_C = 8.0
FNET_GROUPS = 4
CONF_CONV_W = 31

LANES = 128
SUBLANES = 8
HEAD_W = 2 * LANES
SEQ_TILE = 256
FFN_TILE = 512
ATTN_Q_TILE = 1024
ATTN_SUB_TILE = 256
FNET_ROW_TILE = 512
FFN_CHUNK = 256
CONF_HALO = 16
CONF_CHAINS = 4
VMEM_LIMIT = 52 * 1024 * 1024


def _dot(a, b):
    return jnp.dot(a, b, preferred_element_type=F32)


def _rms(x, g):
    return x * lax.rsqrt(jnp.mean(x * x, axis=-1, keepdims=True) + EPS) * g


def _norm_mod(x, g, shift, scale):
    return _rms(x, g) * (1.0 + scale) + shift


def _resident(shape):
    nd = len(shape)
    return pl.BlockSpec(shape, lambda *_: (0,) * nd, pipeline_mode=pl.Buffered(1))


def _resident_at(index, shape):
    nd = len(shape)
    return pl.BlockSpec((None,) + tuple(shape), lambda *_: (index,) + (0,) * nd, pipeline_mode=pl.Buffered(1))


def _params(n_axes, semantics="parallel"):
    return pltpu.CompilerParams(dimension_semantics=(semantics,) * n_axes, vmem_limit_bytes=VMEM_LIMIT)


class _Layout:
    def __init__(self, n_prompt_seq, prompt_len, n_sample_seq, sample_len):
        self.bp, self.sp, self.bs, self.ss = n_prompt_seq, prompt_len, n_sample_seq, sample_len
        self.tp = n_prompt_seq * prompt_len
        self.ts = n_sample_seq * sample_len
        self.t = self.tp + self.ts

    def prompt_tiles(self, tile):
        assert self.tp % tile == 0 and self.ss % tile == 0
        return self.tp // tile

    def sample_tiles_per_seq(self, tile):
        return self.ss // tile

    def mod_row(self, tile):
        npt, spb = self.prompt_tiles(tile), self.sample_tiles_per_seq(tile)
        return lambda i: jnp.where(i < npt, 0, 1 + (i - npt) // spb)

    def mod_spec(self, tile, chunk, d, order=lambda i: i):
        row = self.mod_row(tile)
        return pl.BlockSpec((None, 1, d), lambda i: (row(order(i)), 0, chunk))

    def row_spec(self, tile, order=lambda i: i):
        row = self.mod_row(tile)
        return lambda d: pl.BlockSpec((None, 1, d), lambda i: (row(order(i)), 0, 0))


def _ada_kernel(c_ref, w_ref, b_ref, o_ref):
    c = c_ref[...]
    s = (c * jax.nn.sigmoid(c)).astype(BF16)
    o_ref[...] = _dot(s, w_ref[...].astype(BF16)) + b_ref[...]


def _ada_tables(cvec, ada_w, ada_b):
    depth, d, n = ada_w.shape
    tn = n // 4
    return pl.pallas_call(
        _ada_kernel,
        out_shape=jax.ShapeDtypeStruct((depth, SUBLANES, n), F32),
        grid=(depth, n // tn),
        in_specs=[
            pl.BlockSpec((SUBLANES, d), lambda l, j: (0, 0)),
            pl.BlockSpec((None, d, tn), lambda l, j: (l, 0, j)),
            pl.BlockSpec((None, 1, tn), lambda l, j: (l, 0, j)),
        ],
        out_specs=pl.BlockSpec((None, SUBLANES, tn), lambda l, j: (l, 0, j)),
        compiler_params=_params(2),
        name="ada_tables",
    )(cvec, ada_w, ada_b.reshape(depth, 1, n))


def _read_tokens(refs, n_prompt_tiles):
    if len(refs) == 1:
        return refs[0][...]
    return jnp.where(pl.program_id(0) < n_prompt_tiles, refs[0][...], refs[1][...])


def _ffn_kernel(*refs, n_chunks, n_x, final, n_prompt_tiles):
    x_refs, refs = refs[:n_x], refs[n_x:]
    m_ref, wo_ref, bo_ref, g1_ref, sh_ref, sc_ref, g2_ref, nf_ref, wg_ref, wu_ref, wd_ref = refs[:11]
    a_ref = refs[-1]
    x1 = _read_tokens(x_refs, n_prompt_tiles) + g1_ref[...] * (_dot(m_ref[...], wo_ref[...]) + bo_ref[...])
    h = _norm_mod(x1, nf_ref[...], sh_ref[...], sc_ref[...]).astype(BF16)
    for c in range(n_chunks):
        sl = slice(c * FFN_CHUNK, (c + 1) * FFN_CHUNK)
        g = _dot(h, wg_ref[:, sl])
        u = _dot(h, wu_ref[:, sl])
        a_ref[:, sl] = (g * jax.nn.sigmoid(g) * u).astype(BF16)
    x2 = x1 + g2_ref[...] * _dot(a_ref[...], wd_ref[...])
    if not final:
        refs[11][...] = x2
        return
    fin_ref, op_ref, os_ref = refs[11:14]
    y = _rms(x2, fin_ref[...])
    is_prompt = pl.program_id(0) < n_prompt_tiles

    @pl.when(is_prompt)
    def _():
        op_ref[...] = y

    @pl.when(jnp.logical_not(is_prompt))
    def _():
        os_ref[...] = y


def _ffn_layer(lay, layer, x, m, w_out, b_out, mod, norm_ffn, w_gate, w_up, w_down, final_norm=None):
    xs = x if isinstance(x, tuple) else (x,)
    d = xs[0].shape[1]
    dff = w_gate.shape[2]
    assert dff % FFN_CHUNK == 0
    tm = FFN_TILE
    npt = lay.prompt_tiles(tm)
    final = final_norm is not None
    tile = pl.BlockSpec((tm, d), lambda i: (i, 0))
    split = [pl.BlockSpec((tm, d), lambda i: (jnp.minimum(i, npt - 1), 0)),
             pl.BlockSpec((tm, d), lambda i: (jnp.maximum(i - npt, 0), 0))]
    in_specs = (split if len(xs) == 2 else [tile]) + [
        tile, _resident((d, d)), _resident((1, d)),
        lay.mod_spec(tm, 2, d), lay.mod_spec(tm, 3, d), lay.mod_spec(tm, 4, d), lay.mod_spec(tm, 5, d),
        _resident_at(layer, (1, d)), _resident_at(layer, (d, dff)), _resident_at(layer, (d, dff)),
        _resident_at(layer, (dff, d))]
    args = list(xs) + [m, w_out, b_out, mod, mod, mod, mod, norm_ffn, w_gate, w_up, w_down]
    out_shape, out_specs = jax.ShapeDtypeStruct((lay.t, d), F32), tile
    if final:
        in_specs.append(_resident((1, d)))
        args.append(final_norm)
        out_shape = (jax.ShapeDtypeStruct((lay.tp, d), F32), jax.ShapeDtypeStruct((lay.ts, d), F32))
        out_specs = tuple(split)
    return pl.pallas_call(
        functools.partial(_ffn_kernel, n_chunks=dff // FFN_CHUNK, n_x=len(xs), final=final, n_prompt_tiles=npt),
        out_shape=out_shape,
        grid=(lay.t // tm,),
        in_specs=in_specs,
        out_specs=out_specs,
        scratch_shapes=[pltpu.VMEM((tm, dff), BF16)],
        compiler_params=_params(1, "arbitrary" if final else "parallel"),
        name="ffn",
    )(*args)


def _mla_qkv_kernel(xp_ref, xs_ref, nm_ref, sh_ref, sc_ref, cos_ref, sin_ref, wdq_ref, qn_ref, wqn_ref, wqr_ref, wqx_ref,
                    wdkv_ref, wdkvx_ref, kvn_ref, q_ref, z_ref, *, scale, n_prompt_tiles):
    x = _read_tokens((xp_ref, xs_ref), n_prompt_tiles)
    h = _norm_mod(x, nm_ref[...], sh_ref[...], sc_ref[...]).astype(BF16)
    cos, sin = cos_ref[...], sin_ref[...]
    cq = _rms(_dot(h, wdq_ref[...]), qn_ref[...]).astype(BF16)
    q_nope = _dot(cq, wqn_ref[...])
    q_rope = _dot(cq, wqr_ref[...])
    q_swap = _dot(cq, wqx_ref[...])
    for hd in range(MLA_HEADS):
        sl = slice(hd * LANES, (hd + 1) * LANES)
        q_ref[:, hd * HEAD_W:hd * HEAD_W + LANES] = (q_nope[:, sl] * scale).astype(BF16)
        q_ref[:, hd * HEAD_W + LANES:(hd + 1) * HEAD_W] = ((q_rope[:, sl] * cos + q_swap[:, sl] * sin) * scale).astype(BF16)
    z = _dot(h, wdkv_ref[...])
    z_swap = _dot(h, wdkvx_ref[...])
    z_ref[:, :KV_LORA_RANK] = _rms(z[:, :KV_LORA_RANK], kvn_ref[...])
    z_ref[:, KV_LORA_RANK:] = z[:, KV_LORA_RANK:] * cos + z_swap * sin


def _kv_expand_kernel(z_ref, wk_ref, wv_ref, k_ref, v_ref):
    z = z_ref[...]
    ckv = z[:, :KV_LORA_RANK].astype(BF16)
    k_rope = z[:, KV_LORA_RANK:].astype(BF16)
    k_nope = _dot(ckv, wk_ref[...])
    for hd in range(MLA_HEADS):
        k_ref[:, hd * HEAD_W:hd * HEAD_W + LANES] = k_nope[:, hd * LANES:(hd + 1) * LANES].astype(BF16)
        k_ref[:, hd * HEAD_W + LANES:(hd + 1) * HEAD_W] = k_rope
    v_ref[...] = _dot(ckv, wv_ref[...]).astype(BF16)


def _kv_expand(z, wk, wv, n=None):
    zw = z.shape[1]
    n = z.shape[0] if n is None else n
    tm = SEQ_TILE
    return pl.pallas_call(
        _kv_expand_kernel,
        out_shape=(jax.ShapeDtypeStruct((n, MLA_HEADS * HEAD_W), BF16),
                   jax.ShapeDtypeStruct((n, MLA_HEADS * V_HEAD_DIM), BF16)),
        grid=(n // tm,),
        in_specs=[pl.BlockSpec((tm, zw), lambda i: (i, 0)), _resident(wk.shape), _resident(wv.shape)],
        out_specs=(pl.BlockSpec((tm, MLA_HEADS * HEAD_W), lambda i: (i, 0)),
                   pl.BlockSpec((tm, MLA_HEADS * V_HEAD_DIM), lambda i: (i, 0))),
        compiler_params=_params(1),
        name="mla_kv_expand",
    )(z, wk, wv)


def _attn_kernel(*refs, heads, sub):
    q_ref, k_ref, v_ref = refs[:3]
    o_ref = refs[-1]
    tq = q_ref.shape[0]
    for hd in range(heads):
        k = k_ref[:, hd * HEAD_W:(hd + 1) * HEAD_W]
        v = v_ref[:, hd * V_HEAD_DIM:(hd + 1) * V_HEAD_DIM]
        for j in range(tq // sub):
            rows = slice(j * sub, (j + 1) * sub)
            q = q_ref[rows, hd * HEAD_W:(hd + 1) * HEAD_W]
            s = lax.dot_general(q, k, (((1,), (1,)), ((), ())), preferred_element_type=F32)
            p = jnp.exp(s - jnp.max(s, axis=-1, keepdims=True))
            den = jnp.sum(p, axis=-1, keepdims=True)
            o = _dot(p.astype(BF16), v)
            o_ref[rows, hd * V_HEAD_DIM:(hd + 1) * V_HEAD_DIM] = (o / den).astype(BF16)


def _attention(q, k, v, out, *, n_seq, q_len, k_len, q_row0, tq, heads):
    t = q.shape[0]
    hg = MLA_HEADS // heads
    assert q_row0 % tq == 0 and q_len % tq == 0
    qb0, qt = q_row0 // tq, q_len // tq
    in_specs = [pl.BlockSpec((tq, heads * HEAD_W), lambda b, g, j: (qb0 + b * qt + j, g)),
                pl.BlockSpec((k_len, heads * HEAD_W), lambda b, g, j: (b, g)),
                pl.BlockSpec((k_len, heads * V_HEAD_DIM), lambda b, g, j: (b, g))]
    args = [q, k, v]
    aliases = {}
    if out is not None:
        in_specs.append(pl.BlockSpec(memory_space=pl.ANY))
        args.append(out)
        aliases = {3: 0}
    return pl.pallas_call(
        functools.partial(_attn_kernel, heads=heads, sub=min(ATTN_SUB_TILE, tq)),
        out_shape=jax.ShapeDtypeStruct((t, MLA_HEADS * V_HEAD_DIM), BF16),
        grid=(n_seq, hg, qt),
        in_specs=in_specs,
        out_specs=pl.BlockSpec((tq, heads * V_HEAD_DIM), lambda b, g, j: (qb0 + b * qt + j, g)),
        input_output_aliases=aliases,
        compiler_params=_params(3),
        name="mla_attention",
    )(*args)


def _rope_tables(lay, tile):
    pos = np.arange(lay.ss)
    n_freq = QK_ROPE_DIM // 4
    inv_freq = ROPE_THETA ** (-np.arange(n_freq, dtype=np.float64) / n_freq)
    ang = np.concatenate([(pos // GRID_W)[:, None] * inv_freq, (pos % GRID_W)[:, None] * inv_freq], axis=-1)
    pad = np.zeros((lay.ss, LANES - QK_ROPE_DIM))
    cos = np.concatenate([np.cos(ang), np.cos(ang), pad], axis=-1)
    sin = np.concatenate([np.sin(ang), np.sin(ang), pad], axis=-1)
    ident_cos = np.concatenate([np.ones((tile, QK_ROPE_DIM)), np.zeros((tile, LANES - QK_ROPE_DIM))], axis=-1)
    cos = np.concatenate([ident_cos, cos], axis=0)
    sin = np.concatenate([np.zeros((tile, LANES)), sin], axis=0)
    return jnp.asarray(cos, F32), jnp.asarray(sin, F32)


def _swap_halves(w):
    half = QK_ROPE_DIM // 2
    return jnp.concatenate([-w[..., half:], w[..., :half]], axis=-1)


def _mla_layer(lay, x_prompt, x_sample, mod, norm_mix, cache_ckv, cache_krope, w_dq, q_norm, w_uq, w_dkv, kv_norm,
               w_ukv):
    t, d = lay.t, x_prompt.shape[1]
    tm = SEQ_TILE
    rq = w_dq.shape[1]
    qk = QK_NOPE_DIM + QK_ROPE_DIM
    zw = KV_LORA_RANK + LANES
    rope_pad = [(0, 0)] * 2 + [(0, LANES - QK_ROPE_DIM)]

    wq = w_uq.reshape(rq, MLA_HEADS, qk)
    wq_nope = wq[:, :, :QK_NOPE_DIM].reshape(rq, -1).astype(BF16)
    wq_rope = jnp.pad(wq[:, :, QK_NOPE_DIM:], rope_pad).reshape(rq, -1).astype(BF16)
    wq_swap = jnp.pad(_swap_halves(wq[:, :, QK_NOPE_DIM:]), rope_pad).reshape(rq, -1).astype(BF16)
    wdkv = jnp.pad(w_dkv, [(0, 0), (0, LANES - QK_ROPE_DIM)]).astype(BF16)
    wdkv_swap = jnp.pad(_swap_halves(w_dkv[:, KV_LORA_RANK:]), [(0, 0), (0, LANES - QK_ROPE_DIM)]).astype(BF16)
    wkv = w_ukv.reshape(KV_LORA_RANK, MLA_HEADS, QK_NOPE_DIM + V_HEAD_DIM)
    wk = wkv[:, :, :QK_NOPE_DIM].reshape(KV_LORA_RANK, -1).astype(BF16)
    wv = wkv[:, :, QK_NOPE_DIM:].reshape(KV_LORA_RANK, -1).astype(BF16)

    cos, sin = _rope_tables(lay, tm)
    npt, spb = lay.prompt_tiles(tm), lay.sample_tiles_per_seq(tm)
    rope_spec = pl.BlockSpec((tm, LANES), lambda i: (jnp.where(i < npt, 0, 1 + (i - npt) % spb), 0))
    q, z = pl.pallas_call(
        functools.partial(_mla_qkv_kernel, scale=qk ** -0.5, n_prompt_tiles=npt),
        out_shape=(jax.ShapeDtypeStruct((t, MLA_HEADS * HEAD_W), BF16), jax.ShapeDtypeStruct((t, zw), F32)),
        grid=(t // tm,),
        in_specs=[pl.BlockSpec((tm, d), lambda i: (jnp.minimum(i, npt - 1), 0)),
                  pl.BlockSpec((tm, d), lambda i: (jnp.maximum(i - npt, 0), 0)), _resident((1, d)),
                  lay.mod_spec(tm, 0, d), lay.mod_spec(tm, 1, d), rope_spec, rope_spec,
                  _resident(w_dq.shape), _resident((1, rq)), _resident(wq_nope.shape), _resident(wq_rope.shape),
                  _resident(wq_swap.shape), _resident(wdkv.shape), _resident(wdkv_swap.shape),
                  _resident((1, KV_LORA_RANK))],
        out_specs=(pl.BlockSpec((tm, MLA_HEADS * HEAD_W), lambda i: (i, 0)), pl.BlockSpec((tm, zw), lambda i: (i, 0))),
        compiler_params=_params(1),
        name="mla_qkv",
    )(x_prompt, x_sample, norm_mix, mod, mod, cos, sin, w_dq.astype(BF16), q_norm, wq_nope, wq_rope, wq_swap, wdkv,
      wdkv_swap, kv_norm)

    past = cache_ckv.shape[1]
    z_cache = jnp.concatenate(
        [cache_ckv, cache_krope, jnp.zeros((lay.bs, past, LANES - QK_ROPE_DIM), F32)], axis=-1)
    z_sample = jnp.concatenate([z_cache, z[lay.tp:].reshape(lay.bs, lay.ss, zw)], axis=1)
    k_len = past + lay.ss
    kp, vp = _kv_expand(z, wk, wv, lay.tp)
    ks, vs = _kv_expand(z_sample.reshape(lay.bs * k_len, zw), wk, wv)

    o = _attention(q, kp, vp, None, n_seq=lay.bp, q_len=lay.sp, k_len=lay.sp, q_row0=0, tq=lay.sp, heads=MLA_HEADS)
    o = _attention(q, ks, vs, o, n_seq=lay.bs, q_len=lay.ss, k_len=k_len, q_row0=lay.tp,
                   tq=min(ATTN_Q_TILE, lay.ss), heads=1)
    ckv_state = z[:lay.tp, :KV_LORA_RANK].reshape(lay.bp, 1, lay.sp, KV_LORA_RANK)
    krope_state = z[:lay.tp, KV_LORA_RANK:KV_LORA_RANK + QK_ROPE_DIM].reshape(lay.bp, 1, lay.sp, QK_ROPE_DIM)
    return o, ckv_state, krope_state


def _seq_position(lay, tile, t):
    npt, spb = lay.prompt_tiles(tile), lay.sample_tiles_per_seq(tile)
    ppb = lay.sp // tile
    r = jnp.where(t < npt, t % ppb, (t - npt) % spb)
    n = jnp.where(t < npt, ppb, spb)
    return r, n


def _softplus(x):
    return jnp.maximum(x, 0.0) + jnp.log1p(jnp.exp(-jnp.abs(x)))


def _rglru_kernel(*refs, lay, n_tiles, reverse):
    if reverse:
        (xc_ref, hc_ref, yf_ref, h0_ref, wa_ref, ba_ref, wi_ref, bi_ref, lam_ref, wy_ref,
         o_ref, hl_ref, a_scr, u_scr, y_scr, carry_scr) = refs
    else:
        (xp_ref, x_ref, xn_ref, nm_ref, sh_ref, sc_ref, h0_ref, wx_ref, cw_ref, cb_ref,
         wa_ref, ba_ref, wi_ref, bi_ref, lam_ref,
         o_ref, xc_ref, hc_ref, hl_ref, xw_scr, a_scr, u_scr, carry_scr) = refs
    tm, d = xc_ref.shape
    groups = tm // SUBLANES
    bw = d // RG_BLOCKS
    i = pl.program_id(0)
    t = n_tiles - 1 - i if reverse else i
    r, n = _seq_position(lay, tm, t)

    @pl.when(i == 0)
    def _():
        carry_scr[...] = jnp.zeros_like(carry_scr)

    if reverse:
        xc = xc_ref[...]
    else:
        has_prev, has_next = r > 0, r < n - 1

        def pre(ref):
            return _norm_mod(ref[...], nm_ref[...], sh_ref[...], sc_ref[...]).astype(BF16)

        hc = pre(x_ref)
        hc_ref[...] = hc
        wx = wx_ref[...]
        xw_scr[0:SUBLANES] = jnp.where(has_prev, _dot(pre(xp_ref), wx), 0.0)
        xw_scr[SUBLANES:SUBLANES + tm] = _dot(hc, wx)
        xw_scr[SUBLANES + tm:] = jnp.where(has_next, _dot(pre(xn_ref), wx), 0.0)
        xc = cb_ref[...]
        for k in range(RG_CONV_W):
            xc = xc + cw_ref[k:k + 1, :] * xw_scr[pl.ds(SUBLANES - 1 + k, tm), :]
        xc_ref[...] = xc

    row = lax.broadcasted_iota(jnp.int32, (groups, SUBLANES, bw), 1)
    for nb in range(RG_BLOCKS):
        sl = slice(nb * bw, (nb + 1) * bw)
        xcn = xc[:, sl]
        xcb = xcn.astype(BF16)
        rg = jax.nn.sigmoid(_dot(xcb, wa_ref[nb]) + ba_ref[:, sl])
        ig = jax.nn.sigmoid(_dot(xcb, wi_ref[nb]) + bi_ref[:, sl])
        log_a = -RG_C * rg * _softplus(-lam_ref[:, sl])
        a = jnp.exp(log_a)
        u = (jnp.sqrt(-jnp.tanh(log_a) * (a * a + 1.0)) * (ig * xcn)).reshape(groups, SUBLANES, bw)
        a = a.reshape(groups, SUBLANES, bw)
        for k in (1, 2, 4):
            shift = SUBLANES - k if reverse else k
            keep = row < SUBLANES - k if reverse else row >= k
            a_nb = pltpu.roll(a, shift, 1)
            u_nb = pltpu.roll(u, shift, 1)
            u = jnp.where(keep, a * u_nb + u, u)
            a = jnp.where(keep, a * a_nb, a)
        a_scr[:, :, sl] = a
        u_scr[:, :, sl] = u

    is_start = r == n - 1 if reverse else r == 0
    h_init = jnp.where(is_start, h0_ref[...], carry_scr[...])
    dst = y_scr if reverse else o_ref
    edge = 0 if reverse else SUBLANES - 1

    def step(g, h):
        gg = groups - 1 - g if reverse else g
        y = a_scr[gg] * h + u_scr[gg]
        dst[pl.ds(pl.multiple_of(gg * SUBLANES, SUBLANES), SUBLANES), :] = y
        return y[edge:edge + 1, :]

    h_end = lax.fori_loop(0, groups, step, h_init)
    carry_scr[...] = h_end
    hl_ref[...] = h_end
    if reverse:
        gate = jax.nn.gelu(_dot(hc_ref[...], wy_ref[...]))
        o_ref[...] = ((yf_ref[...] + y_scr[...]) * gate).astype(BF16)


def _rglru_layer(lay, x, mod, norm_mix, state, w_x, w_y, conv_w, conv_b, w_a, b_a, w_i, b_i, lam):
    t, d = x.shape
    tm = SEQ_TILE
    n_tiles = t // tm
    hb = tm // SUBLANES
    n_halo = t // SUBLANES
    bw = d // RG_BLOCKS

    def run(reverse, *fwd_out):
        dr = int(reverse)
        order = (lambda i: n_tiles - 1 - i) if reverse else (lambda i: i)
        h0 = jnp.concatenate([jnp.zeros((1, d), F32), state[:, dr], jnp.zeros((SUBLANES - 1 - lay.bs, d), F32)])
        h0_spec = lay.row_spec(tm, order)(d)
        tile = pl.BlockSpec((tm, d), lambda i: (order(i), 0))
        gate_specs = [_resident((RG_BLOCKS, bw, bw)), _resident((1, d)), _resident((RG_BLOCKS, bw, bw)),
                      _resident((1, d)), _resident((1, d))]
        gate_args = [w_a[dr].astype(BF16), b_a[dr:dr + 1], w_i[dr].astype(BF16), b_i[dr:dr + 1], lam[dr:dr + 1]]
        scan_scratch = [pltpu.VMEM((hb, SUBLANES, d), F32), pltpu.VMEM((hb, SUBLANES, d), F32)]
        last_spec = pl.BlockSpec((None, 1, d), lambda i: (order(i), 0, 0))
        last_shape = jax.ShapeDtypeStruct((n_tiles, 1, d), F32)
        if reverse:
            xc, hc, y_fwd = fwd_out
            in_specs = [tile, tile, tile, h0_spec] + gate_specs + [_resident((d, d))]
            args = [xc, hc, y_fwd, h0.reshape(SUBLANES, 1, d)] + gate_args + [w_y.astype(BF16)]
            out_shape = (jax.ShapeDtypeStruct((t, d), BF16), last_shape)
            out_specs = (tile, last_spec)
            scratch = scan_scratch + [pltpu.VMEM((tm, d), F32)]
        else:
            in_specs = [pl.BlockSpec((SUBLANES, d), lambda i: (jnp.maximum(i * hb - 1, 0), 0)),
                        tile,
                        pl.BlockSpec((SUBLANES, d), lambda i: (jnp.minimum((i + 1) * hb, n_halo - 1), 0)),
                        _resident((1, d)), lay.mod_spec(tm, 0, d), lay.mod_spec(tm, 1, d), h0_spec,
                        _resident((d, d)), _resident((RG_CONV_W, d)), _resident((1, d))] + gate_specs
            args = [x, x, x, norm_mix, mod, mod, h0.reshape(SUBLANES, 1, d), w_x.astype(BF16), conv_w, conv_b] + gate_args
            out_shape = (jax.ShapeDtypeStruct((t, d), F32), jax.ShapeDtypeStruct((t, d), F32),
                         jax.ShapeDtypeStruct((t, d), BF16), last_shape)
            out_specs = (tile, tile, tile, last_spec)
            scratch = [pltpu.VMEM((tm + 2 * SUBLANES, d), F32)] + scan_scratch
        scratch.append(pltpu.VMEM((1, d), F32))
        return pl.pallas_call(
            functools.partial(_rglru_kernel, lay=lay, n_tiles=n_tiles, reverse=reverse),
            out_shape=out_shape,
            grid=(n_tiles,),
            in_specs=in_specs,
            out_specs=out_specs,
            scratch_shapes=scratch,
            compiler_params=_params(1, "arbitrary"),
            name="rglru_bwd" if reverse else "rglru_fwd",
        )(*args)

    y_fwd, xc, hc, last_f = run(False)
    m, last_b = run(True, xc, hc, y_fwd)
    ppb = lay.sp // tm
    tail = last_f[:lay.prompt_tiles(tm)].reshape(lay.bp, ppb, d)[:, ppb - 1]
    head = last_b[:lay.prompt_tiles(tm)].reshape(lay.bp, ppb, d)[:, 0]
    return m, jnp.stack([tail, head], axis=1)[:, None]


def _dft_tables(n, scale):
    jk = np.outer(np.arange(n), np.arange(n)) % n
    ang = 2.0 * np.pi * jk / n
    return np.cos(ang) * scale, np.sin(ang) * scale


def _fnet_channel_kernel(x_ref, nm_ref, sh_ref, sc_ref, w_ref, zc_ref, zs_ref):
    h = _norm_mod(x_ref[...], nm_ref[...], sh_ref[...], sc_ref[...]).astype(BF16)
    gw = w_ref.shape[0]
    for g in range(FNET_GROUPS):
        sl = slice(g * gw, (g + 1) * gw)
        f = _dot(h[:, sl], w_ref[...])
        zc_ref[:, sl] = f[:, :gw].astype(BF16)
        zs_ref[:, sl] = f[:, gw:].astype(BF16)


def _fnet_position_kernel(*refs):
    wc_ref, ws_ref, zc_ref, zs_ref = refs[:4]
    o_ref = refs[-1]
    o_ref[...] = (_dot(wc_ref[...], zc_ref[...]) - _dot(ws_ref[...], zs_ref[...])).astype(BF16)


def _fnet_position(zc, zs, out, *, n_seq, seq_len, row0, tm):
    t, d = zc.shape
    assert row0 % seq_len == 0 and seq_len % tm == 0
    cos, sin = _dft_tables(seq_len, seq_len ** -0.5)
    wc, ws = jnp.asarray(cos, F32).astype(BF16), jnp.asarray(sin, F32).astype(BF16)
    sb0, rt = row0 // seq_len, seq_len // tm
    w_spec = pl.BlockSpec((tm, seq_len), lambda b, j: (j, 0))
    z_spec = pl.BlockSpec((seq_len, d), lambda b, j: (sb0 + b, 0))
    in_specs = [w_spec, w_spec, z_spec, z_spec]
    args = [wc, ws, zc, zs]
    aliases = {}
    if out is not None:
        in_specs.append(pl.BlockSpec(memory_space=pl.ANY))
        args.append(out)
        aliases = {4: 0}
    return pl.pallas_call(
        _fnet_position_kernel,
        out_shape=jax.ShapeDtypeStruct((t, d), BF16),
        grid=(n_seq, rt),
        in_specs=in_specs,
        out_specs=pl.BlockSpec((tm, d), lambda b, j: ((sb0 + b) * rt + j, 0)),
        input_output_aliases=aliases,
        compiler_params=_params(2),
        name="fnet_position",
    )(*args)


def _fnet_layer(lay, x, mod, norm_mix):
    t, d = x.shape
    tm = SEQ_TILE
    gw = d // FNET_GROUPS
    cos, sin = _dft_tables(gw, gw ** -0.5)
    w = jnp.asarray(np.concatenate([cos, sin], axis=1), F32).astype(BF16)
    tile = pl.BlockSpec((tm, d), lambda i: (i, 0))
    zc, zs = pl.pallas_call(
        _fnet_channel_kernel,
        out_shape=(jax.ShapeDtypeStruct((t, d), BF16), jax.ShapeDtypeStruct((t, d), BF16)),
        grid=(t // tm,),
        in_specs=[tile, _resident((1, d)), lay.mod_spec(tm, 0, d), lay.mod_spec(tm, 1, d), _resident(w.shape)],
        out_specs=(tile, tile),
        compiler_params=_params(1),
        name="fnet_channel",
    )(x, norm_mix, mod, mod, w)
    m = _fnet_position(zc, zs, None, n_seq=lay.bp, seq_len=lay.sp, row0=0, tm=lay.sp)
    return _fnet_position(zc, zs, m, n_seq=lay.bs, seq_len=lay.ss, row0=lay.tp, tm=min(FNET_ROW_TILE, lay.ss))


def _conformer_kernel(xp_ref, xc_ref, xn_ref, nm_ref, sh_ref, sc_ref, w1_ref, b1_ref, dw_ref, db_ref, lg_ref, lb_ref,
                      o_ref, src_scr, acc_scr, *, lay):
    tm, d = xc_ref.shape
    r, n = _seq_position(lay, tm, pl.program_id(0))
    has_prev, has_next = r > 0, r < n - 1

    def glu(ref):
        h = _norm_mod(ref[...], nm_ref[...], sh_ref[...], sc_ref[...]).astype(BF16)
        z = _dot(h, w1_ref[...]) + b1_ref[...]
        return z[:, :d] * jax.nn.sigmoid(z[:, d:])

    def put(rows, val):
        for c in range(d // LANES):
            src_scr[0, c, rows, :] = val[:, c * LANES:(c + 1) * LANES]

    def halo(rows, ref, valid):
        @pl.when(valid)
        def _():
            put(rows, glu(ref))

        @pl.when(jnp.logical_not(valid))
        def _():
            put(rows, jnp.zeros((CONF_HALO, d), F32))

    halo(slice(0, CONF_HALO), xp_ref, has_prev)
    put(slice(CONF_HALO, CONF_HALO + tm), glu(xc_ref))
    halo(slice(CONF_HALO + tm, CONF_HALO + tm + CONF_HALO), xn_ref, has_next)
    n_rows = tm + 2 * CONF_HALO - SUBLANES
    for s in range(1, SUBLANES):
        for c in range(d // LANES):
            src_scr[s, c, 0:n_rows, :] = src_scr[0, c, pl.ds(s, n_rows), :]
    pad_left = (CONF_CONV_W - 1) // 2
    taps = [divmod(CONF_HALO - pad_left + k, SUBLANES) for k in range(CONF_CONV_W)]
    for c in range(d // LANES):
        lanes = slice(c * LANES, (c + 1) * LANES)
        w = [jnp.broadcast_to(dw_ref[k:k + 1, lanes], (SUBLANES, LANES)) for k in range(CONF_CONV_W)]
        bias = jnp.broadcast_to(db_ref[:, lanes], (SUBLANES, LANES))

        def conv_rows(g, carry, c=c, lanes=lanes, w=w, bias=bias):
            row0 = pl.multiple_of(g * SUBLANES, SUBLANES)
            part = [bias] + [None] * (CONF_CHAINS - 1)
            for k, (q, s) in enumerate(taps):
                term = w[k] * src_scr[s, c, pl.ds(row0 + q * SUBLANES, SUBLANES), :]
                j = k % CONF_CHAINS
                part[j] = term if part[j] is None else part[j] + term
            while len(part) > 1:
                part = [a + b for a, b in zip(part[0::2], part[1::2])]
            acc_scr[pl.ds(row0, SUBLANES), lanes] = part[0]
            return carry

        lax.fori_loop(0, tm // SUBLANES, conv_rows, 0, unroll=2)
    acc = acc_scr[...]
    mu = jnp.mean(acc, axis=-1, keepdims=True)
    cen = acc - mu
    var = jnp.mean(cen * cen, axis=-1, keepdims=True)
    y = cen * lax.rsqrt(var + EPS) * lg_ref[...] + lb_ref[...]
    o_ref[...] = (y * jax.nn.sigmoid(y)).astype(BF16)


def _conformer_layer(lay, x, mod, norm_mix, w_pw1, b_pw1, dw_w, dw_b, ln_g, ln_b):
    t, d = x.shape
    tm = SEQ_TILE
    hb = tm // CONF_HALO
    n_halo = t // CONF_HALO
    tile = pl.BlockSpec((tm, d), lambda i: (i, 0))
    return pl.pallas_call(
        functools.partial(_conformer_kernel, lay=lay),
        out_shape=jax.ShapeDtypeStruct((t, d), BF16),
        grid=(t // tm,),
        in_specs=[pl.BlockSpec((CONF_HALO, d), lambda i: (jnp.maximum(i * hb - 1, 0), 0)),
                  tile,
                  pl.BlockSpec((CONF_HALO, d), lambda i: (jnp.minimum((i + 1) * hb, n_halo - 1), 0)),
                  _resident((1, d)), lay.mod_spec(tm, 0, d), lay.mod_spec(tm, 1, d),
                  _resident((d, 2 * d)), _resident((1, 2 * d)), _resident((CONF_CONV_W, d)), _resident((1, d)),
                  _resident((1, d)), _resident((1, d))],
        out_specs=tile,
        scratch_shapes=[pltpu.VMEM((SUBLANES, d // LANES, tm + 2 * CONF_HALO + SUBLANES, LANES), F32),
                        pltpu.VMEM((tm, d), F32)],
        compiler_params=_params(1),
        name="conformer",
    )(x, x, x, norm_mix, mod, mod, w_pw1.astype(BF16), b_pw1, dw_w, dw_b, ln_g, ln_b)


def kernel(x_prompt, x_sample, cache_mla_ckv, cache_mla_krope, state_rglru, c, c_ctx, ada_w, ada_b, norm_mix, norm_ffn, mla_w_dq, mla_q_norm, mla_w_uq, mla_w_dkv, mla_kv_norm, mla_w_ukv, mla_w_o, rg_w_x, rg_w_y, rg_conv_w, rg_conv_b, rg_w_a, rg_b_a, rg_w_i, rg_b_i, rg_lam, rg_w_o, fn_w_o, fn_b_o, cf_w_pw1, cf_b_pw1, cf_dw_w, cf_dw_b, cf_ln_g, cf_ln_b, cf_w_pw2, cf_b_pw2, ffn_w_gate, ffn_w_up, ffn_w_down, final_norm):
    bp, sp, d = x_prompt.shape
    bs, ss, _ = x_sample.shape
    depth = ada_w.shape[0]
    assert depth == 4 and bs < SUBLANES and sp == SEQ_TILE and ss % GRID_W == 0
    lay = _Layout(bp, sp, bs, ss)
    x_in = (x_prompt.reshape(lay.tp, d), x_sample.reshape(lay.ts, d))

    cvec = jnp.concatenate([c_ctx[None], c, jnp.zeros((SUBLANES - 1 - bs, d), F32)], axis=0)
    mods = _ada_tables(cvec, ada_w, ada_b).reshape(depth, SUBLANES, 1, -1)
    zero_bias = jnp.zeros((1, d), F32)
    w_gate, w_up, w_down = ffn_w_gate.astype(BF16), ffn_w_up.astype(BF16), ffn_w_down.astype(BF16)
    norm_ffn3 = norm_ffn[:, None, :]

    def ffn(layer, x, m, w_out, b_out, final=None):
        return _ffn_layer(lay, layer, x, m, w_out.astype(BF16), b_out, mods[layer], norm_ffn3, w_gate, w_up, w_down,
                          final)

    m, ckv_state, krope_state = _mla_layer(
        lay, *x_in, mods[0], norm_mix[0:1], cache_mla_ckv[:, 0], cache_mla_krope[:, 0], mla_w_dq[0], mla_q_norm[0:1],
        mla_w_uq[0], mla_w_dkv[0], mla_kv_norm[0:1], mla_w_ukv[0])
    x = ffn(0, x_in, m, mla_w_o[0], zero_bias)

    m, rg_state = _rglru_layer(lay, x, mods[1], norm_mix[1:2], state_rglru[:, 0], rg_w_x[0], rg_w_y[0], rg_conv_w[0],
                               rg_conv_b[0:1], rg_w_a[0], rg_b_a[0], rg_w_i[0], rg_b_i[0], rg_lam[0])
    x = ffn(1, x, m, rg_w_o[0], zero_bias)

    m = _fnet_layer(lay, x, mods[2], norm_mix[2:3])
    x = ffn(2, x, m, fn_w_o[0], fn_b_o[0:1])

    m = _conformer_layer(lay, x, mods[3], norm_mix[3:4], cf_w_pw1[0], cf_b_pw1[0:1], cf_dw_w[0], cf_dw_b[0:1],
                         cf_ln_g[0:1], cf_ln_b[0:1])
    y_prompt, y_sample = ffn(3, x, m, cf_w_pw2[0], cf_b_pw2[0:1], final_norm[None])

    return (y_prompt.reshape(bp, sp, d), y_sample.reshape(bs, ss, d), ckv_state, krope_state, rg_state)
```

```python
import functools

import numpy as np
import jax
import jax.numpy as jnp
from jax import lax
from jax.experimental import pallas as pl
from jax.experimental.pallas import tpu as pltpu

F32 = jnp.float32
BF16 = jnp.bfloat16

EPS = 1e-6
GRID_W = 64
MLA_HEADS = 8
KV_LORA_RANK = 256
QK_NOPE_DIM = 128
QK_ROPE_DIM = 64
V_HEAD_DIM = 128
ROPE_THETA = 10000.0
RG_BLOCKS = 4
RG_CONV_W = 4
RG_C = 8.0
FNET_GROUPS = 4
CONF_CONV_W = 31

LANES = 128
SUBLANES = 8
HEAD_W = 2 * LANES
SEQ_TILE = 256
FFN_TILE = 512
ATTN_Q_TILE = 1024
ATTN_SUB_TILE = 256
FNET_ROW_TILE = 512
FFN_CHUNK = 256
CONF_HALO = 16
RG_HALO = 16
CONF_CHAINS = 4
VMEM_LIMIT = 52 * 1024 * 1024


def _dot(a, b):
    return jnp.dot(a, b, preferred_element_type=F32)


def _rms(x, g):
    return x * lax.rsqrt(jnp.mean(x * x, axis=-1, keepdims=True) + EPS) * g


def _norm_mod(x, g, shift, scale):
    return _rms(x, g) * (1.0 + scale) + shift


def _resident(shape):
    nd = len(shape)
    return pl.BlockSpec(shape, lambda *_: (0,) * nd, pipeline_mode=pl.Buffered(1))


def _resident_at(index, shape):
    nd = len(shape)
    return pl.BlockSpec((None,) + tuple(shape), lambda *_: (index,) + (0,) * nd, pipeline_mode=pl.Buffered(1))


def _params(n_axes, semantics="parallel"):
    return pltpu.CompilerParams(dimension_semantics=(semantics,) * n_axes, vmem_limit_bytes=VMEM_LIMIT)


class _Layout:
    def __init__(self, n_prompt_seq, prompt_len, n_sample_seq, sample_len):
        self.bp, self.sp, self.bs, self.ss = n_prompt_seq, prompt_len, n_sample_seq, sample_len
        self.tp = n_prompt_seq * prompt_len
        self.ts = n_sample_seq * sample_len
        self.t = self.tp + self.ts

    def prompt_tiles(self, tile):
        assert self.tp % tile == 0 and self.ss % tile == 0
        return self.tp // tile

    def sample_tiles_per_seq(self, tile):
        return self.ss // tile

    def mod_row(self, tile):
        npt, spb = self.prompt_tiles(tile), self.sample_tiles_per_seq(tile)
        return lambda i: jnp.where(i < npt, 0, 1 + (i - npt) // spb)

    def mod_spec(self, tile, chunk, d, order=lambda i: i):
        row = self.mod_row(tile)
        return pl.BlockSpec((None, 1, d), lambda i: (row(order(i)), 0, chunk))

    def row_spec(self, tile, order=lambda i: i):
        row = self.mod_row(tile)
        return lambda d: pl.BlockSpec((None, 1, d), lambda i: (row(order(i)), 0, 0))


def _ada_kernel(c_ref, w_ref, b_ref, o_ref):
    c = c_ref[...]
    s = (c * jax.nn.sigmoid(c)).astype(BF16)
    o_ref[...] = _dot(s, w_ref[...].astype(BF16)) + b_ref[...]


def _ada_tables(cvec, ada_w, ada_b):
    depth, d, n = ada_w.shape
    tn = n // 4
    return pl.pallas_call(
        _ada_kernel,
        out_shape=jax.ShapeDtypeStruct((depth, SUBLANES, n), F32),
        grid=(depth, n // tn),
        in_specs=[
            pl.BlockSpec((SUBLANES, d), lambda l, j: (0, 0)),
            pl.BlockSpec((None, d, tn), lambda l, j: (l, 0, j)),
            pl.BlockSpec((None, 1, tn), lambda l, j: (l, 0, j)),
        ],
        out_specs=pl.BlockSpec((None, SUBLANES, tn), lambda l, j: (l, 0, j)),
        compiler_params=_params(2),
        name="ada_tables",
    )(cvec, ada_w, ada_b.reshape(depth, 1, n))


def _read_tokens(refs, n_prompt_tiles):
    if len(refs) == 1:
        return refs[0][...]
    return jnp.where(pl.program_id(0) < n_prompt_tiles, refs[0][...], refs[1][...])


def _fnet_channel_dft(h, w_ref, zc_ref, zs_ref):
    gw = w_ref.shape[0]
    for g in range(FNET_GROUPS):
        sl = slice(g * gw, (g + 1) * gw)
        f = _dot(h[:, sl], w_ref[...])
        zc_ref[:, sl] = f[:, :gw].astype(BF16)
        zs_ref[:, sl] = f[:, gw:].astype(BF16)


def _ffn_kernel(*refs, n_chunks, n_x, n_m, tail, n_prompt_tiles):
    x_refs, m_refs, refs = refs[:n_x], refs[n_x:n_x + n_m], refs[n_x + n_m:]
    wo_ref, bo_ref, g1_ref, sh_ref, sc_ref, g2_ref, nf_ref, wg_ref, wu_ref, wd_ref = refs[:10]
    refs, a_ref = refs[10:-1], refs[-1]
    m = _read_tokens(m_refs, n_prompt_tiles)
    x1 = _read_tokens(x_refs, n_prompt_tiles) + g1_ref[...] * (_dot(m, wo_ref[...]) + bo_ref[...])
    h = _norm_mod(x1, nf_ref[...], sh_ref[...], sc_ref[...]).astype(BF16)
    for c in range(n_chunks):
        sl = slice(c * FFN_CHUNK, (c + 1) * FFN_CHUNK)
        g = _dot(h, wg_ref[:, sl])
        u = _dot(h, wu_ref[:, sl])
        a_ref[:, sl] = (g * jax.nn.sigmoid(g) * u).astype(BF16)
    x2 = x1 + g2_ref[...] * _dot(a_ref[...], wd_ref[...])
    if tail is None:
        refs[0][...] = x2
    elif tail == "fnet":
        nm_ref, sh1_ref, sc1_ref, dft_ref, o_ref, zc_ref, zs_ref = refs
        o_ref[...] = x2
        h_next = _norm_mod(x2, nm_ref[...], sh1_ref[...], sc1_ref[...]).astype(BF16)
        _fnet_channel_dft(h_next, dft_ref, zc_ref, zs_ref)
    else:
        fin_ref, op_ref, os_ref = refs
        y = _rms(x2, fin_ref[...])
        is_prompt = pl.program_id(0) < n_prompt_tiles

        @pl.when(is_prompt)
        def _():
            op_ref[...] = y

        @pl.when(jnp.logical_not(is_prompt))
        def _():
            os_ref[...] = y


def _ffn_layer(lay, layer, x, m, w_out, b_out, mod, norm_ffn, w_gate, w_up, w_down, final_norm=None, fnet_next=None):
    xs = x if isinstance(x, tuple) else (x,)
    ms = m if isinstance(m, tuple) else (m,)
    d = xs[0].shape[1]
    dff = w_gate.shape[2]
    assert dff % FFN_CHUNK == 0
    tm = FFN_TILE
    npt = lay.prompt_tiles(tm)
    tile = pl.BlockSpec((tm, d), lambda i: (i, 0))
    split = [pl.BlockSpec((tm, d), lambda i: (jnp.minimum(i, npt - 1), 0)),
             pl.BlockSpec((tm, d), lambda i: (jnp.maximum(i - npt, 0), 0))]
    in_specs = (split if len(xs) == 2 else [tile]) + (split if len(ms) == 2 else [tile]) + [
        _resident((d, d)), _resident((1, d)),
        lay.mod_spec(tm, 2, d), lay.mod_spec(tm, 3, d), lay.mod_spec(tm, 4, d), lay.mod_spec(tm, 5, d),
        _resident_at(layer, (1, d)), _resident_at(layer, (d, dff)), _resident_at(layer, (d, dff)),
        _resident_at(layer, (dff, d))]
    args = list(xs) + list(ms) + [w_out, b_out, mod, mod, mod, mod, norm_ffn, w_gate, w_up, w_down]
    out_shape, out_specs, tail = jax.ShapeDtypeStruct((lay.t, d), F32), tile, None
    if final_norm is not None:
        tail = "final"
        in_specs.append(_resident((1, d)))
        args.append(final_norm)
        out_shape = (jax.ShapeDtypeStruct((lay.tp, d), F32), jax.ShapeDtypeStruct((lay.ts, d), F32))
        out_specs = tuple(split)
    elif fnet_next is not None:
        tail = "fnet"
        norm_next, mod_next = fnet_next
        dft = _fnet_channel_table(d)
        in_specs += [_resident((1, d)), lay.mod_spec(tm, 0, d), lay.mod_spec(tm, 1, d), _resident(dft.shape)]
        args += [norm_next, mod_next, mod_next, dft]
        out_shape = (out_shape, jax.ShapeDtypeStruct((lay.t, d), BF16), jax.ShapeDtypeStruct((lay.t, d), BF16))
        out_specs = (tile, tile, tile)
    return pl.pallas_call(
        functools.partial(_ffn_kernel, n_chunks=dff // FFN_CHUNK, n_x=len(xs), n_m=len(ms), tail=tail,
                          n_prompt_tiles=npt),
        out_shape=out_shape,
        grid=(lay.t // tm,),
        in_specs=in_specs,
        out_specs=out_specs,
        scratch_shapes=[pltpu.VMEM((tm, dff), BF16)],
        compiler_params=_params(1, "arbitrary" if tail == "final" else "parallel"),
        name="ffn",
    )(*args)


def _mla_qkv_kernel(xp_ref, xs_ref, nm_ref, sh_ref, sc_ref, cos_ref, sin_ref, wdq_ref, qn_ref, wqn_ref, wqr_ref, wqx_ref,
                    wdkv_ref, wdkvx_ref, kvn_ref, q_ref, z_ref, *, scale, n_prompt_tiles):
    x = _read_tokens((xp_ref, xs_ref), n_prompt_tiles)
    h = _norm_mod(x, nm_ref[...], sh_ref[...], sc_ref[...]).astype(BF16)
    cos, sin = cos_ref[...], sin_ref[...]
    cq = _rms(_dot(h, wdq_ref[...]), qn_ref[...]).astype(BF16)
    q_nope = _dot(cq, wqn_ref[...])
    q_rope = _dot(cq, wqr_ref[...])
    q_swap = _dot(cq, wqx_ref[...])
    for hd in range(MLA_HEADS):
        sl = slice(hd * LANES, (hd + 1) * LANES)
        q_ref[:, hd * HEAD_W:hd * HEAD_W + LANES] = (q_nope[:, sl] * scale).astype(BF16)
        q_ref[:, hd * HEAD_W + LANES:(hd + 1) * HEAD_W] = ((q_rope[:, sl] * cos + q_swap[:, sl] * sin) * scale).astype(BF16)
    z = _dot(h, wdkv_ref[...])
    z_swap = _dot(h, wdkvx_ref[...])
    z_ref[:, :KV_LORA_RANK] = _rms(z[:, :KV_LORA_RANK], kvn_ref[...])
    z_ref[:, KV_LORA_RANK:] = z[:, KV_LORA_RANK:] * cos + z_swap * sin


def _kv_expand_kernel(z_ref, wk_ref, wv_ref, k_ref, v_ref):
    z = z_ref[...]
    ckv = z[:, :KV_LORA_RANK].astype(BF16)
    k_rope = z[:, KV_LORA_RANK:].astype(BF16)
    k_nope = _dot(ckv, wk_ref[...])
    for hd in range(MLA_HEADS):
        k_ref[:, hd * HEAD_W:hd * HEAD_W + LANES] = k_nope[:, hd * LANES:(hd + 1) * LANES].astype(BF16)
        k_ref[:, hd * HEAD_W + LANES:(hd + 1) * HEAD_W] = k_rope
    v_ref[...] = _dot(ckv, wv_ref[...]).astype(BF16)


def _kv_expand(z, wk, wv, n=None):
    zw = z.shape[1]
    n = z.shape[0] if n is None else n
    tm = 2 * SEQ_TILE if n % (2 * SEQ_TILE) == 0 else SEQ_TILE
    return pl.pallas_call(
        _kv_expand_kernel,
        out_shape=(jax.ShapeDtypeStruct((n, MLA_HEADS * HEAD_W), BF16),
                   jax.ShapeDtypeStruct((n, MLA_HEADS * V_HEAD_DIM), BF16)),
        grid=(n // tm,),
        in_specs=[pl.BlockSpec((tm, zw), lambda i: (i, 0)), _resident(wk.shape), _resident(wv.shape)],
        out_specs=(pl.BlockSpec((tm, MLA_HEADS * HEAD_W), lambda i: (i, 0)),
                   pl.BlockSpec((tm, MLA_HEADS * V_HEAD_DIM), lambda i: (i, 0))),
        compiler_params=_params(1),
        name="mla_kv_expand",
    )(z, wk, wv)


def _attn_kernel(*refs, heads, sub):
    q_ref, k_ref, v_ref = refs[:3]
    o_ref = refs[-1]
    tq = q_ref.shape[0]
    for hd in range(heads):
        k = k_ref[:, hd * HEAD_W:(hd + 1) * HEAD_W]
        v = v_ref[:, hd * V_HEAD_DIM:(hd + 1) * V_HEAD_DIM]
        for j in range(tq // sub):
            rows = slice(j * sub, (j + 1) * sub)
            q = q_ref[rows, hd * HEAD_W:(hd + 1) * HEAD_W]
            s = lax.dot_general(q, k, (((1,), (1,)), ((), ())), preferred_element_type=F32)
            p = jnp.exp(s - jnp.max(s, axis=-1, keepdims=True))
            den = jnp.sum(p, axis=-1, keepdims=True)
            o = _dot(p.astype(BF16), v)
            o_ref[rows, hd * V_HEAD_DIM:(hd + 1) * V_HEAD_DIM] = (o / den).astype(BF16)


def _attention(q, k, v, *, n_seq, q_len, k_len, q_row0, tq, heads):
    hg = MLA_HEADS // heads
    assert q_row0 % tq == 0 and q_len % tq == 0
    qb0, qt = q_row0 // tq, q_len // tq
    return pl.pallas_call(
        functools.partial(_attn_kernel, heads=heads, sub=min(ATTN_SUB_TILE, tq)),
        out_shape=jax.ShapeDtypeStruct((n_seq * q_len, MLA_HEADS * V_HEAD_DIM), BF16),
        grid=(n_seq, hg, qt),
        in_specs=[pl.BlockSpec((tq, heads * HEAD_W), lambda b, g, j: (qb0 + b * qt + j, g)),
                  pl.BlockSpec((k_len, heads * HEAD_W), lambda b, g, j: (b, g)),
                  pl.BlockSpec((k_len, heads * V_HEAD_DIM), lambda b, g, j: (b, g))],
        out_specs=pl.BlockSpec((tq, heads * V_HEAD_DIM), lambda b, g, j: (b * qt + j, g)),
        compiler_params=_params(3),
        name="mla_attention",
    )(q, k, v)


def _rope_tables(lay, tile):
    pos = np.arange(lay.ss)
    n_freq = QK_ROPE_DIM // 4
    inv_freq = ROPE_THETA ** (-np.arange(n_freq, dtype=np.float64) / n_freq)
    ang = np.concatenate([(pos // GRID_W)[:, None] * inv_freq, (pos % GRID_W)[:, None] * inv_freq], axis=-1)
    pad = np.zeros((lay.ss, LANES - QK_ROPE_DIM))
    cos = np.concatenate([np.cos(ang), np.cos(ang), pad], axis=-1)
    sin = np.concatenate([np.sin(ang), np.sin(ang), pad], axis=-1)
    ident_cos = np.concatenate([np.ones((tile, QK_ROPE_DIM)), np.zeros((tile, LANES - QK_ROPE_DIM))], axis=-1)
    cos = np.concatenate([ident_cos, cos], axis=0)
    sin = np.concatenate([np.zeros((tile, LANES)), sin], axis=0)
    return jnp.asarray(cos, F32), jnp.asarray(sin, F32)


def _swap_halves(w):
    half = QK_ROPE_DIM // 2
    return jnp.concatenate([-w[..., half:], w[..., :half]], axis=-1)


def _mla_layer(lay, x_prompt, x_sample, mod, norm_mix, cache_ckv, cache_krope, w_dq, q_norm, w_uq, w_dkv, kv_norm,
               w_ukv):
    t, d = lay.t, x_prompt.shape[1]
    tm = SEQ_TILE
    rq = w_dq.shape[1]
    qk = QK_NOPE_DIM + QK_ROPE_DIM
    zw = KV_LORA_RANK + LANES
    rope_pad = [(0, 0)] * 2 + [(0, LANES - QK_ROPE_DIM)]

    wq = w_uq.reshape(rq, MLA_HEADS, qk)
    wq_nope = wq[:, :, :QK_NOPE_DIM].reshape(rq, -1).astype(BF16)
    wq_rope = jnp.pad(wq[:, :, QK_NOPE_DIM:], rope_pad).reshape(rq, -1).astype(BF16)
    wq_swap = jnp.pad(_swap_halves(wq[:, :, QK_NOPE_DIM:]), rope_pad).reshape(rq, -1).astype(BF16)
    wdkv = jnp.pad(w_dkv, [(0, 0), (0, LANES - QK_ROPE_DIM)]).astype(BF16)
    wdkv_swap = jnp.pad(_swap_halves(w_dkv[:, KV_LORA_RANK:]), [(0, 0), (0, LANES - QK_ROPE_DIM)]).astype(BF16)
    wkv = w_ukv.reshape(KV_LORA_RANK, MLA_HEADS, QK_NOPE_DIM + V_HEAD_DIM)
    wk = wkv[:, :, :QK_NOPE_DIM].reshape(KV_LORA_RANK, -1).astype(BF16)
    wv = wkv[:, :, QK_NOPE_DIM:].reshape(KV_LORA_RANK, -1).astype(BF16)

    cos, sin = _rope_tables(lay, tm)
    npt, spb = lay.prompt_tiles(tm), lay.sample_tiles_per_seq(tm)
    rope_spec = pl.BlockSpec((tm, LANES), lambda i: (jnp.where(i < npt, 0, 1 + (i - npt) % spb), 0))
    q, z = pl.pallas_call(
        functools.partial(_mla_qkv_kernel, scale=qk ** -0.5, n_prompt_tiles=npt),
        out_shape=(jax.ShapeDtypeStruct((t, MLA_HEADS * HEAD_W), BF16), jax.ShapeDtypeStruct((t, zw), F32)),
        grid=(t // tm,),
        in_specs=[pl.BlockSpec((tm, d), lambda i: (jnp.minimum(i, npt - 1), 0)),
                  pl.BlockSpec((tm, d), lambda i: (jnp.maximum(i - npt, 0), 0)), _resident((1, d)),
                  lay.mod_spec(tm, 0, d), lay.mod_spec(tm, 1, d), rope_spec, rope_spec,
                  _resident(w_dq.shape), _resident((1, rq)), _resident(wq_nope.shape), _resident(wq_rope.shape),
                  _resident(wq_swap.shape), _resident(wdkv.shape), _resident(wdkv_swap.shape),
                  _resident((1, KV_LORA_RANK))],
        out_specs=(pl.BlockSpec((tm, MLA_HEADS * HEAD_W), lambda i: (i, 0)), pl.BlockSpec((tm, zw), lambda i: (i, 0))),
        compiler_params=_params(1),
        name="mla_qkv",
    )(x_prompt, x_sample, norm_mix, mod, mod, cos, sin, w_dq.astype(BF16), q_norm, wq_nope, wq_rope, wq_swap, wdkv,
      wdkv_swap, kv_norm)

    past = cache_ckv.shape[1]
    z_cache = jnp.concatenate(
        [cache_ckv, cache_krope, jnp.zeros((lay.bs, past, LANES - QK_ROPE_DIM), F32)], axis=-1)
    z_sample = jnp.concatenate([z_cache, z[lay.tp:].reshape(lay.bs, lay.ss, zw)], axis=1)
    k_len = past + lay.ss
    kp, vp = _kv_expand(z, wk, wv, lay.tp)
    ks, vs = _kv_expand(z_sample.reshape(lay.bs * k_len, zw), wk, wv)

    o_prompt = _attention(q, kp, vp, n_seq=lay.bp, q_len=lay.sp, k_len=lay.sp, q_row0=0, tq=lay.sp, heads=MLA_HEADS)
    o_sample = _attention(q, ks, vs, n_seq=lay.bs, q_len=lay.ss, k_len=k_len, q_row0=lay.tp,
                          tq=min(ATTN_Q_TILE, lay.ss), heads=1)
    ckv_state = z[:lay.tp, :KV_LORA_RANK].reshape(lay.bp, 1, lay.sp, KV_LORA_RANK)
    krope_state = z[:lay.tp, KV_LORA_RANK:KV_LORA_RANK + QK_ROPE_DIM].reshape(lay.bp, 1, lay.sp, QK_ROPE_DIM)
    return (o_prompt, o_sample), ckv_state, krope_state


def _seq_position(lay, tile, t):
    npt, spb = lay.prompt_tiles(tile), lay.sample_tiles_per_seq(tile)
    ppb = lay.sp // tile
    r = jnp.where(t < npt, t % ppb, (t - npt) % spb)
    n = jnp.where(t < npt, ppb, spb)
    return r, n


def _softplus(x):
    return jnp.maximum(x, 0.0) + jnp.log1p(jnp.exp(-jnp.abs(x)))


def _rglru_kernel(*refs, lay, n_tiles, reverse):
    if reverse:
        (xc_ref, hc_ref, yf_ref, h0_ref, wa_ref, ba_ref, wi_ref, bi_ref, lam_ref, wy_ref,
         o_ref, hl_ref, a_scr, u_scr, y_scr, carry_scr) = refs
    else:
        (xp_ref, x_ref, xn_ref, nm_ref, sh_ref, sc_ref, h0_ref, wx_ref, cw_ref, cb_ref,
         wa_ref, ba_ref, wi_ref, bi_ref, lam_ref,
         o_ref, xc_ref, hc_ref, hl_ref, xw_scr, a_scr, u_scr, carry_scr) = refs
    tm, d = xc_ref.shape
    groups = tm // SUBLANES
    bw = d // RG_BLOCKS
    i = pl.program_id(0)
    t = n_tiles - 1 - i if reverse else i
    r, n = _seq_position(lay, tm, t)

    @pl.when(i == 0)
    def _():
        carry_scr[...] = jnp.zeros_like(carry_scr)

    if reverse:
        xc = xc_ref[...]
    else:
        has_prev, has_next = r > 0, r < n - 1

        def pre(ref):
            return _norm_mod(ref[...], nm_ref[...], sh_ref[...], sc_ref[...]).astype(BF16)

        hc = pre(x_ref)
        hc_ref[...] = hc
        xw = _dot(jnp.concatenate([pre(xp_ref), hc, pre(xn_ref)], axis=0), wx_ref[...])
        xw_scr[0:RG_HALO] = jnp.where(has_prev, xw[0:RG_HALO], 0.0)
        xw_scr[RG_HALO:RG_HALO + tm] = xw[RG_HALO:RG_HALO + tm]
        xw_scr[RG_HALO + tm:] = jnp.where(has_next, xw[RG_HALO + tm:], 0.0)
        xc = cb_ref[...]
        pad_left = (RG_CONV_W - 1) // 2
        for k in range(RG_CONV_W):
            xc = xc + cw_ref[k:k + 1, :] * xw_scr[pl.ds(RG_HALO - pad_left + k, tm), :]
        xc_ref[...] = xc

    row = lax.broadcasted_iota(jnp.int32, (groups, SUBLANES, bw), 1)
    for nb in range(RG_BLOCKS):
        sl = slice(nb * bw, (nb + 1) * bw)
        xcn = xc[:, sl]
        xcb = xcn.astype(BF16)
        rg = jax.nn.sigmoid(_dot(xcb, wa_ref[nb]) + ba_ref[:, sl])
        ig = jax.nn.sigmoid(_dot(xcb, wi_ref[nb]) + bi_ref[:, sl])
        log_a = -RG_C * rg * _softplus(-lam_ref[:, sl])
        a = jnp.exp(log_a)
        u = (jnp.sqrt(-jnp.tanh(log_a) * (a * a + 1.0)) * (ig * xcn)).reshape(groups, SUBLANES, bw)
        a = a.reshape(groups, SUBLANES, bw)
        for k in (1, 2, 4):
            shift = SUBLANES - k if reverse else k
            keep = row < SUBLANES - k if reverse else row >= k
            a_nb = pltpu.roll(a, shift, 1)
            u_nb = pltpu.roll(u, shift, 1)
            u = jnp.where(keep, a * u_nb + u, u)
            a = jnp.where(keep, a * a_nb, a)
        a_scr[:, :, sl] = a
        u_scr[:, :, sl] = u

    is_start = r == n - 1 if reverse else r == 0
    h_init = jnp.where(is_start, h0_ref[...], carry_scr[...])
    dst = y_scr if reverse else o_ref
    edge = 0 if reverse else SUBLANES - 1

    def step(g, h):
        gg = groups - 1 - g if reverse else g
        y = a_scr[gg] * h + u_scr[gg]
        dst[pl.ds(pl.multiple_of(gg * SUBLANES, SUBLANES), SUBLANES), :] = y
        return y[edge:edge + 1, :]

    h_end = lax.fori_loop(0, groups, step, h_init)
    carry_scr[...] = h_end
    hl_ref[...] = h_end
    if reverse:
        gate = jax.nn.gelu(_dot(hc_ref[...], wy_ref[...]))
        o_ref[...] = ((yf_ref[...] + y_scr[...]) * gate).astype(BF16)


def _rglru_layer(lay, x, mod, norm_mix, state, w_x, w_y, conv_w, conv_b, w_a, b_a, w_i, b_i, lam):
    t, d = x.shape
    tm = SEQ_TILE
    n_tiles = t // tm
    hb = tm // SUBLANES
    halo_blocks, n_halo = tm // RG_HALO, t // RG_HALO
    bw = d // RG_BLOCKS

    def run(reverse, *fwd_out):
        dr = int(reverse)
        order = (lambda i: n_tiles - 1 - i) if reverse else (lambda i: i)
        h0 = jnp.concatenate([jnp.zeros((1, d), F32), state[:, dr], jnp.zeros((SUBLANES - 1 - lay.bs, d), F32)])
        h0_spec = lay.row_spec(tm, order)(d)
        tile = pl.BlockSpec((tm, d), lambda i: (order(i), 0))
        gate_specs = [_resident((RG_BLOCKS, bw, bw)), _resident((1, d)), _resident((RG_BLOCKS, bw, bw)),
                      _resident((1, d)), _resident((1, d))]
        gate_args = [w_a[dr].astype(BF16), b_a[dr:dr + 1], w_i[dr].astype(BF16), b_i[dr:dr + 1], lam[dr:dr + 1]]
        scan_scratch = [pltpu.VMEM((hb, SUBLANES, d), F32), pltpu.VMEM((hb, SUBLANES, d), F32)]
        last_spec = pl.BlockSpec((None, 1, d), lambda i: (order(i), 0, 0))
        last_shape = jax.ShapeDtypeStruct((n_tiles, 1, d), F32)
        if reverse:
            xc, hc, y_fwd = fwd_out
            in_specs = [tile, tile, tile, h0_spec] + gate_specs + [_resident((d, d))]
            args = [xc, hc, y_fwd, h0.reshape(SUBLANES, 1, d)] + gate_args + [w_y.astype(BF16)]
            out_shape = (jax.ShapeDtypeStruct((t, d), BF16), last_shape)
            out_specs = (tile, last_spec)
            scratch = scan_scratch + [pltpu.VMEM((tm, d), F32)]
        else:
            in_specs = [pl.BlockSpec((RG_HALO, d), lambda i: (jnp.maximum(i * halo_blocks - 1, 0), 0)),
                        tile,
                        pl.BlockSpec((RG_HALO, d), lambda i: (jnp.minimum((i + 1) * halo_blocks, n_halo - 1), 0)),
                        _resident((1, d)), lay.mod_spec(tm, 0, d), lay.mod_spec(tm, 1, d), h0_spec,
                        _resident((d, d)), _resident((RG_CONV_W, d)), _resident((1, d))] + gate_specs
            args = [x, x, x, norm_mix, mod, mod, h0.reshape(SUBLANES, 1, d), w_x.astype(BF16), conv_w, conv_b] + gate_args
            out_shape = (jax.ShapeDtypeStruct((t, d), F32), jax.ShapeDtypeStruct((t, d), F32),
                         jax.ShapeDtypeStruct((t, d), BF16), last_shape)
            out_specs = (tile, tile, tile, last_spec)
            scratch = [pltpu.VMEM((tm + 2 * RG_HALO, d), F32)] + scan_scratch
        scratch.append(pltpu.VMEM((1, d), F32))
        return pl.pallas_call(
            functools.partial(_rglru_kernel, lay=lay, n_tiles=n_tiles, reverse=reverse),
            out_shape=out_shape,
            grid=(n_tiles,),
            in_specs=in_specs,
            out_specs=out_specs,
            scratch_shapes=scratch,
            compiler_params=_params(1, "arbitrary"),
            name="rglru_bwd" if reverse else "rglru_fwd",
        )(*args)

    y_fwd, xc, hc, last_f = run(False)
    m, last_b = run(True, xc, hc, y_fwd)
    ppb = lay.sp // tm
    tail = last_f[:lay.prompt_tiles(tm)].reshape(lay.bp, ppb, d)[:, ppb - 1]
    head = last_b[:lay.prompt_tiles(tm)].reshape(lay.bp, ppb, d)[:, 0]
    return m, jnp.stack([tail, head], axis=1)[:, None]


def _dft_tables(n, scale):
    jk = np.outer(np.arange(n), np.arange(n)) % n
    ang = 2.0 * np.pi * jk / n
    return np.cos(ang) * scale, np.sin(ang) * scale


def _fnet_channel_table(d):
    gw = d // FNET_GROUPS
    cos, sin = _dft_tables(gw, gw ** -0.5)
    return jnp.asarray(np.concatenate([cos, sin], axis=1), F32).astype(BF16)


def _fnet_position_kernel(wc_ref, ws_ref, zc_ref, zs_ref, o_ref):
    o_ref[...] = (_dot(wc_ref[...], zc_ref[...]) - _dot(ws_ref[...], zs_ref[...])).astype(BF16)


def _fnet_position(zc, zs, *, n_seq, seq_len, row0, tm):
    d = zc.shape[1]
    assert row0 % seq_len == 0 and seq_len % tm == 0
    cos, sin = _dft_tables(seq_len, seq_len ** -0.5)
    wc, ws = jnp.asarray(cos, F32).astype(BF16), jnp.asarray(sin, F32).astype(BF16)
    sb0, rt = row0 // seq_len, seq_len // tm
    w_spec = pl.BlockSpec((tm, seq_len), lambda b, j: (j, 0))
    z_spec = pl.BlockSpec((seq_len, d), lambda b, j: (sb0 + b, 0))
    return pl.pallas_call(
        _fnet_position_kernel,
        out_shape=jax.ShapeDtypeStruct((n_seq * seq_len, d), BF16),
        grid=(n_seq, rt),
        in_specs=[w_spec, w_spec, z_spec, z_spec],
        out_specs=pl.BlockSpec((tm, d), lambda b, j: (b * rt + j, 0)),
        compiler_params=_params(2),
        name="fnet_position",
    )(wc, ws, zc, zs)


def _fnet_layer(lay, zc, zs):
    return (_fnet_position(zc, zs, n_seq=lay.bp, seq_len=lay.sp, row0=0, tm=lay.sp),
            _fnet_position(zc, zs, n_seq=lay.bs, seq_len=lay.ss, row0=lay.tp, tm=min(FNET_ROW_TILE, lay.ss)))


def _conformer_kernel(xp_ref, xc_ref, xn_ref, nm_ref, sh_ref, sc_ref, w1_ref, b1_ref, dw_ref, db_ref, lg_ref, lb_ref,
                      o_ref, src_scr, acc_scr, *, lay):
    tm, d = xc_ref.shape
    r, n = _seq_position(lay, tm, pl.program_id(0))
    has_prev, has_next = r > 0, r < n - 1

    def pre(ref):
        return _norm_mod(ref[...], nm_ref[...], sh_ref[...], sc_ref[...]).astype(BF16)

    z = _dot(jnp.concatenate([pre(xp_ref), pre(xc_ref), pre(xn_ref)], axis=0), w1_ref[...]) + b1_ref[...]
    glu = z[:, :d] * jax.nn.sigmoid(z[:, d:])
    pieces = ((0, CONF_HALO, has_prev), (CONF_HALO, CONF_HALO + tm, None), (CONF_HALO + tm, tm + 2 * CONF_HALO, has_next))
    for lo, hi, valid in pieces:
        part = glu[lo:hi] if valid is None else jnp.where(valid, glu[lo:hi], 0.0)
        for c in range(d // LANES):
            src_scr[0, c, lo:hi, :] = part[:, c * LANES:(c + 1) * LANES]
    n_rows = tm + 2 * CONF_HALO - SUBLANES
    for s in range(1, SUBLANES):
        for c in range(d // LANES):
            src_scr[s, c, 0:n_rows, :] = src_scr[0, c, pl.ds(s, n_rows), :]
    pad_left = (CONF_CONV_W - 1) // 2
    taps = [divmod(CONF_HALO - pad_left + k, SUBLANES) for k in range(CONF_CONV_W)]
    for c in range(d // LANES):
        lanes = slice(c * LANES, (c + 1) * LANES)
        w = [jnp.broadcast_to(dw_ref[k:k + 1, lanes], (SUBLANES, LANES)) for k in range(CONF_CONV_W)]
        bias = jnp.broadcast_to(db_ref[:, lanes], (SUBLANES, LANES))

        def conv_rows(g, carry, c=c, lanes=lanes, w=w, bias=bias):
            row0 = pl.multiple_of(g * SUBLANES, SUBLANES)
            part = [bias] + [None] * (CONF_CHAINS - 1)
            for k, (q, s) in enumerate(taps):
                term = w[k] * src_scr[s, c, pl.ds(row0 + q * SUBLANES, SUBLANES), :]
                j = k % CONF_CHAINS
                part[j] = term if part[j] is None else part[j] + term
            while len(part) > 1:
                part = [a + b for a, b in zip(part[0::2], part[1::2])]
            acc_scr[pl.ds(row0, SUBLANES), lanes] = part[0]
            return carry

        lax.fori_loop(0, tm // SUBLANES, conv_rows, 0, unroll=2)
    acc = acc_scr[...]
    mu = jnp.mean(acc, axis=-1, keepdims=True)
    cen = acc - mu
    var = jnp.mean(cen * cen, axis=-1, keepdims=True)
    y = cen * lax.rsqrt(var + EPS) * lg_ref[...] + lb_ref[...]
    o_ref[...] = (y * jax.nn.sigmoid(y)).astype(BF16)


def _conformer_layer(lay, x, mod, norm_mix, w_pw1, b_pw1, dw_w, dw_b, ln_g, ln_b):
    t, d = x.shape
    tm = SEQ_TILE
    hb = tm // CONF_HALO
    n_halo = t // CONF_HALO
    tile = pl.BlockSpec((tm, d), lambda i: (i, 0))
    return pl.pallas_call(
        functools.partial(_conformer_kernel, lay=lay),
        out_shape=jax.ShapeDtypeStruct((t, d), BF16),
        grid=(t // tm,),
        in_specs=[pl.BlockSpec((CONF_HALO, d), lambda i: (jnp.maximum(i * hb - 1, 0), 0)),
                  tile,
                  pl.BlockSpec((CONF_HALO, d), lambda i: (jnp.minimum((i + 1) * hb, n_halo - 1), 0)),
                  _resident((1, d)), lay.mod_spec(tm, 0, d), lay.mod_spec(tm, 1, d),
                  _resident((d, 2 * d)), _resident((1, 2 * d)), _resident((CONF_CONV_W, d)), _resident((1, d)),
                  _resident((1, d)), _resident((1, d))],
        out_specs=tile,
        scratch_shapes=[pltpu.VMEM((SUBLANES, d // LANES, tm + 2 * CONF_HALO + SUBLANES, LANES), F32),
                        pltpu.VMEM((tm, d), F32)],
        compiler_params=_params(1),
        name="conformer",
    )(x, x, x, norm_mix, mod, mod, w_pw1.astype(BF16), b_pw1, dw_w, dw_b, ln_g, ln_b)


def kernel(x_prompt, x_sample, cache_mla_ckv, cache_mla_krope, state_rglru, c, c_ctx, ada_w, ada_b, norm_mix, norm_ffn, mla_w_dq, mla_q_norm, mla_w_uq, mla_w_dkv, mla_kv_norm, mla_w_ukv, mla_w_o, rg_w_x, rg_w_y, rg_conv_w, rg_conv_b, rg_w_a, rg_b_a, rg_w_i, rg_b_i, rg_lam, rg_w_o, fn_w_o, fn_b_o, cf_w_pw1, cf_b_pw1, cf_dw_w, cf_dw_b, cf_ln_g, cf_ln_b, cf_w_pw2, cf_b_pw2, ffn_w_gate, ffn_w_up, ffn_w_down, final_norm):
    bp, sp, d = x_prompt.shape
    bs, ss, _ = x_sample.shape
    depth = ada_w.shape[0]
    assert depth == 4 and bs < SUBLANES and sp == SEQ_TILE and ss % GRID_W == 0
    lay = _Layout(bp, sp, bs, ss)
    x_in = (x_prompt.reshape(lay.tp, d), x_sample.reshape(lay.ts, d))

    cvec = jnp.concatenate([c_ctx[None], c, jnp.zeros((SUBLANES - 1 - bs, d), F32)], axis=0)
    mods = _ada_tables(cvec, ada_w, ada_b).reshape(depth, SUBLANES, 1, -1)
    zero_bias = jnp.zeros((1, d), F32)
    w_gate, w_up, w_down = ffn_w_gate.astype(BF16), ffn_w_up.astype(BF16), ffn_w_down.astype(BF16)
    norm_ffn3 = norm_ffn[:, None, :]

    def ffn(layer, x, m, w_out, b_out, **tail):
        return _ffn_layer(lay, layer, x, m, w_out.astype(BF16), b_out, mods[layer], norm_ffn3, w_gate, w_up, w_down,
                          **tail)

    m, ckv_state, krope_state = _mla_layer(
        lay, *x_in, mods[0], norm_mix[0:1], cache_mla_ckv[:, 0], cache_mla_krope[:, 0], mla_w_dq[0], mla_q_norm[0:1],
        mla_w_uq[0], mla_w_dkv[0], mla_kv_norm[0:1], mla_w_ukv[0])
    x = ffn(0, x_in, m, mla_w_o[0], zero_bias)

    m, rg_state = _rglru_layer(lay, x, mods[1], norm_mix[1:2], state_rglru[:, 0], rg_w_x[0], rg_w_y[0], rg_conv_w[0],
                               rg_conv_b[0:1], rg_w_a[0], rg_b_a[0], rg_w_i[0], rg_b_i[0], rg_lam[0])
    x, zc, zs = ffn(1, x, m, rg_w_o[0], zero_bias, fnet_next=(norm_mix[2:3], mods[2]))

    m = _fnet_layer(lay, zc, zs)
    x = ffn(2, x, m, fn_w_o[0], fn_b_o[0:1])

    m = _conformer_layer(lay, x, mods[3], norm_mix[3:4], cf_w_pw1[0], cf_b_pw1[0:1], cf_dw_w[0], cf_dw_b[0:1],
                         cf_ln_g[0:1], cf_ln_b[0:1])
    y_prompt, y_sample = ffn(3, x, m, cf_w_pw2[0], cf_b_pw2[0:1], final_norm=final_norm[None])

    return (y_prompt.reshape(bp, sp, d), y_sample.reshape(bs, ss, d), ckv_state, krope_state, rg_state)
```

```python
import functools

import numpy as np
import jax
import jax.numpy as jnp
from jax import lax
from jax.experimental import pallas as pl
from jax.experimental.pallas import tpu as pltpu

F32 = jnp.float32
BF16 = jnp.bfloat16

EPS = 1e-6
GRID_W = 64
MLA_HEADS = 8
KV_LORA_RANK = 256
QK_NOPE_DIM = 128
QK_ROPE_DIM = 64
V_HEAD_DIM = 128
ROPE_THETA = 10000.0
RG_BLOCKS = 4
RG_CONV_W = 4
RG_C = 8.0
FNET_GROUPS = 4
CONF_CONV_W = 31

LANES = 128
SUBLANES = 8
HEAD_W = 2 * LANES
SEQ_TILE = 256
FFN_TILE = 512
ATTN_Q_TILE = 1024
ATTN_SUB_TILE = 256
FNET_ROW_TILE = 512
FFN_CHUNK = 256
CONF_HALO = 16
RG_HALO = 16
CONF_CHAINS = 4
VMEM_LIMIT = 52 * 1024 * 1024


def _dot(a, b):
    return jnp.dot(a, b, preferred_element_type=F32)


def _rms(x, g):
    return x * lax.rsqrt(jnp.mean(x * x, axis=-1, keepdims=True) + EPS) * g


def _norm_mod(x, g, shift, scale):
    return _rms(x, g) * (1.0 + scale) + shift


def _resident(shape):
    nd = len(shape)
    return pl.BlockSpec(shape, lambda *_: (0,) * nd, pipeline_mode=pl.Buffered(1))


def _resident_at(index, shape):
    nd = len(shape)
    return pl.BlockSpec((None,) + tuple(shape), lambda *_: (index,) + (0,) * nd, pipeline_mode=pl.Buffered(1))


def _params(n_axes, semantics="parallel"):
    return pltpu.CompilerParams(dimension_semantics=(semantics,) * n_axes, vmem_limit_bytes=VMEM_LIMIT)


class _Layout:
    def __init__(self, n_prompt_seq, prompt_len, n_sample_seq, sample_len):
        self.bp, self.sp, self.bs, self.ss = n_prompt_seq, prompt_len, n_sample_seq, sample_len
        self.tp = n_prompt_seq * prompt_len
        self.ts = n_sample_seq * sample_len
        self.t = self.tp + self.ts

    def prompt_tiles(self, tile):
        assert self.tp % tile == 0 and self.ss % tile == 0
        return self.tp // tile

    def sample_tiles_per_seq(self, tile):
        return self.ss // tile

    def mod_row(self, tile):
        npt, spb = self.prompt_tiles(tile), self.sample_tiles_per_seq(tile)
        return lambda i: jnp.where(i < npt, 0, 1 + (i - npt) // spb)

    def mod_spec(self, tile, chunk, d, order=lambda i: i):
        row = self.mod_row(tile)
        return pl.BlockSpec((None, 1, d), lambda i: (row(order(i)), 0, chunk))

    def row_spec(self, tile, order=lambda i: i):
        row = self.mod_row(tile)
        return lambda d: pl.BlockSpec((None, 1, d), lambda i: (row(order(i)), 0, 0))


def _ada_kernel(c_ref, w_ref, b_ref, o_ref):
    c = c_ref[...]
    s = (c * jax.nn.sigmoid(c)).astype(BF16)
    o_ref[...] = _dot(s, w_ref[...].astype(BF16)) + b_ref[...]


def _ada_tables(cvec, ada_w, ada_b):
    depth, d, n = ada_w.shape
    tn = n // 4
    return pl.pallas_call(
        _ada_kernel,
        out_shape=jax.ShapeDtypeStruct((depth, SUBLANES, n), F32),
        grid=(depth, n // tn),
        in_specs=[
            pl.BlockSpec((SUBLANES, d), lambda l, j: (0, 0)),
            pl.BlockSpec((None, d, tn), lambda l, j: (l, 0, j)),
            pl.BlockSpec((None, 1, tn), lambda l, j: (l, 0, j)),
        ],
        out_specs=pl.BlockSpec((None, SUBLANES, tn), lambda l, j: (l, 0, j)),
        compiler_params=_params(2),
        name="ada_tables",
    )(cvec, ada_w, ada_b.reshape(depth, 1, n))


def _read_tokens(refs, n_prompt_tiles):
    if len(refs) == 1:
        return refs[0][...]
    return jnp.where(pl.program_id(0) < n_prompt_tiles, refs[0][...], refs[1][...])


def _fnet_channel_dft(h, w_ref, zc_ref, zs_ref):
    gw = w_ref.shape[0]
    for g in range(FNET_GROUPS):
        sl = slice(g * gw, (g + 1) * gw)
        f = _dot(h[:, sl], w_ref[...])
        zc_ref[:, sl] = f[:, :gw].astype(BF16)
        zs_ref[:, sl] = f[:, gw:].astype(BF16)


def _ffn_kernel(*refs, n_chunks, n_x, n_m, tail, n_prompt_tiles):
    x_refs, m_refs, refs = refs[:n_x], refs[n_x:n_x + n_m], refs[n_x + n_m:]
    wo_ref, bo_ref, g1_ref, sh_ref, sc_ref, g2_ref, nf_ref, wg_ref, wu_ref, wd_ref = refs[:10]
    refs, a_ref = refs[10:-1], refs[-1]
    m = _read_tokens(m_refs, n_prompt_tiles)
    x1 = _read_tokens(x_refs, n_prompt_tiles) + g1_ref[...] * (_dot(m, wo_ref[...]) + bo_ref[...])
    h = _norm_mod(x1, nf_ref[...], sh_ref[...], sc_ref[...]).astype(BF16)
    for c in range(n_chunks):
        sl = slice(c * FFN_CHUNK, (c + 1) * FFN_CHUNK)
        g = _dot(h, wg_ref[:, sl])
        u = _dot(h, wu_ref[:, sl])
        a_ref[:, sl] = (g * jax.nn.sigmoid(g) * u).astype(BF16)
    x2 = x1 + g2_ref[...] * _dot(a_ref[...], wd_ref[...])
    if tail is None:
        refs[0][...] = x2
    elif tail == "fnet":
        nm_ref, sh1_ref, sc1_ref, dft_ref, o_ref, zc_ref, zs_ref = refs
        o_ref[...] = x2
        h_next = _norm_mod(x2, nm_ref[...], sh1_ref[...], sc1_ref[...]).astype(BF16)
        _fnet_channel_dft(h_next, dft_ref, zc_ref, zs_ref)
    else:
        fin_ref, op_ref, os_ref = refs
        y = _rms(x2, fin_ref[...])
        is_prompt = pl.program_id(0) < n_prompt_tiles

        @pl.when(is_prompt)
        def _():
            op_ref[...] = y

        @pl.when(jnp.logical_not(is_prompt))
        def _():
            os_ref[...] = y


def _ffn_layer(lay, layer, x, m, w_out, b_out, mod, norm_ffn, w_gate, w_up, w_down, final_norm=None, fnet_next=None):
    xs = x if isinstance(x, tuple) else (x,)
    ms = m if isinstance(m, tuple) else (m,)
    d = xs[0].shape[1]
    dff = w_gate.shape[2]
    assert dff % FFN_CHUNK == 0
    tm = FFN_TILE
    npt = lay.prompt_tiles(tm)
    tile = pl.BlockSpec((tm, d), lambda i: (i, 0))
    split = [pl.BlockSpec((tm, d), lambda i: (jnp.minimum(i, npt - 1), 0)),
             pl.BlockSpec((tm, d), lambda i: (jnp.maximum(i - npt, 0), 0))]
    in_specs = (split if len(xs) == 2 else [tile]) + (split if len(ms) == 2 else [tile]) + [
        _resident((d, d)), _resident((1, d)),
        lay.mod_spec(tm, 2, d), lay.mod_spec(tm, 3, d), lay.mod_spec(tm, 4, d), lay.mod_spec(tm, 5, d),
        _resident_at(layer, (1, d)), _resident_at(layer, (d, dff)), _resident_at(layer, (d, dff)),
        _resident_at(layer, (dff, d))]
    args = list(xs) + list(ms) + [w_out, b_out, mod, mod, mod, mod, norm_ffn, w_gate, w_up, w_down]
    out_shape, out_specs, tail = jax.ShapeDtypeStruct((lay.t, d), F32), tile, None
    if final_norm is not None:
        tail = "final"
        in_specs.append(_resident((1, d)))
        args.append(final_norm)
        out_shape = (jax.ShapeDtypeStruct((lay.tp, d), F32), jax.ShapeDtypeStruct((lay.ts, d), F32))
        out_specs = tuple(split)
    elif fnet_next is not None:
        tail = "fnet"
        norm_next, mod_next = fnet_next
        dft = _fnet_channel_table(d)
        in_specs += [_resident((1, d)), lay.mod_spec(tm, 0, d), lay.mod_spec(tm, 1, d), _resident(dft.shape)]
        args += [norm_next, mod_next, mod_next, dft]
        out_shape = (out_shape, jax.ShapeDtypeStruct((lay.t, d), BF16), jax.ShapeDtypeStruct((lay.t, d), BF16))
        out_specs = (tile, tile, tile)
    return pl.pallas_call(
        functools.partial(_ffn_kernel, n_chunks=dff // FFN_CHUNK, n_x=len(xs), n_m=len(ms), tail=tail,
                          n_prompt_tiles=npt),
        out_shape=out_shape,
        grid=(lay.t // tm,),
        in_specs=in_specs,
        out_specs=out_specs,
        scratch_shapes=[pltpu.VMEM((tm, dff), BF16)],
        compiler_params=_params(1, "arbitrary" if tail == "final" else "parallel"),
        name="ffn",
    )(*args)


def _mla_qkv_kernel(xp_ref, xs_ref, nm_ref, sh_ref, sc_ref, cos_ref, sin_ref, wdq_ref, qn_ref, wqn_ref, wqr_ref, wqx_ref,
                    wdkv_ref, wdkvx_ref, kvn_ref, q_ref, z_ref, *, scale, n_prompt_tiles):
    x = _read_tokens((xp_ref, xs_ref), n_prompt_tiles)
    h = _norm_mod(x, nm_ref[...], sh_ref[...], sc_ref[...]).astype(BF16)
    cos, sin = cos_ref[...], sin_ref[...]
    cq = _rms(_dot(h, wdq_ref[...]), qn_ref[...]).astype(BF16)
    q_nope = _dot(cq, wqn_ref[...])
    q_rope = _dot(cq, wqr_ref[...])
    q_swap = _dot(cq, wqx_ref[...])
    for hd in range(MLA_HEADS):
        sl = slice(hd * LANES, (hd + 1) * LANES)
        q_ref[:, hd * HEAD_W:hd * HEAD_W + LANES] = (q_nope[:, sl] * scale).astype(BF16)
        q_ref[:, hd * HEAD_W + LANES:(hd + 1) * HEAD_W] = ((q_rope[:, sl] * cos + q_swap[:, sl] * sin) * scale).astype(BF16)
    z = _dot(h, wdkv_ref[...])
    z_swap = _dot(h, wdkvx_ref[...])
    z_ref[:, :KV_LORA_RANK] = _rms(z[:, :KV_LORA_RANK], kvn_ref[...])
    z_ref[:, KV_LORA_RANK:] = z[:, KV_LORA_RANK:] * cos + z_swap * sin


def _kv_expand_kernel(z_ref, wk_ref, wv_ref, k_ref, v_ref):
    z = z_ref[...]
    ckv = z[:, :KV_LORA_RANK].astype(BF16)
    k_rope = z[:, KV_LORA_RANK:].astype(BF16)
    k_nope = _dot(ckv, wk_ref[...])
    for hd in range(MLA_HEADS):
        k_ref[:, hd * HEAD_W:hd * HEAD_W + LANES] = k_nope[:, hd * LANES:(hd + 1) * LANES].astype(BF16)
        k_ref[:, hd * HEAD_W + LANES:(hd + 1) * HEAD_W] = k_rope
    v_ref[...] = _dot(ckv, wv_ref[...]).astype(BF16)


def _kv_expand(z, wk, wv, n=None):
    zw = z.shape[1]
    n = z.shape[0] if n is None else n
    tm = 2 * SEQ_TILE if n % (2 * SEQ_TILE) == 0 else SEQ_TILE
    return pl.pallas_call(
        _kv_expand_kernel,
        out_shape=(jax.ShapeDtypeStruct((n, MLA_HEADS * HEAD_W), BF16),
                   jax.ShapeDtypeStruct((n, MLA_HEADS * V_HEAD_DIM), BF16)),
        grid=(n // tm,),
        in_specs=[pl.BlockSpec((tm, zw), lambda i: (i, 0)), _resident(wk.shape), _resident(wv.shape)],
        out_specs=(pl.BlockSpec((tm, MLA_HEADS * HEAD_W), lambda i: (i, 0)),
                   pl.BlockSpec((tm, MLA_HEADS * V_HEAD_DIM), lambda i: (i, 0))),
        compiler_params=_params(1),
        name="mla_kv_expand",
    )(z, wk, wv)


def _attn_kernel(*refs, heads, sub):
    q_ref, k_ref, v_ref = refs[:3]
    o_ref = refs[-1]
    tq = q_ref.shape[0]
    for hd in range(heads):
        k = k_ref[:, hd * HEAD_W:(hd + 1) * HEAD_W]
        v = v_ref[:, hd * V_HEAD_DIM:(hd + 1) * V_HEAD_DIM]
        for j in range(tq // sub):
            rows = slice(j * sub, (j + 1) * sub)
            q = q_ref[rows, hd * HEAD_W:(hd + 1) * HEAD_W]
            s = lax.dot_general(q, k, (((1,), (1,)), ((), ())), preferred_element_type=F32)
            p = jnp.exp(s - jnp.max(s, axis=-1, keepdims=True))
            den = jnp.sum(p, axis=-1, keepdims=True)
            o = _dot(p.astype(BF16), v)
            o_ref[rows, hd * V_HEAD_DIM:(hd + 1) * V_HEAD_DIM] = (o / den).astype(BF16)


def _attention(q, k, v, *, n_seq, q_len, k_len, q_row0, tq, heads):
    hg = MLA_HEADS // heads
    assert q_row0 % tq == 0 and q_len % tq == 0
    qb0, qt = q_row0 // tq, q_len // tq
    return pl.pallas_call(
        functools.partial(_attn_kernel, heads=heads, sub=min(ATTN_SUB_TILE, tq)),
        out_shape=jax.ShapeDtypeStruct((n_seq * q_len, MLA_HEADS * V_HEAD_DIM), BF16),
        grid=(n_seq, hg, qt),
        in_specs=[pl.BlockSpec((tq, heads * HEAD_W), lambda b, g, j: (qb0 + b * qt + j, g)),
                  pl.BlockSpec((k_len, heads * HEAD_W), lambda b, g, j: (b, g)),
                  pl.BlockSpec((k_len, heads * V_HEAD_DIM), lambda b, g, j: (b, g))],
        out_specs=pl.BlockSpec((tq, heads * V_HEAD_DIM), lambda b, g, j: (b * qt + j, g)),
        compiler_params=_params(3),
        name="mla_attention",
    )(q, k, v)


def _rope_tables(lay, tile):
    pos = np.arange(lay.ss)
    n_freq = QK_ROPE_DIM // 4
    inv_freq = ROPE_THETA ** (-np.arange(n_freq, dtype=np.float64) / n_freq)
    ang = np.concatenate([(pos // GRID_W)[:, None] * inv_freq, (pos % GRID_W)[:, None] * inv_freq], axis=-1)
    pad = np.zeros((lay.ss, LANES - QK_ROPE_DIM))
    cos = np.concatenate([np.cos(ang), np.cos(ang), pad], axis=-1)
    sin = np.concatenate([np.sin(ang), np.sin(ang), pad], axis=-1)
    ident_cos = np.concatenate([np.ones((tile, QK_ROPE_DIM)), np.zeros((tile, LANES - QK_ROPE_DIM))], axis=-1)
    cos = np.concatenate([ident_cos, cos], axis=0)
    sin = np.concatenate([np.zeros((tile, LANES)), sin], axis=0)
    return jnp.asarray(cos, F32), jnp.asarray(sin, F32)


def _swap_halves(w):
    half = QK_ROPE_DIM // 2
    return jnp.concatenate([-w[..., half:], w[..., :half]], axis=-1)


def _mla_layer(lay, x_prompt, x_sample, mod, norm_mix, cache_ckv, cache_krope, w_dq, q_norm, w_uq, w_dkv, kv_norm,
               w_ukv):
    t, d = lay.t, x_prompt.shape[1]
    tm = SEQ_TILE
    rq = w_dq.shape[1]
    qk = QK_NOPE_DIM + QK_ROPE_DIM
    zw = KV_LORA_RANK + LANES
    rope_pad = [(0, 0)] * 2 + [(0, LANES - QK_ROPE_DIM)]

    wq = w_uq.reshape(rq, MLA_HEADS, qk)
    wq_nope = wq[:, :, :QK_NOPE_DIM].reshape(rq, -1).astype(BF16)
    wq_rope = jnp.pad(wq[:, :, QK_NOPE_DIM:], rope_pad).reshape(rq, -1).astype(BF16)
    wq_swap = jnp.pad(_swap_halves(wq[:, :, QK_NOPE_DIM:]), rope_pad).reshape(rq, -1).astype(BF16)
    wdkv = jnp.pad(w_dkv, [(0, 0), (0, LANES - QK_ROPE_DIM)]).astype(BF16)
    wdkv_swap = jnp.pad(_swap_halves(w_dkv[:, KV_LORA_RANK:]), [(0, 0), (0, LANES - QK_ROPE_DIM)]).astype(BF16)
    wkv = w_ukv.reshape(KV_LORA_RANK, MLA_HEADS, QK_NOPE_DIM + V_HEAD_DIM)
    wk = wkv[:, :, :QK_NOPE_DIM].reshape(KV_LORA_RANK, -1).astype(BF16)
    wv = wkv[:, :, QK_NOPE_DIM:].reshape(KV_LORA_RANK, -1).astype(BF16)

    cos, sin = _rope_tables(lay, tm)
    npt, spb = lay.prompt_tiles(tm), lay.sample_tiles_per_seq(tm)
    rope_spec = pl.BlockSpec((tm, LANES), lambda i: (jnp.where(i < npt, 0, 1 + (i - npt) % spb), 0))
    q, z = pl.pallas_call(
        functools.partial(_mla_qkv_kernel, scale=qk ** -0.5, n_prompt_tiles=npt),
        out_shape=(jax.ShapeDtypeStruct((t, MLA_HEADS * HEAD_W), BF16), jax.ShapeDtypeStruct((t, zw), F32)),
        grid=(t // tm,),
        in_specs=[pl.BlockSpec((tm, d), lambda i: (jnp.minimum(i, npt - 1), 0)),
                  pl.BlockSpec((tm, d), lambda i: (jnp.maximum(i - npt, 0), 0)), _resident((1, d)),
                  lay.mod_spec(tm, 0, d), lay.mod_spec(tm, 1, d), rope_spec, rope_spec,
                  _resident(w_dq.shape), _resident((1, rq)), _resident(wq_nope.shape), _resident(wq_rope.shape),
                  _resident(wq_swap.shape), _resident(wdkv.shape), _resident(wdkv_swap.shape),
                  _resident((1, KV_LORA_RANK))],
        out_specs=(pl.BlockSpec((tm, MLA_HEADS * HEAD_W), lambda i: (i, 0)), pl.BlockSpec((tm, zw), lambda i: (i, 0))),
        compiler_params=_params(1),
        name="mla_qkv",
    )(x_prompt, x_sample, norm_mix, mod, mod, cos, sin, w_dq.astype(BF16), q_norm, wq_nope, wq_rope, wq_swap, wdkv,
      wdkv_swap, kv_norm)

    past = cache_ckv.shape[1]
    z_cache = jnp.concatenate(
        [cache_ckv, cache_krope, jnp.zeros((lay.bs, past, LANES - QK_ROPE_DIM), F32)], axis=-1)
    z_sample = jnp.concatenate([z_cache, z[lay.tp:].reshape(lay.bs, lay.ss, zw)], axis=1)
    k_len = past + lay.ss
    kp, vp = _kv_expand(z, wk, wv, lay.tp)
    ks, vs = _kv_expand(z_sample.reshape(lay.bs * k_len, zw), wk, wv)

    o_prompt = _attention(q, kp, vp, n_seq=lay.bp, q_len=lay.sp, k_len=lay.sp, q_row0=0, tq=lay.sp, heads=MLA_HEADS)
    o_sample = _attention(q, ks, vs, n_seq=lay.bs, q_len=lay.ss, k_len=k_len, q_row0=lay.tp,
                          tq=min(ATTN_Q_TILE, lay.ss), heads=1)
    ckv_state = z[:lay.tp, :KV_LORA_RANK].reshape(lay.bp, 1, lay.sp, KV_LORA_RANK)
    krope_state = z[:lay.tp, KV_LORA_RANK:KV_LORA_RANK + QK_ROPE_DIM].reshape(lay.bp, 1, lay.sp, QK_ROPE_DIM)
    return (o_prompt, o_sample), ckv_state, krope_state


def _seq_position(lay, tile, t):
    npt, spb = lay.prompt_tiles(tile), lay.sample_tiles_per_seq(tile)
    ppb = lay.sp // tile
    r = jnp.where(t < npt, t % ppb, (t - npt) % spb)
    n = jnp.where(t < npt, ppb, spb)
    return r, n


def _softplus(x):
    return jnp.maximum(x, 0.0) + jnp.log1p(jnp.exp(-jnp.abs(x)))


def _rglru_kernel(*refs, lay, n_tiles, reverse):
    if reverse:
        (xc_ref, hc_ref, yf_ref, h0_ref, wa_ref, ba_ref, wi_ref, bi_ref, lam_ref, wy_ref,
         o_ref, hl_ref, a_scr, u_scr, carry_scr) = refs
    else:
        (xp_ref, x_ref, xn_ref, nm_ref, sh_ref, sc_ref, h0_ref, wx_ref, cw_ref, cb_ref,
         wa_ref, ba_ref, wi_ref, bi_ref, lam_ref,
         o_ref, xc_ref, hc_ref, hl_ref, xw_scr, a_scr, u_scr, carry_scr) = refs
    tm, d = xc_ref.shape
    groups = tm // SUBLANES
    bw = d // RG_BLOCKS
    i = pl.program_id(0)
    t = n_tiles - 1 - i if reverse else i
    r, n = _seq_position(lay, tm, t)

    @pl.when(i == 0)
    def _():
        carry_scr[...] = jnp.zeros_like(carry_scr)

    if reverse:
        xc = xc_ref[...]
    else:
        has_prev, has_next = r > 0, r < n - 1

        def pre(ref):
            return _norm_mod(ref[...], nm_ref[...], sh_ref[...], sc_ref[...]).astype(BF16)

        hc = pre(x_ref)
        hc_ref[...] = hc
        xw = _dot(jnp.concatenate([pre(xp_ref), hc, pre(xn_ref)], axis=0), wx_ref[...])
        sub = lax.broadcasted_iota(jnp.int32, (SUBLANES, d), 0)

        def tile_group(j):
            return xw[RG_HALO + j * SUBLANES:RG_HALO + (j + 1) * SUBLANES]

        def halo_row(k, valid):
            return jnp.broadcast_to(jnp.where(valid, xw[k:k + 1], 0.0), (SUBLANES, d))

        before = jnp.where(sub == 0, halo_row(RG_HALO - 1, has_prev), pltpu.roll(tile_group(groups - 1), 1, 0))
        after = [jnp.where(sub == SUBLANES - 1, halo_row(RG_HALO + tm + j * SUBLANES, has_next),
                           pltpu.roll(tile_group(j), SUBLANES - 1, 0)) for j in range(RG_CONV_W - 2)]
        xw_scr[0:SUBLANES] = before
        xw_scr[SUBLANES:SUBLANES + tm] = xw[RG_HALO:RG_HALO + tm]
        for j, grp in enumerate(after):
            xw_scr[(groups + 1 + j) * SUBLANES:(groups + 2 + j) * SUBLANES] = grp
        xc = cb_ref[...]
        for k in range(RG_CONV_W):
            xc = xc + cw_ref[k:k + 1, :] * xw_scr[k * SUBLANES:k * SUBLANES + tm]
        xc_ref[...] = xc

    for nb in range(RG_BLOCKS):
        sl = slice(nb * bw, (nb + 1) * bw)
        xcn = xc[:, sl]
        xcb = xcn.astype(BF16)
        rg = jax.nn.sigmoid(_dot(xcb, wa_ref[nb]) + ba_ref[:, sl])
        ig = jax.nn.sigmoid(_dot(xcb, wi_ref[nb]) + bi_ref[:, sl])
        log_a = -RG_C * rg * _softplus(-lam_ref[:, sl])
        a = jnp.exp(log_a)
        u = jnp.sqrt(-jnp.tanh(log_a) * (a * a + 1.0)) * (ig * xcn)
        a_scr[:, :, sl] = a.reshape(groups, SUBLANES, bw)
        u_scr[:, :, sl] = u.reshape(groups, SUBLANES, bw)

    def local_step(g, carry):
        h, p = carry
        j = groups - 1 - g if reverse else g
        a = a_scr[j]
        h = a * h + u_scr[j]
        p = a * p
        u_scr[j] = h
        a_scr[j] = p
        return h, p

    h_seg, p_seg = lax.fori_loop(0, groups, local_step, (jnp.zeros((SUBLANES, d), F32), jnp.ones((SUBLANES, d), F32)))
    is_start = r == n - 1 if reverse else r == 0
    state = jnp.where(is_start, h0_ref[...], carry_scr[...])
    entering = [None] * SUBLANES
    for s in (range(SUBLANES - 1, -1, -1) if reverse else range(SUBLANES)):
        entering[s] = state
        state = h_seg[s:s + 1] + p_seg[s:s + 1] * state
    carry_scr[...] = state
    hl_ref[...] = state
    y = u_scr[...] + a_scr[...] * jnp.concatenate(entering, axis=0)[None]
    y = y.reshape(tm, d)
    if reverse:
        gate = jax.nn.gelu(_dot(hc_ref[...], wy_ref[...]))
        o_ref[...] = ((yf_ref[...] + y) * gate).astype(BF16)
    else:
        o_ref[...] = y


def _rglru_layer(lay, x, mod, norm_mix, state, w_x, w_y, conv_w, conv_b, w_a, b_a, w_i, b_i, lam):
    t, d = x.shape
    tm = SEQ_TILE
    n_tiles = t // tm
    hb = tm // SUBLANES
    halo_blocks, n_halo = tm // RG_HALO, t // RG_HALO
    bw = d // RG_BLOCKS
    x = x.reshape(n_tiles, SUBLANES, hb, d).swapaxes(1, 2).reshape(t, d)

    def run(reverse, *fwd_out):
        dr = int(reverse)
        order = (lambda i: n_tiles - 1 - i) if reverse else (lambda i: i)
        h0 = jnp.concatenate([jnp.zeros((1, d), F32), state[:, dr], jnp.zeros((SUBLANES - 1 - lay.bs, d), F32)])
        h0_spec = lay.row_spec(tm, order)(d)
        tile = pl.BlockSpec((tm, d), lambda i: (order(i), 0))
        gate_specs = [_resident((RG_BLOCKS, bw, bw)), _resident((1, d)), _resident((RG_BLOCKS, bw, bw)),
                      _resident((1, d)), _resident((1, d))]
        gate_args = [w_a[dr].astype(BF16), b_a[dr:dr + 1], w_i[dr].astype(BF16), b_i[dr:dr + 1], lam[dr:dr + 1]]
        scan_scratch = [pltpu.VMEM((hb, SUBLANES, d), F32), pltpu.VMEM((hb, SUBLANES, d), F32)]
        last_spec = pl.BlockSpec((None, 1, d), lambda i: (order(i), 0, 0))
        last_shape = jax.ShapeDtypeStruct((n_tiles, 1, d), F32)
        if reverse:
            xc, hc, y_fwd = fwd_out
            in_specs = [tile, tile, tile, h0_spec] + gate_specs + [_resident((d, d))]
            args = [xc, hc, y_fwd, h0.reshape(SUBLANES, 1, d)] + gate_args + [w_y.astype(BF16)]
            out_shape = (jax.ShapeDtypeStruct((t, d), BF16), last_shape)
            out_specs = (tile, last_spec)
            scratch = list(scan_scratch)
        else:
            in_specs = [pl.BlockSpec((RG_HALO, d), lambda i: (jnp.maximum(i * halo_blocks - 1, 0), 0)),
                        tile,
                        pl.BlockSpec((RG_HALO, d), lambda i: (jnp.minimum((i + 1) * halo_blocks, n_halo - 1), 0)),
                        _resident((1, d)), lay.mod_spec(tm, 0, d), lay.mod_spec(tm, 1, d), h0_spec,
                        _resident((d, d)), _resident((RG_CONV_W, d)), _resident((1, d))] + gate_specs
            args = [x, x, x, norm_mix, mod, mod, h0.reshape(SUBLANES, 1, d), w_x.astype(BF16), conv_w, conv_b] + gate_args
            out_shape = (jax.ShapeDtypeStruct((t, d), F32), jax.ShapeDtypeStruct((t, d), F32),
                         jax.ShapeDtypeStruct((t, d), BF16), last_shape)
            out_specs = (tile, tile, tile, last_spec)
            scratch = [pltpu.VMEM((tm + 2 * RG_HALO, d), F32)] + scan_scratch
        scratch.append(pltpu.VMEM((1, d), F32))
        return pl.pallas_call(
            functools.partial(_rglru_kernel, lay=lay, n_tiles=n_tiles, reverse=reverse),
            out_shape=out_shape,
            grid=(n_tiles,),
            in_specs=in_specs,
            out_specs=out_specs,
            scratch_shapes=scratch,
            compiler_params=_params(1, "arbitrary"),
            name="rglru_bwd" if reverse else "rglru_fwd",
        )(*args)

    y_fwd, xc, hc, last_f = run(False)
    m, last_b = run(True, xc, hc, y_fwd)
    m = m.reshape(n_tiles, hb, SUBLANES, d).swapaxes(1, 2).reshape(t, d)
    ppb = lay.sp // tm
    tail = last_f[:lay.prompt_tiles(tm)].reshape(lay.bp, ppb, d)[:, ppb - 1]
    head = last_b[:lay.prompt_tiles(tm)].reshape(lay.bp, ppb, d)[:, 0]
    return m, jnp.stack([tail, head], axis=1)[:, None]


def _dft_tables(n, scale):
    jk = np.outer(np.arange(n), np.arange(n)) % n
    ang = 2.0 * np.pi * jk / n
    return np.cos(ang) * scale, np.sin(ang) * scale


def _fnet_channel_table(d):
    gw = d // FNET_GROUPS
    cos, sin = _dft_tables(gw, gw ** -0.5)
    return jnp.asarray(np.concatenate([cos, sin], axis=1), F32).astype(BF16)


def _fnet_position_kernel(wc_ref, ws_ref, zc_ref, zs_ref, o_ref):
    o_ref[...] = (_dot(wc_ref[...], zc_ref[...]) - _dot(ws_ref[...], zs_ref[...])).astype(BF16)


def _fnet_position(zc, zs, *, n_seq, seq_len, row0, tm):
    d = zc.shape[1]
    assert row0 % seq_len == 0 and seq_len % tm == 0
    cos, sin = _dft_tables(seq_len, seq_len ** -0.5)
    wc, ws = jnp.asarray(cos, F32).astype(BF16), jnp.asarray(sin, F32).astype(BF16)
    sb0, rt = row0 // seq_len, seq_len // tm
    w_spec = pl.BlockSpec((tm, seq_len), lambda b, j: (j, 0))
    z_spec = pl.BlockSpec((seq_len, d), lambda b, j: (sb0 + b, 0))
    return pl.pallas_call(
        _fnet_position_kernel,
        out_shape=jax.ShapeDtypeStruct((n_seq * seq_len, d), BF16),
        grid=(n_seq, rt),
        in_specs=[w_spec, w_spec, z_spec, z_spec],
        out_specs=pl.BlockSpec((tm, d), lambda b, j: (b * rt + j, 0)),
        compiler_params=_params(2),
        name="fnet_position",
    )(wc, ws, zc, zs)


def _fnet_layer(lay, zc, zs):
    return (_fnet_position(zc, zs, n_seq=lay.bp, seq_len=lay.sp, row0=0, tm=lay.sp),
            _fnet_position(zc, zs, n_seq=lay.bs, seq_len=lay.ss, row0=lay.tp, tm=min(FNET_ROW_TILE, lay.ss)))


def _conformer_kernel(xp_ref, xc_ref, xn_ref, nm_ref, sh_ref, sc_ref, w1_ref, b1_ref, dw_ref, db_ref, lg_ref, lb_ref,
                      o_ref, src_scr, acc_scr, *, lay):
    tm, d = xc_ref.shape
    r, n = _seq_position(lay, tm, pl.program_id(0))
    has_prev, has_next = r > 0, r < n - 1

    def pre(ref):
        return _norm_mod(ref[...], nm_ref[...], sh_ref[...], sc_ref[...]).astype(BF16)

    z = _dot(jnp.concatenate([pre(xp_ref), pre(xc_ref), pre(xn_ref)], axis=0), w1_ref[...]) + b1_ref[...]
    glu = z[:, :d] * jax.nn.sigmoid(z[:, d:])
    pieces = ((0, CONF_HALO, has_prev), (CONF_HALO, CONF_HALO + tm, None), (CONF_HALO + tm, tm + 2 * CONF_HALO, has_next))
    for lo, hi, valid in pieces:
        part = glu[lo:hi] if valid is None else jnp.where(valid, glu[lo:hi], 0.0)
        for c in range(d // LANES):
            src_scr[0, c, lo:hi, :] = part[:, c * LANES:(c + 1) * LANES]
    n_rows = tm + 2 * CONF_HALO - SUBLANES
    for s in range(1, SUBLANES):
        for c in range(d // LANES):
            src_scr[s, c, 0:n_rows, :] = src_scr[0, c, pl.ds(s, n_rows), :]
    pad_left = (CONF_CONV_W - 1) // 2
    taps = [divmod(CONF_HALO - pad_left + k, SUBLANES) for k in range(CONF_CONV_W)]
    for c in range(d // LANES):
        lanes = slice(c * LANES, (c + 1) * LANES)
        w = [jnp.broadcast_to(dw_ref[k:k + 1, lanes], (SUBLANES, LANES)) for k in range(CONF_CONV_W)]
        bias = jnp.broadcast_to(db_ref[:, lanes], (SUBLANES, LANES))

        def conv_rows(g, carry, c=c, lanes=lanes, w=w, bias=bias):
            row0 = pl.multiple_of(g * SUBLANES, SUBLANES)
            part = [bias] + [None] * (CONF_CHAINS - 1)
            for k, (q, s) in enumerate(taps):
                term = w[k] * src_scr[s, c, pl.ds(row0 + q * SUBLANES, SUBLANES), :]
                j = k % CONF_CHAINS
                part[j] = term if part[j] is None else part[j] + term
            while len(part) > 1:
                part = [a + b for a, b in zip(part[0::2], part[1::2])]
            acc_scr[pl.ds(row0, SUBLANES), lanes] = part[0]
            return carry

        lax.fori_loop(0, tm // SUBLANES, conv_rows, 0, unroll=2)
    acc = acc_scr[...]
    mu = jnp.mean(acc, axis=-1, keepdims=True)
    cen = acc - mu
    var = jnp.mean(cen * cen, axis=-1, keepdims=True)
    y = cen * lax.rsqrt(var + EPS) * lg_ref[...] + lb_ref[...]
    o_ref[...] = (y * jax.nn.sigmoid(y)).astype(BF16)


def _conformer_layer(lay, x, mod, norm_mix, w_pw1, b_pw1, dw_w, dw_b, ln_g, ln_b):
    t, d = x.shape
    tm = SEQ_TILE
    hb = tm // CONF_HALO
    n_halo = t // CONF_HALO
    tile = pl.BlockSpec((tm, d), lambda i: (i, 0))
    return pl.pallas_call(
        functools.partial(_conformer_kernel, lay=lay),
        out_shape=jax.ShapeDtypeStruct((t, d), BF16),
        grid=(t // tm,),
        in_specs=[pl.BlockSpec((CONF_HALO, d), lambda i: (jnp.maximum(i * hb - 1, 0), 0)),
                  tile,
                  pl.BlockSpec((CONF_HALO, d), lambda i: (jnp.minimum((i + 1) * hb, n_halo - 1), 0)),
                  _resident((1, d)), lay.mod_spec(tm, 0, d), lay.mod_spec(tm, 1, d),
                  _resident((d, 2 * d)), _resident((1, 2 * d)), _resident((CONF_CONV_W, d)), _resident((1, d)),
                  _resident((1, d)), _resident((1, d))],
        out_specs=tile,
        scratch_shapes=[pltpu.VMEM((SUBLANES, d // LANES, tm + 2 * CONF_HALO + SUBLANES, LANES), F32),
                        pltpu.VMEM((tm, d), F32)],
        compiler_params=_params(1),
        name="conformer",
    )(x, x, x, norm_mix, mod, mod, w_pw1.astype(BF16), b_pw1, dw_w, dw_b, ln_g, ln_b)


def kernel(x_prompt, x_sample, cache_mla_ckv, cache_mla_krope, state_rglru, c, c_ctx, ada_w, ada_b, norm_mix, norm_ffn, mla_w_dq, mla_q_norm, mla_w_uq, mla_w_dkv, mla_kv_norm, mla_w_ukv, mla_w_o, rg_w_x, rg_w_y, rg_conv_w, rg_conv_b, rg_w_a, rg_b_a, rg_w_i, rg_b_i, rg_lam, rg_w_o, fn_w_o, fn_b_o, cf_w_pw1, cf_b_pw1, cf_dw_w, cf_dw_b, cf_ln_g, cf_ln_b, cf_w_pw2, cf_b_pw2, ffn_w_gate, ffn_w_up, ffn_w_down, final_norm):
    bp, sp, d = x_prompt.shape
    bs, ss, _ = x_sample.shape
    depth = ada_w.shape[0]
    assert depth == 4 and bs < SUBLANES and sp == SEQ_TILE and ss % GRID_W == 0
    lay = _Layout(bp, sp, bs, ss)
    x_in = (x_prompt.reshape(lay.tp, d), x_sample.reshape(lay.ts, d))

    cvec = jnp.concatenate([c_ctx[None], c, jnp.zeros((SUBLANES - 1 - bs, d), F32)], axis=0)
    mods = _ada_tables(cvec, ada_w, ada_b).reshape(depth, SUBLANES, 1, -1)
    zero_bias = jnp.zeros((1, d), F32)
    w_gate, w_up, w_down = ffn_w_gate.astype(BF16), ffn_w_up.astype(BF16), ffn_w_down.astype(BF16)
    norm_ffn3 = norm_ffn[:, None, :]

    def ffn(layer, x, m, w_out, b_out, **tail):
        return _ffn_layer(lay, layer, x, m, w_out.astype(BF16), b_out, mods[layer], norm_ffn3, w_gate, w_up, w_down,
                          **tail)

    m, ckv_state, krope_state = _mla_layer(
        lay, *x_in, mods[0], norm_mix[0:1], cache_mla_ckv[:, 0], cache_mla_krope[:, 0], mla_w_dq[0], mla_q_norm[0:1],
        mla_w_uq[0], mla_w_dkv[0], mla_kv_norm[0:1], mla_w_ukv[0])
    x = ffn(0, x_in, m, mla_w_o[0], zero_bias)

    m, rg_state = _rglru_layer(lay, x, mods[1], norm_mix[1:2], state_rglru[:, 0], rg_w_x[0], rg_w_y[0], rg_conv_w[0],
                               rg_conv_b[0:1], rg_w_a[0], rg_b_a[0], rg_w_i[0], rg_b_i[0], rg_lam[0])
    x, zc, zs = ffn(1, x, m, rg_w_o[0], zero_bias, fnet_next=(norm_mix[2:3], mods[2]))

    m = _fnet_layer(lay, zc, zs)
    x = ffn(2, x, m, fn_w_o[0], fn_b_o[0:1])

    m = _conformer_layer(lay, x, mods[3], norm_mix[3:4], cf_w_pw1[0], cf_b_pw1[0:1], cf_dw_w[0], cf_dw_b[0:1],
                         cf_ln_g[0:1], cf_ln_b[0:1])
    y_prompt, y_sample = ffn(3, x, m, cf_w_pw2[0], cf_b_pw2[0:1], final_norm=final_norm[None])

    return (y_prompt.reshape(bp, sp, d), y_sample.reshape(bs, ss, d), ckv_state, krope_state, rg_state)
```

```python
import functools

import numpy as np
import jax
import jax.numpy as jnp
from jax import lax
from jax.experimental import pallas as pl
from jax.experimental.pallas import tpu as pltpu

F32 = jnp.float32
BF16 = jnp.bfloat16

EPS = 1e-6
GRID_W = 64
MLA_HEADS = 8
KV_LORA_RANK = 256
QK_NOPE_DIM = 128
QK_ROPE_DIM = 64
V_HEAD_DIM = 128
ROPE_THETA = 10000.0
RG_BLOCKS = 4
RG_CONV_W = 4
RG_C = 8.0
FNET_GROUPS = 4
CONF_CONV_W = 31

LANES = 128
SUBLANES = 8
HEAD_W = 2 * LANES
SEQ_TILE = 256
FFN_TILE = 512
ATTN_Q_TILE = 2048
ATTN_SUB_TILE = 256
FNET_ROW_TILE = 512
FFN_CHUNK = 256
CONF_HALO = 16
RG_HALO = 16
CONF_CHAINS = 4
VMEM_LIMIT = 52 * 1024 * 1024


def _dot(a, b):
    return jnp.dot(a, b, preferred_element_type=F32)


def _rms(x, g):
    return x * lax.rsqrt(jnp.mean(x * x, axis=-1, keepdims=True) + EPS) * g


def _norm_mod(x, g, shift, scale):
    return _rms(x, g) * (1.0 + scale) + shift


def _resident(shape):
    nd = len(shape)
    return pl.BlockSpec(shape, lambda *_: (0,) * nd, pipeline_mode=pl.Buffered(1))


def _resident_at(index, shape):
    nd = len(shape)
    return pl.BlockSpec((None,) + tuple(shape), lambda *_: (index,) + (0,) * nd, pipeline_mode=pl.Buffered(1))


def _params(n_axes, semantics="parallel"):
    return pltpu.CompilerParams(dimension_semantics=(semantics,) * n_axes, vmem_limit_bytes=VMEM_LIMIT)


class _Layout:
    def __init__(self, n_prompt_seq, prompt_len, n_sample_seq, sample_len):
        self.bp, self.sp, self.bs, self.ss = n_prompt_seq, prompt_len, n_sample_seq, sample_len
        self.tp = n_prompt_seq * prompt_len
        self.ts = n_sample_seq * sample_len
        self.t = self.tp + self.ts

    def prompt_tiles(self, tile):
        assert self.tp % tile == 0 and self.ss % tile == 0
        return self.tp // tile

    def sample_tiles_per_seq(self, tile):
        return self.ss // tile

    def mod_row(self, tile):
        npt, spb = self.prompt_tiles(tile), self.sample_tiles_per_seq(tile)
        return lambda i: jnp.where(i < npt, 0, 1 + (i - npt) // spb)

    def mod_spec(self, tile, chunk, d, order=lambda i: i):
        row = self.mod_row(tile)
        return pl.BlockSpec((None, 1, d), lambda i: (row(order(i)), 0, chunk))

    def row_spec(self, tile, order=lambda i: i):
        row = self.mod_row(tile)
        return lambda d: pl.BlockSpec((None, 1, d), lambda i: (row(order(i)), 0, 0))


def _ada_kernel(c_ref, w_ref, b_ref, o_ref):
    c = c_ref[...]
    s = (c * jax.nn.sigmoid(c)).astype(BF16)
    o_ref[...] = _dot(s, w_ref[...].astype(BF16)) + b_ref[...]


def _ada_tables(cvec, ada_w, ada_b):
    depth, d, n = ada_w.shape
    tn = n // 4
    return pl.pallas_call(
        _ada_kernel,
        out_shape=jax.ShapeDtypeStruct((depth, SUBLANES, n), F32),
        grid=(depth, n // tn),
        in_specs=[
            pl.BlockSpec((SUBLANES, d), lambda l, j: (0, 0)),
            pl.BlockSpec((None, d, tn), lambda l, j: (l, 0, j)),
            pl.BlockSpec((None, 1, tn), lambda l, j: (l, 0, j)),
        ],
        out_specs=pl.BlockSpec((None, SUBLANES, tn), lambda l, j: (l, 0, j)),
        compiler_params=_params(2),
        name="ada_tables",
    )(cvec, ada_w, ada_b.reshape(depth, 1, n))


def _read_tokens(refs, n_prompt_tiles):
    if len(refs) == 1:
        return refs[0][...]
    return jnp.where(pl.program_id(0) < n_prompt_tiles, refs[0][...], refs[1][...])


def _fnet_channel_dft(h, w_ref, zc_ref, zs_ref):
    gw = w_ref.shape[0]
    for g in range(FNET_GROUPS):
        sl = slice(g * gw, (g + 1) * gw)
        f = _dot(h[:, sl], w_ref[...])
        zc_ref[:, sl] = f[:, :gw].astype(BF16)
        zs_ref[:, sl] = f[:, gw:].astype(BF16)


def _ffn_kernel(*refs, n_chunks, n_x, n_m, tail, n_prompt_tiles):
    x_refs, m_refs, refs = refs[:n_x], refs[n_x:n_x + n_m], refs[n_x + n_m:]
    wo_ref, bo_ref, g1_ref, sh_ref, sc_ref, g2_ref, nf_ref, wg_ref, wu_ref, wd_ref = refs[:10]
    refs, a_ref = refs[10:-1], refs[-1]
    m = _read_tokens(m_refs, n_prompt_tiles)
    x1 = _read_tokens(x_refs, n_prompt_tiles) + g1_ref[...] * (_dot(m, wo_ref[...]) + bo_ref[...])
    h = _norm_mod(x1, nf_ref[...], sh_ref[...], sc_ref[...]).astype(BF16)
    for c in range(n_chunks):
        sl = slice(c * FFN_CHUNK, (c + 1) * FFN_CHUNK)
        g = _dot(h, wg_ref[:, sl])
        u = _dot(h, wu_ref[:, sl])
        a_ref[:, sl] = (g * jax.nn.sigmoid(g) * u).astype(BF16)
    x2 = x1 + g2_ref[...] * _dot(a_ref[...], wd_ref[...])
    if tail is None:
        refs[0][...] = x2
    elif tail == "fnet":
        nm_ref, sh1_ref, sc1_ref, dft_ref, o_ref, zc_ref, zs_ref = refs
        o_ref[...] = x2
        h_next = _norm_mod(x2, nm_ref[...], sh1_ref[...], sc1_ref[...]).astype(BF16)
        _fnet_channel_dft(h_next, dft_ref, zc_ref, zs_ref)
    else:
        fin_ref, op_ref, os_ref = refs
        y = _rms(x2, fin_ref[...])
        is_prompt = pl.program_id(0) < n_prompt_tiles

        @pl.when(is_prompt)
        def _():
            op_ref[...] = y

        @pl.when(jnp.logical_not(is_prompt))
        def _():
            os_ref[...] = y


def _ffn_layer(lay, layer, x, m, w_out, b_out, mod, norm_ffn, w_gate, w_up, w_down, final_norm=None, fnet_next=None):
    xs = x if isinstance(x, tuple) else (x,)
    ms = m if isinstance(m, tuple) else (m,)
    d = xs[0].shape[1]
    dff = w_gate.shape[2]
    assert dff % FFN_CHUNK == 0
    tm = FFN_TILE
    npt = lay.prompt_tiles(tm)
    tile = pl.BlockSpec((tm, d), lambda i: (i, 0))
    split = [pl.BlockSpec((tm, d), lambda i: (jnp.minimum(i, npt - 1), 0)),
             pl.BlockSpec((tm, d), lambda i: (jnp.maximum(i - npt, 0), 0))]
    in_specs = (split if len(xs) == 2 else [tile]) + (split if len(ms) == 2 else [tile]) + [
        _resident((d, d)), _resident((1, d)),
        lay.mod_spec(tm, 2, d), lay.mod_spec(tm, 3, d), lay.mod_spec(tm, 4, d), lay.mod_spec(tm, 5, d),
        _resident_at(layer, (1, d)), _resident_at(layer, (d, dff)), _resident_at(layer, (d, dff)),
        _resident_at(layer, (dff, d))]
    args = list(xs) + list(ms) + [w_out, b_out, mod, mod, mod, mod, norm_ffn, w_gate, w_up, w_down]
    out_shape, out_specs, tail = jax.ShapeDtypeStruct((lay.t, d), F32), tile, None
    if final_norm is not None:
        tail = "final"
        in_specs.append(_resident((1, d)))
        args.append(final_norm)
        out_shape = (jax.ShapeDtypeStruct((lay.tp, d), F32), jax.ShapeDtypeStruct((lay.ts, d), F32))
        out_specs = tuple(split)
    elif fnet_next is not None:
        tail = "fnet"
        norm_next, mod_next = fnet_next
        dft = _fnet_channel_table(d)
        in_specs += [_resident((1, d)), lay.mod_spec(tm, 0, d), lay.mod_spec(tm, 1, d), _resident(dft.shape)]
        args += [norm_next, mod_next, mod_next, dft]
        out_shape = (out_shape, jax.ShapeDtypeStruct((lay.t, d), BF16), jax.ShapeDtypeStruct((lay.t, d), BF16))
        out_specs = (tile, tile, tile)
    return pl.pallas_call(
        functools.partial(_ffn_kernel, n_chunks=dff // FFN_CHUNK, n_x=len(xs), n_m=len(ms), tail=tail,
                          n_prompt_tiles=npt),
        out_shape=out_shape,
        grid=(lay.t // tm,),
        in_specs=in_specs,
        out_specs=out_specs,
        scratch_shapes=[pltpu.VMEM((tm, dff), BF16)],
        compiler_params=_params(1, "arbitrary" if tail == "final" else "parallel"),
        name="ffn",
    )(*args)


def _mla_qkv_kernel(xp_ref, xs_ref, nm_ref, sh_ref, sc_ref, cos_ref, sin_ref, wdq_ref, qn_ref, wqn_ref, wqr_ref, wqx_ref,
                    wdkv_ref, wdkvx_ref, kvn_ref, wk_ref, wv_ref, q_ref, z_ref, k_ref, v_ref, *, scale, n_prompt_tiles):
    x = _read_tokens((xp_ref, xs_ref), n_prompt_tiles)
    h = _norm_mod(x, nm_ref[...], sh_ref[...], sc_ref[...]).astype(BF16)
    cos, sin = cos_ref[...], sin_ref[...]
    cq = _rms(_dot(h, wdq_ref[...]), qn_ref[...]).astype(BF16)
    q_nope = _dot(cq, wqn_ref[...])
    q_rope = _dot(cq, wqr_ref[...])
    q_swap = _dot(cq, wqx_ref[...])
    for hd in range(MLA_HEADS):
        sl = slice(hd * LANES, (hd + 1) * LANES)
        q_ref[:, hd * HEAD_W:hd * HEAD_W + LANES] = (q_nope[:, sl] * scale).astype(BF16)
        q_ref[:, hd * HEAD_W + LANES:(hd + 1) * HEAD_W] = ((q_rope[:, sl] * cos + q_swap[:, sl] * sin) * scale).astype(BF16)
    z = _dot(h, wdkv_ref[...])
    z_swap = _dot(h, wdkvx_ref[...])
    ckv = _rms(z[:, :KV_LORA_RANK], kvn_ref[...])
    k_rope = z[:, KV_LORA_RANK:] * cos + z_swap * sin
    z_ref[:, :KV_LORA_RANK] = ckv
    z_ref[:, KV_LORA_RANK:] = k_rope
    _expand_kv(ckv, k_rope, wk_ref, wv_ref, k_ref, v_ref)


def _expand_kv(ckv, k_rope, wk_ref, wv_ref, k_ref, v_ref):
    ckv = ckv.astype(BF16)
    k_rope = k_rope.astype(BF16)
    k_nope = _dot(ckv, wk_ref[...])
    for hd in range(MLA_HEADS):
        k_ref[:, hd * HEAD_W:hd * HEAD_W + LANES] = k_nope[:, hd * LANES:(hd + 1) * LANES].astype(BF16)
        k_ref[:, hd * HEAD_W + LANES:(hd + 1) * HEAD_W] = k_rope
    v_ref[...] = _dot(ckv, wv_ref[...]).astype(BF16)


def _kv_expand_kernel(z_ref, wk_ref, wv_ref, k_ref, v_ref):
    z = z_ref[...]
    _expand_kv(z[:, :KV_LORA_RANK], z[:, KV_LORA_RANK:], wk_ref, wv_ref, k_ref, v_ref)


def _kv_expand(z, wk, wv, n=None):
    zw = z.shape[1]
    n = z.shape[0] if n is None else n
    tm = 2 * SEQ_TILE if n % (2 * SEQ_TILE) == 0 else SEQ_TILE
    return pl.pallas_call(
        _kv_expand_kernel,
        out_shape=(jax.ShapeDtypeStruct((n, MLA_HEADS * HEAD_W), BF16),
                   jax.ShapeDtypeStruct((n, MLA_HEADS * V_HEAD_DIM), BF16)),
        grid=(n // tm,),
        in_specs=[pl.BlockSpec((tm, zw), lambda i: (i, 0)), _resident(wk.shape), _resident(wv.shape)],
        out_specs=(pl.BlockSpec((tm, MLA_HEADS * HEAD_W), lambda i: (i, 0)),
                   pl.BlockSpec((tm, MLA_HEADS * V_HEAD_DIM), lambda i: (i, 0))),
        compiler_params=_params(1),
        name="mla_kv_expand",
    )(z, wk, wv)


def _attn_kernel(*refs, heads, sub, n_kv):
    q_ref, kv_refs, o_ref = refs[0], refs[1:1 + 2 * n_kv], refs[-1]
    tq = q_ref.shape[0]
    for hd in range(heads):
        ks = [r[:, hd * HEAD_W:(hd + 1) * HEAD_W] for r in kv_refs[0::2]]
        vs = [r[:, hd * V_HEAD_DIM:(hd + 1) * V_HEAD_DIM] for r in kv_refs[1::2]]
        for j in range(tq // sub):
            rows = slice(j * sub, (j + 1) * sub)
            q = q_ref[rows, hd * HEAD_W:(hd + 1) * HEAD_W]
            ss = [lax.dot_general(q, k, (((1,), (1,)), ((), ())), preferred_element_type=F32) for k in ks]
            top = functools.reduce(jnp.maximum, [jnp.max(s, axis=-1, keepdims=True) for s in ss])
            ps = [jnp.exp(s - top) for s in ss]
            den = sum(jnp.sum(p, axis=-1, keepdims=True) for p in ps)
            o = sum(_dot(p.astype(BF16), v) for p, v in zip(ps, vs))
            o_ref[rows, hd * V_HEAD_DIM:(hd + 1) * V_HEAD_DIM] = (o / den).astype(BF16)


def _attention(q, kv, *, n_seq, q_len, q_row0, tq, heads):
    hg = MLA_HEADS // heads
    assert q_row0 % tq == 0 and q_len % tq == 0
    qb0, qt = q_row0 // tq, q_len // tq
    in_specs = [pl.BlockSpec((tq, heads * HEAD_W), lambda b, g, j: (qb0 + b * qt + j, g))]
    args = [q]
    for k, v, k_len, row0 in kv:
        assert row0 % k_len == 0
        kb0 = row0 // k_len
        in_specs += [pl.BlockSpec((k_len, heads * HEAD_W), lambda b, g, j, kb0=kb0: (kb0 + b, g)),
                     pl.BlockSpec((k_len, heads * V_HEAD_DIM), lambda b, g, j, kb0=kb0: (kb0 + b, g))]
        args += [k, v]
    return pl.pallas_call(
        functools.partial(_attn_kernel, heads=heads, sub=min(ATTN_SUB_TILE, tq), n_kv=len(kv)),
        out_shape=jax.ShapeDtypeStruct((n_seq * q_len, MLA_HEADS * V_HEAD_DIM), BF16),
        grid=(n_seq, hg, qt),
        in_specs=in_specs,
        out_specs=pl.BlockSpec((tq, heads * V_HEAD_DIM), lambda b, g, j: (b * qt + j, g)),
        compiler_params=_params(3),
        name="mla_attention",
    )(*args)


def _rope_tables(lay, tile):
    pos = np.arange(lay.ss)
    n_freq = QK_ROPE_DIM // 4
    inv_freq = ROPE_THETA ** (-np.arange(n_freq, dtype=np.float64) / n_freq)
    ang = np.concatenate([(pos // GRID_W)[:, None] * inv_freq, (pos % GRID_W)[:, None] * inv_freq], axis=-1)
    pad = np.zeros((lay.ss, LANES - QK_ROPE_DIM))
    cos = np.concatenate([np.cos(ang), np.cos(ang), pad], axis=-1)
    sin = np.concatenate([np.sin(ang), np.sin(ang), pad], axis=-1)
    ident_cos = np.concatenate([np.ones((tile, QK_ROPE_DIM)), np.zeros((tile, LANES - QK_ROPE_DIM))], axis=-1)
    cos = np.concatenate([ident_cos, cos], axis=0)
    sin = np.concatenate([np.zeros((tile, LANES)), sin], axis=0)
    return jnp.asarray(cos, F32), jnp.asarray(sin, F32)


def _swap_halves(w):
    half = QK_ROPE_DIM // 2
    return jnp.concatenate([-w[..., half:], w[..., :half]], axis=-1)


def _mla_layer(lay, x_prompt, x_sample, mod, norm_mix, cache_ckv, cache_krope, w_dq, q_norm, w_uq, w_dkv, kv_norm,
               w_ukv):
    t, d = lay.t, x_prompt.shape[1]
    tm = SEQ_TILE
    rq = w_dq.shape[1]
    qk = QK_NOPE_DIM + QK_ROPE_DIM
    zw = KV_LORA_RANK + LANES
    rope_pad = [(0, 0)] * 2 + [(0, LANES - QK_ROPE_DIM)]

    wq = w_uq.reshape(rq, MLA_HEADS, qk)
    wq_nope = wq[:, :, :QK_NOPE_DIM].reshape(rq, -1).astype(BF16)
    wq_rope = jnp.pad(wq[:, :, QK_NOPE_DIM:], rope_pad).reshape(rq, -1).astype(BF16)
    wq_swap = jnp.pad(_swap_halves(wq[:, :, QK_NOPE_DIM:]), rope_pad).reshape(rq, -1).astype(BF16)
    wdkv = jnp.pad(w_dkv, [(0, 0), (0, LANES - QK_ROPE_DIM)]).astype(BF16)
    wdkv_swap = jnp.pad(_swap_halves(w_dkv[:, KV_LORA_RANK:]), [(0, 0), (0, LANES - QK_ROPE_DIM)]).astype(BF16)
    wkv = w_ukv.reshape(KV_LORA_RANK, MLA_HEADS, QK_NOPE_DIM + V_HEAD_DIM)
    wk = wkv[:, :, :QK_NOPE_DIM].reshape(KV_LORA_RANK, -1).astype(BF16)
    wv = wkv[:, :, QK_NOPE_DIM:].reshape(KV_LORA_RANK, -1).astype(BF16)

    cos, sin = _rope_tables(lay, tm)
    npt, spb = lay.prompt_tiles(tm), lay.sample_tiles_per_seq(tm)
    rope_spec = pl.BlockSpec((tm, LANES), lambda i: (jnp.where(i < npt, 0, 1 + (i - npt) % spb), 0))
    def rows(width):
        return pl.BlockSpec((tm, width), lambda i: (i, 0))

    q, z, k_new, v_new = pl.pallas_call(
        functools.partial(_mla_qkv_kernel, scale=qk ** -0.5, n_prompt_tiles=npt),
        out_shape=(jax.ShapeDtypeStruct((t, MLA_HEADS * HEAD_W), BF16), jax.ShapeDtypeStruct((t, zw), F32),
                   jax.ShapeDtypeStruct((t, MLA_HEADS * HEAD_W), BF16),
                   jax.ShapeDtypeStruct((t, MLA_HEADS * V_HEAD_DIM), BF16)),
        grid=(t // tm,),
        in_specs=[pl.BlockSpec((tm, d), lambda i: (jnp.minimum(i, npt - 1), 0)),
                  pl.BlockSpec((tm, d), lambda i: (jnp.maximum(i - npt, 0), 0)), _resident((1, d)),
                  lay.mod_spec(tm, 0, d), lay.mod_spec(tm, 1, d), rope_spec, rope_spec,
                  _resident(w_dq.shape), _resident((1, rq)), _resident(wq_nope.shape), _resident(wq_rope.shape),
                  _resident(wq_swap.shape), _resident(wdkv.shape), _resident(wdkv_swap.shape),
                  _resident((1, KV_LORA_RANK)), _resident(wk.shape), _resident(wv.shape)],
        out_specs=(rows(MLA_HEADS * HEAD_W), rows(zw), rows(MLA_HEADS * HEAD_W), rows(MLA_HEADS * V_HEAD_DIM)),
        compiler_params=_params(1),
        name="mla_qkv",
    )(x_prompt, x_sample, norm_mix, mod, mod, cos, sin, w_dq.astype(BF16), q_norm, wq_nope, wq_rope, wq_swap, wdkv,
      wdkv_swap, kv_norm, wk, wv)

    past = cache_ckv.shape[1]
    z_cache = jnp.concatenate(
        [cache_ckv, cache_krope, jnp.zeros((lay.bs, past, LANES - QK_ROPE_DIM), F32)], axis=-1)
    k_past, v_past = _kv_expand(z_cache.reshape(lay.bs * past, zw), wk, wv)

    o_prompt = _attention(q, [(k_new, v_new, lay.sp, 0)], n_seq=lay.bp, q_len=lay.sp, q_row0=0, tq=lay.sp,
                          heads=MLA_HEADS)
    o_sample = _attention(q, [(k_past, v_past, past, 0), (k_new, v_new, lay.ss, lay.tp)], n_seq=lay.bs, q_len=lay.ss,
                          q_row0=lay.tp, tq=min(ATTN_Q_TILE, lay.ss), heads=1)
    ckv_state = z[:lay.tp, :KV_LORA_RANK].reshape(lay.bp, 1, lay.sp, KV_LORA_RANK)
    krope_state = z[:lay.tp, KV_LORA_RANK:KV_LORA_RANK + QK_ROPE_DIM].reshape(lay.bp, 1, lay.sp, QK_ROPE_DIM)
    return (o_prompt, o_sample), ckv_state, krope_state


def _seq_position(lay, tile, t):
    npt, spb = lay.prompt_tiles(tile), lay.sample_tiles_per_seq(tile)
    ppb = lay.sp // tile
    r = jnp.where(t < npt, t % ppb, (t - npt) % spb)
    n = jnp.where(t < npt, ppb, spb)
    return r, n


def _softplus(x):
    return jnp.maximum(x, 0.0) + jnp.log1p(jnp.exp(-jnp.abs(x)))


def _rglru_kernel(*refs, lay, n_tiles, reverse):
    if reverse:
        (xc_ref, hc_ref, yf_ref, h0_ref, wa_ref, ba_ref, wi_ref, bi_ref, lam_ref, wy_ref,
         o_ref, hl_ref, a_scr, u_scr, perm_scr, carry_scr) = refs
    else:
        (xp_ref, x_ref, xn_ref, nm_ref, sh_ref, sc_ref, h0_ref, wx_ref, cw_ref, cb_ref,
         wa_ref, ba_ref, wi_ref, bi_ref, lam_ref,
         o_ref, xc_ref, hc_ref, hl_ref, xw_scr, a_scr, u_scr, perm_scr, carry_scr) = refs
    tm, d = xc_ref.shape
    groups = tm // SUBLANES
    bw = d // RG_BLOCKS
    i = pl.program_id(0)
    t = n_tiles - 1 - i if reverse else i
    r, n = _seq_position(lay, tm, t)

    @pl.when(i == 0)
    def _():
        carry_scr[...] = jnp.zeros_like(carry_scr)

    if reverse:
        xc = xc_ref[...]
    else:
        has_prev, has_next = r > 0, r < n - 1

        def pre(x):
            return _norm_mod(x, nm_ref[...], sh_ref[...], sc_ref[...]).astype(BF16)

        x_nat = x_ref[...]
        for c in range(d // LANES):
            for s in range(SUBLANES):
                perm_scr[c, pl.ds(s, groups, stride=SUBLANES), :] = x_nat[s * groups:(s + 1) * groups,
                                                                         c * LANES:(c + 1) * LANES]
        hc = pre(jnp.concatenate([perm_scr[c] for c in range(d // LANES)], axis=-1))
        hc_ref[...] = hc
        xw = _dot(jnp.concatenate([pre(xp_ref[...]), hc, pre(xn_ref[...])], axis=0), wx_ref[...])
        sub = lax.broadcasted_iota(jnp.int32, (SUBLANES, d), 0)

        def tile_group(j):
            return xw[RG_HALO + j * SUBLANES:RG_HALO + (j + 1) * SUBLANES]

        def halo_row(k, valid):
            return jnp.broadcast_to(jnp.where(valid, xw[k:k + 1], 0.0), (SUBLANES, d))

        before = jnp.where(sub == 0, halo_row(RG_HALO - 1, has_prev), pltpu.roll(tile_group(groups - 1), 1, 0))
        after = [jnp.where(sub == SUBLANES - 1, halo_row(RG_HALO + tm + j, has_next),
                           pltpu.roll(tile_group(j), SUBLANES - 1, 0)) for j in range(RG_CONV_W - 2)]
        xw_scr[0:SUBLANES] = before
        xw_scr[SUBLANES:SUBLANES + tm] = xw[RG_HALO:RG_HALO + tm]
        for j, grp in enumerate(after):
            xw_scr[(groups + 1 + j) * SUBLANES:(groups + 2 + j) * SUBLANES] = grp
        xc = cb_ref[...]
        for k in range(RG_CONV_W):
            xc = xc + cw_ref[k:k + 1, :] * xw_scr[k * SUBLANES:k * SUBLANES + tm]
        xc_ref[...] = xc

    for nb in range(RG_BLOCKS):
        sl = slice(nb * bw, (nb + 1) * bw)
        xcn = xc[:, sl]
        xcb = xcn.astype(BF16)
        rg = jax.nn.sigmoid(_dot(xcb, wa_ref[nb]) + ba_ref[:, sl])
        ig = jax.nn.sigmoid(_dot(xcb, wi_ref[nb]) + bi_ref[:, sl])
        log_a = -RG_C * rg * _softplus(-lam_ref[:, sl])
        a = jnp.exp(log_a)
        u = jnp.sqrt(-jnp.tanh(log_a) * (a * a + 1.0)) * (ig * xcn)
        a_scr[:, :, sl] = a.reshape(groups, SUBLANES, bw)
        u_scr[:, :, sl] = u.reshape(groups, SUBLANES, bw)

    def local_step(g, carry):
        h, p = carry
        j = groups - 1 - g if reverse else g
        a = a_scr[j]
        h = a * h + u_scr[j]
        p = a * p
        u_scr[j] = h
        a_scr[j] = p
        return h, p

    h_seg, p_seg = lax.fori_loop(0, groups, local_step, (jnp.zeros((SUBLANES, d), F32), jnp.ones((SUBLANES, d), F32)))
    is_start = r == n - 1 if reverse else r == 0
    state = jnp.where(is_start, h0_ref[...], carry_scr[...])
    entering = [None] * SUBLANES
    for s in (range(SUBLANES - 1, -1, -1) if reverse else range(SUBLANES)):
        entering[s] = state
        state = h_seg[s:s + 1] + p_seg[s:s + 1] * state
    carry_scr[...] = state
    hl_ref[...] = state
    y = u_scr[...] + a_scr[...] * jnp.concatenate(entering, axis=0)[None]
    y = y.reshape(tm, d)
    if reverse:
        gate = jax.nn.gelu(_dot(hc_ref[...], wy_ref[...]))
        mixed = (yf_ref[...] + y) * gate
        for c in range(d // LANES):
            perm_scr[c] = mixed[:, c * LANES:(c + 1) * LANES]
        for s in range(SUBLANES):
            for c in range(d // LANES):
                o_ref[s * groups:(s + 1) * groups, c * LANES:(c + 1) * LANES] = (
                    perm_scr[c, pl.ds(s, groups, stride=SUBLANES), :].astype(BF16))
    else:
        o_ref[...] = y


def _rglru_layer(lay, x, mod, norm_mix, state, w_x, w_y, conv_w, conv_b, w_a, b_a, w_i, b_i, lam):
    t, d = x.shape
    tm = SEQ_TILE
    n_tiles = t // tm
    hb = tm // SUBLANES
    halo_blocks, n_halo = tm // RG_HALO, t // RG_HALO
    bw = d // RG_BLOCKS

    def run(reverse, *fwd_out):
        dr = int(reverse)
        order = (lambda i: n_tiles - 1 - i) if reverse else (lambda i: i)
        h0 = jnp.concatenate([jnp.zeros((1, d), F32), state[:, dr], jnp.zeros((SUBLANES - 1 - lay.bs, d), F32)])
        h0_spec = lay.row_spec(tm, order)(d)
        tile = pl.BlockSpec((tm, d), lambda i: (order(i), 0))
        gate_specs = [_resident((RG_BLOCKS, bw, bw)), _resident((1, d)), _resident((RG_BLOCKS, bw, bw)),
                      _resident((1, d)), _resident((1, d))]
        gate_args = [w_a[dr].astype(BF16), b_a[dr:dr + 1], w_i[dr].astype(BF16), b_i[dr:dr + 1], lam[dr:dr + 1]]
        scan_scratch = [pltpu.VMEM((hb, SUBLANES, d), F32), pltpu.VMEM((hb, SUBLANES, d), F32),
                        pltpu.VMEM((d // LANES, tm, LANES), F32)]
        last_spec = pl.BlockSpec((None, 1, d), lambda i: (order(i), 0, 0))
        last_shape = jax.ShapeDtypeStruct((n_tiles, 1, d), F32)
        if reverse:
            xc, hc, y_fwd = fwd_out
            in_specs = [tile, tile, tile, h0_spec] + gate_specs + [_resident((d, d))]
            args = [xc, hc, y_fwd, h0.reshape(SUBLANES, 1, d)] + gate_args + [w_y.astype(BF16)]
            out_shape = (jax.ShapeDtypeStruct((t, d), BF16), last_shape)
            out_specs = (tile, last_spec)
            scratch = list(scan_scratch)
        else:
            in_specs = [pl.BlockSpec((RG_HALO, d), lambda i: (jnp.maximum(i * halo_blocks - 1, 0), 0)),
                        tile,
                        pl.BlockSpec((RG_HALO, d), lambda i: (jnp.minimum((i + 1) * halo_blocks, n_halo - 1), 0)),
                        _resident((1, d)), lay.mod_spec(tm, 0, d), lay.mod_spec(tm, 1, d), h0_spec,
                        _resident((d, d)), _resident((RG_CONV_W, d)), _resident((1, d))] + gate_specs
            args = [x, x, x, norm_mix, mod, mod, h0.reshape(SUBLANES, 1, d), w_x.astype(BF16), conv_w, conv_b] + gate_args
            out_shape = (jax.ShapeDtypeStruct((t, d), F32), jax.ShapeDtypeStruct((t, d), F32),
                         jax.ShapeDtypeStruct((t, d), BF16), last_shape)
            out_specs = (tile, tile, tile, last_spec)
            scratch = [pltpu.VMEM((tm + 2 * RG_HALO, d), F32)] + scan_scratch
        scratch.append(pltpu.VMEM((1, d), F32))
        return pl.pallas_call(
            functools.partial(_rglru_kernel, lay=lay, n_tiles=n_tiles, reverse=reverse),
            out_shape=out_shape,
            grid=(n_tiles,),
            in_specs=in_specs,
            out_specs=out_specs,
            scratch_shapes=scratch,
            compiler_params=_params(1, "arbitrary"),
            name="rglru_bwd" if reverse else "rglru_fwd",
        )(*args)

    y_fwd, xc, hc, last_f = run(False)
    m, last_b = run(True, xc, hc, y_fwd)
    ppb = lay.sp // tm
    tail = last_f[:lay.prompt_tiles(tm)].reshape(lay.bp, ppb, d)[:, ppb - 1]
    head = last_b[:lay.prompt_tiles(tm)].reshape(lay.bp, ppb, d)[:, 0]
    return m, jnp.stack([tail, head], axis=1)[:, None]


def _dft_tables(n, scale):
    jk = np.outer(np.arange(n), np.arange(n)) % n
    ang = 2.0 * np.pi * jk / n
    return np.cos(ang) * scale, np.sin(ang) * scale


def _fnet_channel_table(d):
    gw = d // FNET_GROUPS
    cos, sin = _dft_tables(gw, gw ** -0.5)
    return jnp.asarray(np.concatenate([cos, sin], axis=1), F32).astype(BF16)


def _fnet_position_kernel(wc_ref, ws_ref, zc_ref, zs_ref, o_ref):
    o_ref[...] = (_dot(wc_ref[...], zc_ref[...]) - _dot(ws_ref[...], zs_ref[...])).astype(BF16)


def _fnet_position(zc, zs, *, n_seq, seq_len, row0, tm):
    d = zc.shape[1]
    assert row0 % seq_len == 0 and seq_len % tm == 0
    cos, sin = _dft_tables(seq_len, seq_len ** -0.5)
    wc, ws = jnp.asarray(cos, F32).astype(BF16), jnp.asarray(sin, F32).astype(BF16)
    sb0, rt = row0 // seq_len, seq_len // tm
    w_spec = pl.BlockSpec((tm, seq_len), lambda b, j: (j, 0))
    z_spec = pl.BlockSpec((seq_len, d), lambda b, j: (sb0 + b, 0))
    return pl.pallas_call(
        _fnet_position_kernel,
        out_shape=jax.ShapeDtypeStruct((n_seq * seq_len, d), BF16),
        grid=(n_seq, rt),
        in_specs=[w_spec, w_spec, z_spec, z_spec],
        out_specs=pl.BlockSpec((tm, d), lambda b, j: (b * rt + j, 0)),
        compiler_params=_params(2),
        name="fnet_position",
    )(wc, ws, zc, zs)


def _fnet_layer(lay, zc, zs):
    return (_fnet_position(zc, zs, n_seq=lay.bp, seq_len=lay.sp, row0=0, tm=lay.sp),
            _fnet_position(zc, zs, n_seq=lay.bs, seq_len=lay.ss, row0=lay.tp, tm=min(FNET_ROW_TILE, lay.ss)))


def _conformer_kernel(xp_ref, xc_ref, xn_ref, nm_ref, sh_ref, sc_ref, w1_ref, b1_ref, dw_ref, db_ref, lg_ref, lb_ref,
                      o_ref, src_scr, acc_scr, *, lay):
    tm, d = xc_ref.shape
    r, n = _seq_position(lay, tm, pl.program_id(0))
    has_prev, has_next = r > 0, r < n - 1

    def pre(ref):
        return _norm_mod(ref[...], nm_ref[...], sh_ref[...], sc_ref[...]).astype(BF16)

    z = _dot(jnp.concatenate([pre(xp_ref), pre(xc_ref), pre(xn_ref)], axis=0), w1_ref[...]) + b1_ref[...]
    glu = z[:, :d] * jax.nn.sigmoid(z[:, d:])
    pieces = ((0, CONF_HALO, has_prev), (CONF_HALO, CONF_HALO + tm, None), (CONF_HALO + tm, tm + 2 * CONF_HALO, has_next))
    for lo, hi, valid in pieces:
        part = glu[lo:hi] if valid is None else jnp.where(valid, glu[lo:hi], 0.0)
        for c in range(d // LANES):
            src_scr[0, c, lo:hi, :] = part[:, c * LANES:(c + 1) * LANES]
    n_rows = tm + 2 * CONF_HALO - SUBLANES
    for s in range(1, SUBLANES):
        for c in range(d // LANES):
            src_scr[s, c, 0:n_rows, :] = src_scr[0, c, pl.ds(s, n_rows), :]
    pad_left = (CONF_CONV_W - 1) // 2
    taps = [divmod(CONF_HALO - pad_left + k, SUBLANES) for k in range(CONF_CONV_W)]
    for c in range(d // LANES):
        lanes = slice(c * LANES, (c + 1) * LANES)
        w = [jnp.broadcast_to(dw_ref[k:k + 1, lanes], (SUBLANES, LANES)) for k in range(CONF_CONV_W)]
        bias = jnp.broadcast_to(db_ref[:, lanes], (SUBLANES, LANES))

        def conv_rows(g, carry, c=c, lanes=lanes, w=w, bias=bias):
            row0 = pl.multiple_of(g * SUBLANES, SUBLANES)
            part = [bias] + [None] * (CONF_CHAINS - 1)
            for k, (q, s) in enumerate(taps):
                term = w[k] * src_scr[s, c, pl.ds(row0 + q * SUBLANES, SUBLANES), :]
                j = k % CONF_CHAINS
                part[j] = term if part[j] is None else part[j] + term
            while len(part) > 1:
                part = [a + b for a, b in zip(part[0::2], part[1::2])]
            acc_scr[pl.ds(row0, SUBLANES), lanes] = part[0]
            return carry

        lax.fori_loop(0, tm // SUBLANES, conv_rows, 0, unroll=8)
    acc = acc_scr[...]
    mu = jnp.mean(acc, axis=-1, keepdims=True)
    cen = acc - mu
    var = jnp.mean(cen * cen, axis=-1, keepdims=True)
    y = cen * lax.rsqrt(var + EPS) * lg_ref[...] + lb_ref[...]
    o_ref[...] = (y * jax.nn.sigmoid(y)).astype(BF16)


def _conformer_layer(lay, x, mod, norm_mix, w_pw1, b_pw1, dw_w, dw_b, ln_g, ln_b):
    t, d = x.shape
    tm = SEQ_TILE
    hb = tm // CONF_HALO
    n_halo = t // CONF_HALO
    tile = pl.BlockSpec((tm, d), lambda i: (i, 0))
    return pl.pallas_call(
        functools.partial(_conformer_kernel, lay=lay),
        out_shape=jax.ShapeDtypeStruct((t, d), BF16),
        grid=(t // tm,),
        in_specs=[pl.BlockSpec((CONF_HALO, d), lambda i: (jnp.maximum(i * hb - 1, 0), 0)),
                  tile,
                  pl.BlockSpec((CONF_HALO, d), lambda i: (jnp.minimum((i + 1) * hb, n_halo - 1), 0)),
                  _resident((1, d)), lay.mod_spec(tm, 0, d), lay.mod_spec(tm, 1, d),
                  _resident((d, 2 * d)), _resident((1, 2 * d)), _resident((CONF_CONV_W, d)), _resident((1, d)),
                  _resident((1, d)), _resident((1, d))],
        out_specs=tile,
        scratch_shapes=[pltpu.VMEM((SUBLANES, d // LANES, tm + 2 * CONF_HALO + SUBLANES, LANES), F32),
                        pltpu.VMEM((tm, d), F32)],
        compiler_params=_params(1),
        name="conformer",
    )(x, x, x, norm_mix, mod, mod, w_pw1.astype(BF16), b_pw1, dw_w, dw_b, ln_g, ln_b)


def kernel(x_prompt, x_sample, cache_mla_ckv, cache_mla_krope, state_rglru, c, c_ctx, ada_w, ada_b, norm_mix, norm_ffn, mla_w_dq, mla_q_norm, mla_w_uq, mla_w_dkv, mla_kv_norm, mla_w_ukv, mla_w_o, rg_w_x, rg_w_y, rg_conv_w, rg_conv_b, rg_w_a, rg_b_a, rg_w_i, rg_b_i, rg_lam, rg_w_o, fn_w_o, fn_b_o, cf_w_pw1, cf_b_pw1, cf_dw_w, cf_dw_b, cf_ln_g, cf_ln_b, cf_w_pw2, cf_b_pw2, ffn_w_gate, ffn_w_up, ffn_w_down, final_norm):
    bp, sp, d = x_prompt.shape
    bs, ss, _ = x_sample.shape
    depth = ada_w.shape[0]
    assert depth == 4 and bs < SUBLANES and sp == SEQ_TILE and ss % GRID_W == 0
    lay = _Layout(bp, sp, bs, ss)
    x_in = (x_prompt.reshape(lay.tp, d), x_sample.reshape(lay.ts, d))

    cvec = jnp.concatenate([c_ctx[None], c, jnp.zeros((SUBLANES - 1 - bs, d), F32)], axis=0)
    mods = _ada_tables(cvec, ada_w, ada_b).reshape(depth, SUBLANES, 1, -1)
    zero_bias = jnp.zeros((1, d), F32)
    w_gate, w_up, w_down = ffn_w_gate.astype(BF16), ffn_w_up.astype(BF16), ffn_w_down.astype(BF16)
    norm_ffn3 = norm_ffn[:, None, :]

    def ffn(layer, x, m, w_out, b_out, **tail):
        return _ffn_layer(lay, layer, x, m, w_out.astype(BF16), b_out, mods[layer], norm_ffn3, w_gate, w_up, w_down,
                          **tail)

    m, ckv_state, krope_state = _mla_layer(
        lay, *x_in, mods[0], norm_mix[0:1], cache_mla_ckv[:, 0], cache_mla_krope[:, 0], mla_w_dq[0], mla_q_norm[0:1],
        mla_w_uq[0], mla_w_dkv[0], mla_kv_norm[0:1], mla_w_ukv[0])
    x = ffn(0, x_in, m, mla_w_o[0], zero_bias)

    m, rg_state = _rglru_layer(lay, x, mods[1], norm_mix[1:2], state_rglru[:, 0], rg_w_x[0], rg_w_y[0], rg_conv_w[0],
                               rg_conv_b[0:1], rg_w_a[0], rg_b_a[0], rg_w_i[0], rg_b_i[0], rg_lam[0])
    x, zc, zs = ffn(1, x, m, rg_w_o[0], zero_bias, fnet_next=(norm_mix[2:3], mods[2]))

    m = _fnet_layer(lay, zc, zs)
    x = ffn(2, x, m, fn_w_o[0], fn_b_o[0:1])

    m = _conformer_layer(lay, x, mods[3], norm_mix[3:4], cf_w_pw1[0], cf_b_pw1[0:1], cf_dw_w[0], cf_dw_b[0:1],
                         cf_ln_g[0:1], cf_ln_b[0:1])
    y_prompt, y_sample = ffn(3, x, m, cf_w_pw2[0], cf_b_pw2[0:1], final_norm=final_norm[None])

    return (y_prompt.reshape(bp, sp, d), y_sample.reshape(bs, ss, d), ckv_state, krope_state, rg_state)
```

```python
import functools

import numpy as np
import jax
import jax.numpy as jnp
from jax import lax
from jax.experimental import pallas as pl
from jax.experimental.pallas import tpu as pltpu

F32 = jnp.float32
BF16 = jnp.bfloat16

EPS = 1e-6
GRID_W = 64
MLA_HEADS = 8
KV_LORA_RANK = 256
QK_NOPE_DIM = 128
QK_ROPE_DIM = 64
V_HEAD_DIM = 128
ROPE_THETA = 10000.0
RG_BLOCKS = 4
RG_CONV_W = 4
RG_C = 8.0
FNET_GROUPS = 4
CONF_CONV_W = 31

LANES = 128
SUBLANES = 8
HEAD_W = 2 * LANES
SEQ_TILE = 256
FFN_TILE = 512
MLA_ROW_TILE = 512
ATTN_Q_TILE = 2048
ATTN_SUB_TILE = 256
FNET_ROW_TILE = 512
FNET_PROMPT_GROUP = 4
FFN_CHUNK = 256
CONF_HALO = 16
RG_HALO = 16
CONF_CHAINS = 4
CONF_GLU_CHUNK = 256
VMEM_LIMIT = 52 * 1024 * 1024


def _dot(a, b):
    return jnp.dot(a, b, preferred_element_type=F32)


def _rms(x, g):
    return x * lax.rsqrt(jnp.mean(x * x, axis=-1, keepdims=True) + EPS) * g


def _norm_mod(x, g, shift, scale):
    return _rms(x, g) * (1.0 + scale) + shift


def _resident(shape):
    nd = len(shape)
    return pl.BlockSpec(shape, lambda *_: (0,) * nd, pipeline_mode=pl.Buffered(1))


def _resident_at(index, shape):
    nd = len(shape)
    return pl.BlockSpec((None,) + tuple(shape), lambda *_: (index,) + (0,) * nd, pipeline_mode=pl.Buffered(1))


def _params(n_axes, semantics="parallel"):
    return pltpu.CompilerParams(dimension_semantics=(semantics,) * n_axes, vmem_limit_bytes=VMEM_LIMIT)


class _Layout:
    def __init__(self, n_prompt_seq, prompt_len, n_sample_seq, sample_len):
        self.bp, self.sp, self.bs, self.ss = n_prompt_seq, prompt_len, n_sample_seq, sample_len
        self.tp = n_prompt_seq * prompt_len
        self.ts = n_sample_seq * sample_len
        self.t = self.tp + self.ts

    def prompt_tiles(self, tile):
        assert self.tp % tile == 0 and self.ss % tile == 0
        return self.tp // tile

    def sample_tiles_per_seq(self, tile):
        return self.ss // tile

    def mod_row(self, tile):
        npt, spb = self.prompt_tiles(tile), self.sample_tiles_per_seq(tile)
        return lambda i: jnp.where(i < npt, 0, 1 + (i - npt) // spb)

    def mod_spec(self, tile, chunk, d, order=lambda i: i):
        row = self.mod_row(tile)
        return pl.BlockSpec((None, 1, d), lambda i: (row(order(i)), 0, chunk))

    def row_spec(self, tile, order=lambda i: i):
        row = self.mod_row(tile)
        return lambda d: pl.BlockSpec((None, 1, d), lambda i: (row(order(i)), 0, 0))


def _ada_kernel(c_ref, w_ref, b_ref, o_ref):
    c = c_ref[...]
    s = (c * jax.nn.sigmoid(c)).astype(BF16)
    o_ref[...] = _dot(s, w_ref[...].astype(BF16)) + b_ref[...]


def _ada_tables(cvec, ada_w, ada_b):
    depth, d, n = ada_w.shape
    tn = n // 4
    return pl.pallas_call(
        _ada_kernel,
        out_shape=jax.ShapeDtypeStruct((depth, SUBLANES, n), F32),
        grid=(depth, n // tn),
        in_specs=[
            pl.BlockSpec((SUBLANES, d), lambda l, j: (0, 0)),
            pl.BlockSpec((None, d, tn), lambda l, j: (l, 0, j)),
            pl.BlockSpec((None, 1, tn), lambda l, j: (l, 0, j)),
        ],
        out_specs=pl.BlockSpec((None, SUBLANES, tn), lambda l, j: (l, 0, j)),
        compiler_params=_params(2),
        name="ada_tables",
    )(cvec, ada_w, ada_b.reshape(depth, 1, n))


def _read_tokens(refs, n_prompt_tiles):
    if len(refs) == 1:
        return refs[0][...]
    return jnp.where(pl.program_id(0) < n_prompt_tiles, refs[0][...], refs[1][...])


def _fnet_channel_dft(h, w_ref, zc_ref, zs_ref):
    gw = w_ref.shape[0]
    for g in range(FNET_GROUPS):
        sl = slice(g * gw, (g + 1) * gw)
        f = _dot(h[:, sl], w_ref[...])
        zc_ref[:, sl] = f[:, :gw].astype(BF16)
        zs_ref[:, sl] = f[:, gw:].astype(BF16)


def _ffn_kernel(*refs, n_chunks, n_x, n_m, tail, n_prompt_tiles):
    x_refs, m_refs, refs = refs[:n_x], refs[n_x:n_x + n_m], refs[n_x + n_m:]
    wo_ref, bo_ref, g1_ref, sh_ref, sc_ref, g2_ref, nf_ref, wg_ref, wu_ref, wd_ref = refs[:10]
    refs, a_ref = refs[10:-1], refs[-1]
    m = _read_tokens(m_refs, n_prompt_tiles)
    x1 = _read_tokens(x_refs, n_prompt_tiles) + g1_ref[...] * (_dot(m, wo_ref[...]) + bo_ref[...])
    h = _norm_mod(x1, nf_ref[...], sh_ref[...], sc_ref[...]).astype(BF16)
    for c in range(n_chunks):
        sl = slice(c * FFN_CHUNK, (c + 1) * FFN_CHUNK)
        g = _dot(h, wg_ref[:, sl])
        u = _dot(h, wu_ref[:, sl])
        a_ref[:, sl] = (g * jax.nn.sigmoid(g) * u).astype(BF16)
    x2 = x1 + g2_ref[...] * _dot(a_ref[...], wd_ref[...])
    if tail is None:
        refs[0][...] = x2
    elif tail == "fnet":
        nm_ref, sh1_ref, sc1_ref, dft_ref, o_ref, zc_ref, zs_ref = refs
        o_ref[...] = x2
        h_next = _norm_mod(x2, nm_ref[...], sh1_ref[...], sc1_ref[...]).astype(BF16)
        _fnet_channel_dft(h_next, dft_ref, zc_ref, zs_ref)
    else:
        fin_ref, op_ref, os_ref = refs
        y = _rms(x2, fin_ref[...])
        is_prompt = pl.program_id(0) < n_prompt_tiles

        @pl.when(is_prompt)
        def _():
            op_ref[...] = y

        @pl.when(jnp.logical_not(is_prompt))
        def _():
            os_ref[...] = y


def _ffn_layer(lay, layer, x, m, w_out, b_out, mod, norm_ffn, w_gate, w_up, w_down, final_norm=None, fnet_next=None):
    xs = x if isinstance(x, tuple) else (x,)
    ms = m if isinstance(m, tuple) else (m,)
    d = xs[0].shape[1]
    dff = w_gate.shape[2]
    assert dff % FFN_CHUNK == 0
    tm = FFN_TILE
    npt = lay.prompt_tiles(tm)
    tile = pl.BlockSpec((tm, d), lambda i: (i, 0))
    split = [pl.BlockSpec((tm, d), lambda i: (jnp.minimum(i, npt - 1), 0)),
             pl.BlockSpec((tm, d), lambda i: (jnp.maximum(i - npt, 0), 0))]
    in_specs = (split if len(xs) == 2 else [tile]) + (split if len(ms) == 2 else [tile]) + [
        _resident((d, d)), _resident((1, d)),
        lay.mod_spec(tm, 2, d), lay.mod_spec(tm, 3, d), lay.mod_spec(tm, 4, d), lay.mod_spec(tm, 5, d),
        _resident_at(layer, (1, d)), _resident_at(layer, (d, dff)), _resident_at(layer, (d, dff)),
        _resident_at(layer, (dff, d))]
    args = list(xs) + list(ms) + [w_out, b_out, mod, mod, mod, mod, norm_ffn, w_gate, w_up, w_down]
    out_shape, out_specs, tail = jax.ShapeDtypeStruct((lay.t, d), F32), tile, None
    if final_norm is not None:
        tail = "final"
        in_specs.append(_resident((1, d)))
        args.append(final_norm)
        out_shape = (jax.ShapeDtypeStruct((lay.tp, d), F32), jax.ShapeDtypeStruct((lay.ts, d), F32))
        out_specs = tuple(split)
    elif fnet_next is not None:
        tail = "fnet"
        norm_next, mod_next = fnet_next
        dft = _fnet_channel_table(d)
        in_specs += [_resident((1, d)), lay.mod_spec(tm, 0, d), lay.mod_spec(tm, 1, d), _resident(dft.shape)]
        args += [norm_next, mod_next, mod_next, dft]
        out_shape = (out_shape, jax.ShapeDtypeStruct((lay.t, d), BF16), jax.ShapeDtypeStruct((lay.t, d), BF16))
        out_specs = (tile, tile, tile)
    return pl.pallas_call(
        functools.partial(_ffn_kernel, n_chunks=dff // FFN_CHUNK, n_x=len(xs), n_m=len(ms), tail=tail,
                          n_prompt_tiles=npt),
        out_shape=out_shape,
        grid=(lay.t // tm,),
        in_specs=in_specs,
        out_specs=out_specs,
        scratch_shapes=[pltpu.VMEM((tm, dff), BF16)],
        compiler_params=_params(1, "arbitrary" if tail == "final" else "parallel"),
        name="ffn",
    )(*args)


def _mla_qkv_kernel(xp_ref, xs_ref, nm_ref, sh_ref, sc_ref, cos_ref, sin_ref, wdq_ref, qn_ref, wqn_ref, wqr_ref, wqx_ref,
                    wdkv_ref, wdkvx_ref, kvn_ref, wk_ref, wv_ref, q_ref, z_ref, k_ref, v_ref, *, scale, n_prompt_tiles):
    x_all = _read_tokens((xp_ref, xs_ref), n_prompt_tiles)
    for j in range(x_all.shape[0] // SEQ_TILE):
        rows = slice(j * SEQ_TILE, (j + 1) * SEQ_TILE)
        h = _norm_mod(x_all[rows], nm_ref[...], sh_ref[...], sc_ref[...]).astype(BF16)
        cos, sin = cos_ref[rows, :], sin_ref[rows, :]
        cq = _rms(_dot(h, wdq_ref[...]), qn_ref[...]).astype(BF16)
        q_nope = _dot(cq, wqn_ref[...])
        q_rope = _dot(cq, wqr_ref[...])
        q_swap = _dot(cq, wqx_ref[...])
        for hd in range(MLA_HEADS):
            sl = slice(hd * LANES, (hd + 1) * LANES)
            q_ref[rows, hd * HEAD_W:hd * HEAD_W + LANES] = (q_nope[:, sl] * scale).astype(BF16)
            q_ref[rows, hd * HEAD_W + LANES:(hd + 1) * HEAD_W] = (
                (q_rope[:, sl] * cos + q_swap[:, sl] * sin) * scale).astype(BF16)
        z = _dot(h, wdkv_ref[...])
        z_swap = _dot(h, wdkvx_ref[...])
        ckv = _rms(z[:, :KV_LORA_RANK], kvn_ref[...])
        k_rope = z[:, KV_LORA_RANK:] * cos + z_swap * sin
        z_ref[rows, :KV_LORA_RANK] = ckv
        z_ref[rows, KV_LORA_RANK:] = k_rope
        _expand_kv(ckv, k_rope, wk_ref, wv_ref, k_ref, v_ref, rows)


def _expand_kv(ckv, k_rope, wk_ref, wv_ref, k_ref, v_ref, rows=slice(None)):
    ckv = ckv.astype(BF16)
    k_rope = k_rope.astype(BF16)
    k_nope = _dot(ckv, wk_ref[...])
    for hd in range(MLA_HEADS):
        k_ref[rows, hd * HEAD_W:hd * HEAD_W + LANES] = k_nope[:, hd * LANES:(hd + 1) * LANES].astype(BF16)
        k_ref[rows, hd * HEAD_W + LANES:(hd + 1) * HEAD_W] = k_rope
    v_ref[rows, :] = _dot(ckv, wv_ref[...]).astype(BF16)


def _kv_expand_kernel(z_ref, wk_ref, wv_ref, k_ref, v_ref):
    z = z_ref[...]
    _expand_kv(z[:, :KV_LORA_RANK], z[:, KV_LORA_RANK:], wk_ref, wv_ref, k_ref, v_ref)


def _kv_expand(z, wk, wv, n=None):
    zw = z.shape[1]
    n = z.shape[0] if n is None else n
    tm = 2 * SEQ_TILE if n % (2 * SEQ_TILE) == 0 else SEQ_TILE
    return pl.pallas_call(
        _kv_expand_kernel,
        out_shape=(jax.ShapeDtypeStruct((n, MLA_HEADS * HEAD_W), BF16),
                   jax.ShapeDtypeStruct((n, MLA_HEADS * V_HEAD_DIM), BF16)),
        grid=(n // tm,),
        in_specs=[pl.BlockSpec((tm, zw), lambda i: (i, 0)), _resident(wk.shape), _resident(wv.shape)],
        out_specs=(pl.BlockSpec((tm, MLA_HEADS * HEAD_W), lambda i: (i, 0)),
                   pl.BlockSpec((tm, MLA_HEADS * V_HEAD_DIM), lambda i: (i, 0))),
        compiler_params=_params(1),
        name="mla_kv_expand",
    )(z, wk, wv)


def _attn_kernel(*refs, heads, sub, n_kv):
    q_ref, kv_refs, o_ref = refs[0], refs[1:1 + 2 * n_kv], refs[-1]
    tq = q_ref.shape[0]
    for hd in range(heads):
        ks = [r[:, hd * HEAD_W:(hd + 1) * HEAD_W] for r in kv_refs[0::2]]
        vs = [r[:, hd * V_HEAD_DIM:(hd + 1) * V_HEAD_DIM] for r in kv_refs[1::2]]
        for j in range(tq // sub):
            rows = slice(j * sub, (j + 1) * sub)
            q = q_ref[rows, hd * HEAD_W:(hd + 1) * HEAD_W]
            ss = [lax.dot_general(q, k, (((1,), (1,)), ((), ())), preferred_element_type=F32) for k in ks]
            top = functools.reduce(jnp.maximum, [jnp.max(s, axis=-1, keepdims=True) for s in ss])
            ps = [jnp.exp(s - top) for s in ss]
            den = sum(jnp.sum(p, axis=-1, keepdims=True) for p in ps)
            o = sum(_dot(p.astype(BF16), v) for p, v in zip(ps, vs))
            o_ref[rows, hd * V_HEAD_DIM:(hd + 1) * V_HEAD_DIM] = (o / den).astype(BF16)


def _attention(q, kv, *, n_seq, q_len, q_row0, tq, heads):
    hg = MLA_HEADS // heads
    assert q_row0 % tq == 0 and q_len % tq == 0
    qb0, qt = q_row0 // tq, q_len // tq
    in_specs = [pl.BlockSpec((tq, heads * HEAD_W), lambda b, g, j: (qb0 + b * qt + j, g))]
    args = [q]
    for k, v, k_len, row0 in kv:
        assert row0 % k_len == 0
        kb0 = row0 // k_len
        in_specs += [pl.BlockSpec((k_len, heads * HEAD_W), lambda b, g, j, kb0=kb0: (kb0 + b, g)),
                     pl.BlockSpec((k_len, heads * V_HEAD_DIM), lambda b, g, j, kb0=kb0: (kb0 + b, g))]
        args += [k, v]
    return pl.pallas_call(
        functools.partial(_attn_kernel, heads=heads, sub=min(ATTN_SUB_TILE, tq), n_kv=len(kv)),
        out_shape=jax.ShapeDtypeStruct((n_seq * q_len, MLA_HEADS * V_HEAD_DIM), BF16),
        grid=(n_seq, hg, qt),
        in_specs=in_specs,
        out_specs=pl.BlockSpec((tq, heads * V_HEAD_DIM), lambda b, g, j: (b * qt + j, g)),
        compiler_params=_params(3),
        name="mla_attention",
    )(*args)


def _rope_tables(lay, tile):
    pos = np.arange(lay.ss)
    n_freq = QK_ROPE_DIM // 4
    inv_freq = ROPE_THETA ** (-np.arange(n_freq, dtype=np.float64) / n_freq)
    ang = np.concatenate([(pos // GRID_W)[:, None] * inv_freq, (pos % GRID_W)[:, None] * inv_freq], axis=-1)
    pad = np.zeros((lay.ss, LANES - QK_ROPE_DIM))
    cos = np.concatenate([np.cos(ang), np.cos(ang), pad], axis=-1)
    sin = np.concatenate([np.sin(ang), np.sin(ang), pad], axis=-1)
    ident_cos = np.concatenate([np.ones((tile, QK_ROPE_DIM)), np.zeros((tile, LANES - QK_ROPE_DIM))], axis=-1)
    cos = np.concatenate([ident_cos, cos], axis=0)
    sin = np.concatenate([np.zeros((tile, LANES)), sin], axis=0)
    return jnp.asarray(cos, F32), jnp.asarray(sin, F32)


def _swap_halves(w):
    half = QK_ROPE_DIM // 2
    return jnp.concatenate([-w[..., half:], w[..., :half]], axis=-1)


def _mla_layer(lay, x_prompt, x_sample, mod, norm_mix, cache_ckv, cache_krope, w_dq, q_norm, w_uq, w_dkv, kv_norm,
               w_ukv):
    t, d = lay.t, x_prompt.shape[1]
    tm = MLA_ROW_TILE
    rq = w_dq.shape[1]
    qk = QK_NOPE_DIM + QK_ROPE_DIM
    zw = KV_LORA_RANK + LANES
    rope_pad = [(0, 0)] * 2 + [(0, LANES - QK_ROPE_DIM)]

    wq = w_uq.reshape(rq, MLA_HEADS, qk)
    wq_nope = wq[:, :, :QK_NOPE_DIM].reshape(rq, -1).astype(BF16)
    wq_rope = jnp.pad(wq[:, :, QK_NOPE_DIM:], rope_pad).reshape(rq, -1).astype(BF16)
    wq_swap = jnp.pad(_swap_halves(wq[:, :, QK_NOPE_DIM:]), rope_pad).reshape(rq, -1).astype(BF16)
    wdkv = jnp.pad(w_dkv, [(0, 0), (0, LANES - QK_ROPE_DIM)]).astype(BF16)
    wdkv_swap = jnp.pad(_swap_halves(w_dkv[:, KV_LORA_RANK:]), [(0, 0), (0, LANES - QK_ROPE_DIM)]).astype(BF16)
    wkv = w_ukv.reshape(KV_LORA_RANK, MLA_HEADS, QK_NOPE_DIM + V_HEAD_DIM)
    wk = wkv[:, :, :QK_NOPE_DIM].reshape(KV_LORA_RANK, -1).astype(BF16)
    wv = wkv[:, :, QK_NOPE_DIM:].reshape(KV_LORA_RANK, -1).astype(BF16)

    cos, sin = _rope_tables(lay, tm)
    npt, spb = lay.prompt_tiles(tm), lay.sample_tiles_per_seq(tm)
    rope_spec = pl.BlockSpec((tm, LANES), lambda i: (jnp.where(i < npt, 0, 1 + (i - npt) % spb), 0))
    def rows(width):
        return pl.BlockSpec((tm, width), lambda i: (i, 0))

    q, z, k_new, v_new = pl.pallas_call(
        functools.partial(_mla_qkv_kernel, scale=qk ** -0.5, n_prompt_tiles=npt),
        out_shape=(jax.ShapeDtypeStruct((t, MLA_HEADS * HEAD_W), BF16), jax.ShapeDtypeStruct((t, zw), F32),
                   jax.ShapeDtypeStruct((t, MLA_HEADS * HEAD_W), BF16),
                   jax.ShapeDtypeStruct((t, MLA_HEADS * V_HEAD_DIM), BF16)),
        grid=(t // tm,),
        in_specs=[pl.BlockSpec((tm, d), lambda i: (jnp.minimum(i, npt - 1), 0)),
                  pl.BlockSpec((tm, d), lambda i: (jnp.maximum(i - npt, 0), 0)), _resident((1, d)),
                  lay.mod_spec(tm, 0, d), lay.mod_spec(tm, 1, d), rope_spec, rope_spec,
                  _resident(w_dq.shape), _resident((1, rq)), _resident(wq_nope.shape), _resident(wq_rope.shape),
                  _resident(wq_swap.shape), _resident(wdkv.shape), _resident(wdkv_swap.shape),
                  _resident((1, KV_LORA_RANK)), _resident(wk.shape), _resident(wv.shape)],
        out_specs=(rows(MLA_HEADS * HEAD_W), rows(zw), rows(MLA_HEADS * HEAD_W), rows(MLA_HEADS * V_HEAD_DIM)),
        compiler_params=_params(1),
        name="mla_qkv",
    )(x_prompt, x_sample, norm_mix, mod, mod, cos, sin, w_dq.astype(BF16), q_norm, wq_nope, wq_rope, wq_swap, wdkv,
      wdkv_swap, kv_norm, wk, wv)

    past = cache_ckv.shape[1]
    z_cache = jnp.concatenate(
        [cache_ckv, cache_krope, jnp.zeros((lay.bs, past, LANES - QK_ROPE_DIM), F32)], axis=-1)
    k_past, v_past = _kv_expand(z_cache.reshape(lay.bs * past, zw), wk, wv)

    o_prompt = _attention(q, [(k_new, v_new, lay.sp, 0)], n_seq=lay.bp, q_len=lay.sp, q_row0=0, tq=lay.sp,
                          heads=MLA_HEADS)
    o_sample = _attention(q, [(k_past, v_past, past, 0), (k_new, v_new, lay.ss, lay.tp)], n_seq=lay.bs, q_len=lay.ss,
                          q_row0=lay.tp, tq=min(ATTN_Q_TILE, lay.ss), heads=1)
    ckv_state = z[:lay.tp, :KV_LORA_RANK].reshape(lay.bp, 1, lay.sp, KV_LORA_RANK)
    krope_state = z[:lay.tp, KV_LORA_RANK:KV_LORA_RANK + QK_ROPE_DIM].reshape(lay.bp, 1, lay.sp, QK_ROPE_DIM)
    return (o_prompt, o_sample), ckv_state, krope_state


def _seq_position(lay, tile, t):
    npt, spb = lay.prompt_tiles(tile), lay.sample_tiles_per_seq(tile)
    ppb = lay.sp // tile
    r = jnp.where(t < npt, t % ppb, (t - npt) % spb)
    n = jnp.where(t < npt, ppb, spb)
    return r, n


def _softplus(x):
    return jnp.maximum(x, 0.0) + jnp.log1p(jnp.exp(-jnp.abs(x)))


def _rglru_kernel(*refs, lay, n_tiles, reverse):
    if reverse:
        (xc_ref, hc_ref, yf_ref, h0_ref, wa_ref, ba_ref, wi_ref, bi_ref, lam_ref, wy_ref,
         o_ref, hl_ref, a_scr, u_scr, perm_scr, carry_scr) = refs
    else:
        (xp_ref, x_ref, xn_ref, nm_ref, sh_ref, sc_ref, h0_ref, wx_ref, cw_ref, cb_ref,
         wa_ref, ba_ref, wi_ref, bi_ref, lam_ref,
         o_ref, xc_ref, hc_ref, hl_ref, xw_scr, a_scr, u_scr, perm_scr, carry_scr) = refs
    tm, d = xc_ref.shape
    groups = tm // SUBLANES
    bw = d // RG_BLOCKS
    i = pl.program_id(0)
    t = n_tiles - 1 - i if reverse else i
    r, n = _seq_position(lay, tm, t)

    @pl.when(i == 0)
    def _():
        carry_scr[...] = jnp.zeros_like(carry_scr)

    if reverse:
        xc = xc_ref[...]
    else:
        has_prev, has_next = r > 0, r < n - 1

        def pre(x):
            return _norm_mod(x, nm_ref[...], sh_ref[...], sc_ref[...]).astype(BF16)

        x_nat = x_ref[...]
        for c in range(d // LANES):
            for s in range(SUBLANES):
                perm_scr[c, pl.ds(s, groups, stride=SUBLANES), :] = x_nat[s * groups:(s + 1) * groups,
                                                                         c * LANES:(c + 1) * LANES]
        hc = pre(jnp.concatenate([perm_scr[c] for c in range(d // LANES)], axis=-1))
        hc_ref[...] = hc
        xw = _dot(jnp.concatenate([pre(xp_ref[...]), hc, pre(xn_ref[...])], axis=0), wx_ref[...])
        sub = lax.broadcasted_iota(jnp.int32, (SUBLANES, d), 0)

        def tile_group(j):
            return xw[RG_HALO + j * SUBLANES:RG_HALO + (j + 1) * SUBLANES]

        def halo_row(k, valid):
            return jnp.broadcast_to(jnp.where(valid, xw[k:k + 1], 0.0), (SUBLANES, d))

        before = jnp.where(sub == 0, halo_row(RG_HALO - 1, has_prev), pltpu.roll(tile_group(groups - 1), 1, 0))
        after = [jnp.where(sub == SUBLANES - 1, halo_row(RG_HALO + tm + j, has_next),
                           pltpu.roll(tile_group(j), SUBLANES - 1, 0)) for j in range(RG_CONV_W - 2)]
        xw_scr[0:SUBLANES] = before
        xw_scr[SUBLANES:SUBLANES + tm] = xw[RG_HALO:RG_HALO + tm]
        for j, grp in enumerate(after):
            xw_scr[(groups + 1 + j) * SUBLANES:(groups + 2 + j) * SUBLANES] = grp
        xc = cb_ref[...]
        for k in range(RG_CONV_W):
            xc = xc + cw_ref[k:k + 1, :] * xw_scr[k * SUBLANES:k * SUBLANES + tm]
        xc_ref[...] = xc

    for nb in range(RG_BLOCKS):
        sl = slice(nb * bw, (nb + 1) * bw)
        xcn = xc[:, sl]
        xcb = xcn.astype(BF16)
        rg = jax.nn.sigmoid(_dot(xcb, wa_ref[nb]) + ba_ref[:, sl])
        ig = jax.nn.sigmoid(_dot(xcb, wi_ref[nb]) + bi_ref[:, sl])
        log_a = -RG_C * rg * _softplus(-lam_ref[:, sl])
        a = jnp.exp(log_a)
        u = jnp.sqrt(-jnp.tanh(log_a) * (a * a + 1.0)) * (ig * xcn)
        a_scr[:, :, sl] = a.reshape(groups, SUBLANES, bw)
        u_scr[:, :, sl] = u.reshape(groups, SUBLANES, bw)

    def local_step(g, carry):
        h, p = carry
        j = groups - 1 - g if reverse else g
        a = a_scr[j]
        h = a * h + u_scr[j]
        p = a * p
        u_scr[j] = h
        a_scr[j] = p
        return h, p

    h_seg, p_seg = lax.fori_loop(0, groups, local_step, (jnp.zeros((SUBLANES, d), F32), jnp.ones((SUBLANES, d), F32)))
    is_start = r == n - 1 if reverse else r == 0
    state = jnp.where(is_start, h0_ref[...], carry_scr[...])
    entering = [None] * SUBLANES
    for s in (range(SUBLANES - 1, -1, -1) if reverse else range(SUBLANES)):
        entering[s] = state
        state = h_seg[s:s + 1] + p_seg[s:s + 1] * state
    carry_scr[...] = state
    hl_ref[...] = state
    y = u_scr[...] + a_scr[...] * jnp.concatenate(entering, axis=0)[None]
    y = y.reshape(tm, d)
    if reverse:
        gate = jax.nn.gelu(_dot(hc_ref[...], wy_ref[...]))
        mixed = (yf_ref[...] + y) * gate
        for c in range(d // LANES):
            perm_scr[c] = mixed[:, c * LANES:(c + 1) * LANES]
        for s in range(SUBLANES):
            for c in range(d // LANES):
                o_ref[s * groups:(s + 1) * groups, c * LANES:(c + 1) * LANES] = (
                    perm_scr[c, pl.ds(s, groups, stride=SUBLANES), :].astype(BF16))
    else:
        o_ref[...] = y


def _rglru_layer(lay, x, mod, norm_mix, state, w_x, w_y, conv_w, conv_b, w_a, b_a, w_i, b_i, lam):
    t, d = x.shape
    tm = SEQ_TILE
    n_tiles = t // tm
    hb = tm // SUBLANES
    halo_blocks, n_halo = tm // RG_HALO, t // RG_HALO
    bw = d // RG_BLOCKS

    def run(reverse, *fwd_out):
        dr = int(reverse)
        order = (lambda i: n_tiles - 1 - i) if reverse else (lambda i: i)
        h0 = jnp.concatenate([jnp.zeros((1, d), F32), state[:, dr], jnp.zeros((SUBLANES - 1 - lay.bs, d), F32)])
        h0_spec = lay.row_spec(tm, order)(d)
        tile = pl.BlockSpec((tm, d), lambda i: (order(i), 0))
        gate_specs = [_resident((RG_BLOCKS, bw, bw)), _resident((1, d)), _resident((RG_BLOCKS, bw, bw)),
                      _resident((1, d)), _resident((1, d))]
        gate_args = [w_a[dr].astype(BF16), b_a[dr:dr + 1], w_i[dr].astype(BF16), b_i[dr:dr + 1], lam[dr:dr + 1]]
        scan_scratch = [pltpu.VMEM((hb, SUBLANES, d), F32), pltpu.VMEM((hb, SUBLANES, d), F32),
                        pltpu.VMEM((d // LANES, tm, LANES), F32)]
        last_spec = pl.BlockSpec((None, 1, d), lambda i: (order(i), 0, 0))
        last_shape = jax.ShapeDtypeStruct((n_tiles, 1, d), F32)
        if reverse:
            xc, hc, y_fwd = fwd_out
            in_specs = [tile, tile, tile, h0_spec] + gate_specs + [_resident((d, d))]
            args = [xc, hc, y_fwd, h0.reshape(SUBLANES, 1, d)] + gate_args + [w_y.astype(BF16)]
            out_shape = (jax.ShapeDtypeStruct((t, d), BF16), last_shape)
            out_specs = (tile, last_spec)
            scratch = list(scan_scratch)
        else:
            in_specs = [pl.BlockSpec((RG_HALO, d), lambda i: (jnp.maximum(i * halo_blocks - 1, 0), 0)),
                        tile,
                        pl.BlockSpec((RG_HALO, d), lambda i: (jnp.minimum((i + 1) * halo_blocks, n_halo - 1), 0)),
                        _resident((1, d)), lay.mod_spec(tm, 0, d), lay.mod_spec(tm, 1, d), h0_spec,
                        _resident((d, d)), _resident((RG_CONV_W, d)), _resident((1, d))] + gate_specs
            args = [x, x, x, norm_mix, mod, mod, h0.reshape(SUBLANES, 1, d), w_x.astype(BF16), conv_w, conv_b] + gate_args
            out_shape = (jax.ShapeDtypeStruct((t, d), F32), jax.ShapeDtypeStruct((t, d), F32),
                         jax.ShapeDtypeStruct((t, d), BF16), last_shape)
            out_specs = (tile, tile, tile, last_spec)
            scratch = [pltpu.VMEM((tm + 2 * RG_HALO, d), F32)] + scan_scratch
        scratch.append(pltpu.VMEM((1, d), F32))
        return pl.pallas_call(
            functools.partial(_rglru_kernel, lay=lay, n_tiles=n_tiles, reverse=reverse),
            out_shape=out_shape,
            grid=(n_tiles,),
            in_specs=in_specs,
            out_specs=out_specs,
            scratch_shapes=scratch,
            compiler_params=_params(1, "arbitrary"),
            name="rglru_bwd" if reverse else "rglru_fwd",
        )(*args)

    y_fwd, xc, hc, last_f = run(False)
    m, last_b = run(True, xc, hc, y_fwd)
    ppb = lay.sp // tm
    tail = last_f[:lay.prompt_tiles(tm)].reshape(lay.bp, ppb, d)[:, ppb - 1]
    head = last_b[:lay.prompt_tiles(tm)].reshape(lay.bp, ppb, d)[:, 0]
    return m, jnp.stack([tail, head], axis=1)[:, None]


def _dft_tables(n, scale):
    jk = np.outer(np.arange(n), np.arange(n)) % n
    ang = 2.0 * np.pi * jk / n
    return np.cos(ang) * scale, np.sin(ang) * scale


def _fnet_channel_table(d):
    gw = d // FNET_GROUPS
    cos, sin = _dft_tables(gw, gw ** -0.5)
    return jnp.asarray(np.concatenate([cos, sin], axis=1), F32).astype(BF16)


def _fnet_position_kernel(wc_ref, ws_ref, zc_ref, zs_ref, o_ref, *, seq_len):
    for i in range(zc_ref.shape[0] // seq_len):
        rows = slice(i * seq_len, (i + 1) * seq_len)
        o_ref[i * wc_ref.shape[0]:(i + 1) * wc_ref.shape[0], :] = (
            _dot(wc_ref[...], zc_ref[rows, :]) - _dot(ws_ref[...], zs_ref[rows, :])).astype(BF16)


def _fnet_position(zc, zs, *, n_seq, seq_len, row0, tm, seqs_per_step=1):
    d = zc.shape[1]
    g = seqs_per_step
    assert row0 % (g * seq_len) == 0 and seq_len % tm == 0 and n_seq % g == 0 and (g == 1 or tm == seq_len)
    cos, sin = _dft_tables(seq_len, seq_len ** -0.5)
    wc, ws = jnp.asarray(cos, F32).astype(BF16), jnp.asarray(sin, F32).astype(BF16)
    sb0, rt = row0 // (g * seq_len), seq_len // tm
    w_spec = pl.BlockSpec((tm, seq_len), lambda b, j: (j, 0))
    z_spec = pl.BlockSpec((g * seq_len, d), lambda b, j: (sb0 + b, 0))
    return pl.pallas_call(
        functools.partial(_fnet_position_kernel, seq_len=seq_len),
        out_shape=jax.ShapeDtypeStruct((n_seq * seq_len, d), BF16),
        grid=(n_seq // g, rt),
        in_specs=[w_spec, w_spec, z_spec, z_spec],
        out_specs=pl.BlockSpec((g * tm, d), lambda b, j: (b * rt + j, 0)),
        compiler_params=_params(2),
        name="fnet_position",
    )(wc, ws, zc, zs)


def _fnet_layer(lay, zc, zs):
    group = FNET_PROMPT_GROUP if lay.bp % FNET_PROMPT_GROUP == 0 else 1
    return (_fnet_position(zc, zs, n_seq=lay.bp, seq_len=lay.sp, row0=0, tm=lay.sp, seqs_per_step=group),
            _fnet_position(zc, zs, n_seq=lay.bs, seq_len=lay.ss, row0=lay.tp, tm=min(FNET_ROW_TILE, lay.ss)))


def _conformer_kernel(xp_ref, xc_ref, xn_ref, nm_ref, sh_ref, sc_ref, w1_ref, b1_ref, dw_ref, db_ref, lg_ref, lb_ref,
                      o_ref, src_scr, acc_scr, *, lay):
    tm, d = xc_ref.shape
    r, n = _seq_position(lay, tm, pl.program_id(0))
    has_prev, has_next = r > 0, r < n - 1

    def pre(ref):
        return _norm_mod(ref[...], nm_ref[...], sh_ref[...], sc_ref[...]).astype(BF16)

    h = jnp.concatenate([pre(xp_ref), pre(xc_ref), pre(xn_ref)], axis=0)
    groups = tm // SUBLANES
    n_before = (CONF_CONV_W - 1) // 2
    n_after = CONF_CONV_W - 1 - n_before
    sub = lax.broadcasted_iota(jnp.int32, (SUBLANES, LANES), 0)

    def group_rows(g):
        return slice(g * SUBLANES, (g + 1) * SUBLANES)

    for j in range(d // CONF_GLU_CHUNK):
        val = slice(j * CONF_GLU_CHUNK, (j + 1) * CONF_GLU_CHUNK)
        gate = slice(d + j * CONF_GLU_CHUNK, d + (j + 1) * CONF_GLU_CHUNK)
        glu = (_dot(h, w1_ref[:, val]) + b1_ref[:, val]) * jax.nn.sigmoid(_dot(h, w1_ref[:, gate]) + b1_ref[:, gate])
        for i in range(CONF_GLU_CHUNK // LANES):
            c = j * CONF_GLU_CHUNK // LANES + i
            col = glu[:, i * LANES:(i + 1) * LANES]
            for s in range(SUBLANES):
                src_scr[c, pl.ds(n_before * SUBLANES + s, groups, stride=SUBLANES), :] = (
                    col[CONF_HALO + s * groups:CONF_HALO + (s + 1) * groups])
            for k in range(1, n_before + 1):
                edge = jnp.where(has_prev, col[CONF_HALO - k:CONF_HALO - k + 1], 0.0)
                inner = pltpu.roll(src_scr[c, group_rows(n_before + groups - k), :], 1, 0)
                src_scr[c, group_rows(n_before - k), :] = jnp.where(sub == 0, edge, inner)
            for k in range(n_after):
                edge = jnp.where(has_next, col[CONF_HALO + tm + k:CONF_HALO + tm + k + 1], 0.0)
                inner = pltpu.roll(src_scr[c, group_rows(n_before + k), :], SUBLANES - 1, 0)
                src_scr[c, group_rows(n_before + groups + k), :] = jnp.where(sub == SUBLANES - 1, edge, inner)
    for c in range(d // LANES):
        lanes = slice(c * LANES, (c + 1) * LANES)
        w = [jnp.broadcast_to(dw_ref[k:k + 1, lanes], (SUBLANES, LANES)) for k in range(CONF_CONV_W)]
        bias = jnp.broadcast_to(db_ref[:, lanes], (SUBLANES, LANES))

        def conv_rows(g, carry, c=c, w=w, bias=bias):
            row0 = pl.multiple_of(g * SUBLANES, SUBLANES)
            part = [bias] + [None] * (CONF_CHAINS - 1)
            for k in range(CONF_CONV_W):
                term = w[k] * src_scr[c, pl.ds(row0 + k * SUBLANES, SUBLANES), :]
                j = k % CONF_CHAINS
                part[j] = term if part[j] is None else part[j] + term
            while len(part) > 1:
                part = [a + b for a, b in zip(part[0::2], part[1::2])]
            acc_scr[c, pl.ds(row0, SUBLANES), :] = part[0]
            return carry

        lax.fori_loop(0, groups, conv_rows, 0, unroll=8)
    acc = jnp.concatenate([acc_scr[c] for c in range(d // LANES)], axis=-1)
    mu = jnp.mean(acc, axis=-1, keepdims=True)
    cen = acc - mu
    var = jnp.mean(cen * cen, axis=-1, keepdims=True)
    y = cen * lax.rsqrt(var + EPS) * lg_ref[...] + lb_ref[...]
    y = y * jax.nn.sigmoid(y)
    for c in range(d // LANES):
        acc_scr[c] = y[:, c * LANES:(c + 1) * LANES]
    for s in range(SUBLANES):
        for c in range(d // LANES):
            o_ref[s * groups:(s + 1) * groups, c * LANES:(c + 1) * LANES] = (
                acc_scr[c, pl.ds(s, groups, stride=SUBLANES), :].astype(BF16))


def _conformer_layer(lay, x, mod, norm_mix, w_pw1, b_pw1, dw_w, dw_b, ln_g, ln_b):
    t, d = x.shape
    tm = SEQ_TILE
    hb = tm // CONF_HALO
    n_halo = t // CONF_HALO
    tile = pl.BlockSpec((tm, d), lambda i: (i, 0))
    return pl.pallas_call(
        functools.partial(_conformer_kernel, lay=lay),
        out_shape=jax.ShapeDtypeStruct((t, d), BF16),
        grid=(t // tm,),
        in_specs=[pl.BlockSpec((CONF_HALO, d), lambda i: (jnp.maximum(i * hb - 1, 0), 0)),
                  tile,
                  pl.BlockSpec((CONF_HALO, d), lambda i: (jnp.minimum((i + 1) * hb, n_halo - 1), 0)),
                  _resident((1, d)), lay.mod_spec(tm, 0, d), lay.mod_spec(tm, 1, d),
                  _resident((d, 2 * d)), _resident((1, 2 * d)), _resident((CONF_CONV_W, d)), _resident((1, d)),
                  _resident((1, d)), _resident((1, d))],
        out_specs=tile,
        scratch_shapes=[pltpu.VMEM((d // LANES, tm + (CONF_CONV_W - 1) * SUBLANES, LANES), F32),
                        pltpu.VMEM((d // LANES, tm, LANES), F32)],
        compiler_params=_params(1),
        name="conformer",
    )(x, x, x, norm_mix, mod, mod, w_pw1.astype(BF16), b_pw1, dw_w, dw_b, ln_g, ln_b)


def kernel(x_prompt, x_sample, cache_mla_ckv, cache_mla_krope, state_rglru, c, c_ctx, ada_w, ada_b, norm_mix, norm_ffn, mla_w_dq, mla_q_norm, mla_w_uq, mla_w_dkv, mla_kv_norm, mla_w_ukv, mla_w_o, rg_w_x, rg_w_y, rg_conv_w, rg_conv_b, rg_w_a, rg_b_a, rg_w_i, rg_b_i, rg_lam, rg_w_o, fn_w_o, fn_b_o, cf_w_pw1, cf_b_pw1, cf_dw_w, cf_dw_b, cf_ln_g, cf_ln_b, cf_w_pw2, cf_b_pw2, ffn_w_gate, ffn_w_up, ffn_w_down, final_norm):
    bp, sp, d = x_prompt.shape
    bs, ss, _ = x_sample.shape
    depth = ada_w.shape[0]
    assert depth == 4 and bs < SUBLANES and sp == SEQ_TILE and ss % GRID_W == 0
    lay = _Layout(bp, sp, bs, ss)
    x_in = (x_prompt.reshape(lay.tp, d), x_sample.reshape(lay.ts, d))

    cvec = jnp.concatenate([c_ctx[None], c, jnp.zeros((SUBLANES - 1 - bs, d), F32)], axis=0)
    mods = _ada_tables(cvec, ada_w, ada_b).reshape(depth, SUBLANES, 1, -1)
    zero_bias = jnp.zeros((1, d), F32)
    w_gate, w_up, w_down = ffn_w_gate.astype(BF16), ffn_w_up.astype(BF16), ffn_w_down.astype(BF16)
    norm_ffn3 = norm_ffn[:, None, :]

    def ffn(layer, x, m, w_out, b_out, **tail):
        return _ffn_layer(lay, layer, x, m, w_out.astype(BF16), b_out, mods[layer], norm_ffn3, w_gate, w_up, w_down,
                          **tail)

    m, ckv_state, krope_state = _mla_layer(
        lay, *x_in, mods[0], norm_mix[0:1], cache_mla_ckv[:, 0], cache_mla_krope[:, 0], mla_w_dq[0], mla_q_norm[0:1],
        mla_w_uq[0], mla_w_dkv[0], mla_kv_norm[0:1], mla_w_ukv[0])
    x = ffn(0, x_in, m, mla_w_o[0], zero_bias)

    m, rg_state = _rglru_layer(lay, x, mods[1], norm_mix[1:2], state_rglru[:, 0], rg_w_x[0], rg_w_y[0], rg_conv_w[0],
                               rg_conv_b[0:1], rg_w_a[0], rg_b_a[0], rg_w_i[0], rg_b_i[0], rg_lam[0])
    x, zc, zs = ffn(1, x, m, rg_w_o[0], zero_bias, fnet_next=(norm_mix[2:3], mods[2]))

    m = _fnet_layer(lay, zc, zs)
    x = ffn(2, x, m, fn_w_o[0], fn_b_o[0:1])

    m = _conformer_layer(lay, x, mods[3], norm_mix[3:4], cf_w_pw1[0], cf_b_pw1[0:1], cf_dw_w[0], cf_dw_b[0:1],
                         cf_ln_g[0:1], cf_ln_b[0:1])
    y_prompt, y_sample = ffn(3, x, m, cf_w_pw2[0], cf_b_pw2[0:1], final_norm=final_norm[None])

    return (y_prompt.reshape(bp, sp, d), y_sample.reshape(bs, ss, d), ckv_state, krope_state, rg_state)
```

```python
import functools

import numpy as np
import jax
import jax.numpy as jnp
from jax import lax
from jax.experimental import pallas as pl
from jax.experimental.pallas import tpu as pltpu

F32 = jnp.float32
BF16 = jnp.bfloat16

EPS = 1e-6
GRID_W = 64
MLA_HEADS = 8
KV_LORA_RANK = 256
QK_NOPE_DIM = 128
QK_ROPE_DIM = 64
V_HEAD_DIM = 128
ROPE_THETA = 10000.0
RG_BLOCKS = 4
RG_CONV_W = 4
RG_C = 8.0
FNET_GROUPS = 4
CONF_CONV_W = 31

LANES = 128
SUBLANES = 8
BF16_SUBLANES = 16
HEAD_W = 2 * LANES
SEQ_TILE = 256
FFN_TILE = 512
MLA_ROW_TILE = 512
ATTN_Q_TILE = 2048
ATTN_SUB_TILE = 256
FNET_ROW_TILE = 512
FNET_PROMPT_GROUP = 4
FFN_CHUNK = 256
CONF_HALO = 16
RG_HALO = 16
CONF_CHAINS = 4
CONF_GLU_CHUNK = 256
VMEM_LIMIT = 52 * 1024 * 1024


def _dot(a, b):
    return jnp.dot(a, b, preferred_element_type=F32)


def _rms(x, g):
    return x * lax.rsqrt(jnp.mean(x * x, axis=-1, keepdims=True) + EPS) * g


def _norm_mod(x, g, shift, scale):
    return _rms(x, g) * (1.0 + scale) + shift


def _resident(shape):
    nd = len(shape)
    return pl.BlockSpec(shape, lambda *_: (0,) * nd, pipeline_mode=pl.Buffered(1))


def _resident_at(index, shape):
    nd = len(shape)
    return pl.BlockSpec((None,) + tuple(shape), lambda *_: (index,) + (0,) * nd, pipeline_mode=pl.Buffered(1))


def _params(n_axes, semantics="parallel"):
    return pltpu.CompilerParams(dimension_semantics=(semantics,) * n_axes, vmem_limit_bytes=VMEM_LIMIT)


class _Layout:
    def __init__(self, n_prompt_seq, prompt_len, n_sample_seq, sample_len):
        self.bp, self.sp, self.bs, self.ss = n_prompt_seq, prompt_len, n_sample_seq, sample_len
        self.tp = n_prompt_seq * prompt_len
        self.ts = n_sample_seq * sample_len
        self.t = self.tp + self.ts

    def prompt_tiles(self, tile):
        assert self.tp % tile == 0 and self.ss % tile == 0
        return self.tp // tile

    def sample_tiles_per_seq(self, tile):
        return self.ss // tile

    def mod_row(self, tile):
        npt, spb = self.prompt_tiles(tile), self.sample_tiles_per_seq(tile)
        return lambda i: jnp.where(i < npt, 0, 1 + (i - npt) // spb)

    def mod_spec(self, tile, chunk, d, order=lambda i: i):
        row = self.mod_row(tile)
        return pl.BlockSpec((None, 1, d), lambda i: (row(order(i)), 0, chunk))

    def row_spec(self, tile, order=lambda i: i):
        row = self.mod_row(tile)
        return lambda d: pl.BlockSpec((None, 1, d), lambda i: (row(order(i)), 0, 0))


def _ada_kernel(c_ref, w_ref, b_ref, o_ref):
    c = c_ref[...]
    s = (c * jax.nn.sigmoid(c)).astype(BF16)
    o_ref[...] = _dot(s, w_ref[...].astype(BF16)) + b_ref[...]


def _ada_tables(cvec, ada_w, ada_b):
    depth, d, n = ada_w.shape
    tn = n // 4
    return pl.pallas_call(
        _ada_kernel,
        out_shape=jax.ShapeDtypeStruct((depth, SUBLANES, n), F32),
        grid=(depth, n // tn),
        in_specs=[
            pl.BlockSpec((SUBLANES, d), lambda l, j: (0, 0)),
            pl.BlockSpec((None, d, tn), lambda l, j: (l, 0, j)),
            pl.BlockSpec((None, 1, tn), lambda l, j: (l, 0, j)),
        ],
        out_specs=pl.BlockSpec((None, SUBLANES, tn), lambda l, j: (l, 0, j)),
        compiler_params=_params(2),
        name="ada_tables",
    )(cvec, ada_w, ada_b.reshape(depth, 1, n))


def _read_tokens(refs, n_prompt_tiles):
    if len(refs) == 1:
        return refs[0][...]
    return jnp.where(pl.program_id(0) < n_prompt_tiles, refs[0][...], refs[1][...])


def _fnet_channel_dft(h, w_ref, zc_ref, zs_ref):
    gw = w_ref.shape[0]
    for g in range(FNET_GROUPS):
        sl = slice(g * gw, (g + 1) * gw)
        f = _dot(h[:, sl], w_ref[...])
        zc_ref[:, sl] = f[:, :gw].astype(BF16)
        zs_ref[:, sl] = f[:, gw:].astype(BF16)


def _ffn_kernel(*refs, n_chunks, n_x, n_m, tail, n_prompt_tiles, n_cast):
    x_refs, m_refs, refs = refs[:n_x], refs[n_x:n_x + n_m], refs[n_x + n_m:]
    wo_ref, bo_ref, g1_ref, sh_ref, sc_ref, g2_ref, nf_ref, wg_ref, wu_ref, wd_ref = refs[:10]
    refs, a_ref = refs[10:-1], refs[-1]
    if n_cast:
        for src, dst in zip(refs[:n_cast], refs[-n_cast:]):
            dst[...] = src[...].astype(BF16)
        refs = refs[n_cast:-n_cast]
    m = _read_tokens(m_refs, n_prompt_tiles)
    x1 = _read_tokens(x_refs, n_prompt_tiles) + g1_ref[...] * (_dot(m, wo_ref[...]) + bo_ref[...])
    h = _norm_mod(x1, nf_ref[...], sh_ref[...], sc_ref[...]).astype(BF16)
    for c in range(n_chunks):
        sl = slice(c * FFN_CHUNK, (c + 1) * FFN_CHUNK)
        g = _dot(h, wg_ref[:, sl])
        u = _dot(h, wu_ref[:, sl])
        a_ref[:, sl] = (g * jax.nn.sigmoid(g) * u).astype(BF16)
    x2 = x1 + g2_ref[...] * _dot(a_ref[...], wd_ref[...])
    if tail is None:
        refs[0][...] = x2
    elif tail == "fnet":
        nm_ref, sh1_ref, sc1_ref, dft_ref, o_ref, zc_ref, zs_ref = refs
        o_ref[...] = x2
        h_next = _norm_mod(x2, nm_ref[...], sh1_ref[...], sc1_ref[...]).astype(BF16)
        _fnet_channel_dft(h_next, dft_ref, zc_ref, zs_ref)
    else:
        fin_ref, op_ref, os_ref = refs
        y = _rms(x2, fin_ref[...])
        is_prompt = pl.program_id(0) < n_prompt_tiles

        @pl.when(is_prompt)
        def _():
            op_ref[...] = y

        @pl.when(jnp.logical_not(is_prompt))
        def _():
            os_ref[...] = y


def _cast_specs(stack, index, n_steps):
    _, rows, width = stack.shape
    span = 1
    while (rows * span) % n_steps or (rows * span // n_steps) % BF16_SUBLANES:
        span *= 2
    assert n_steps % span == 0
    block = rows * span // n_steps
    return (pl.BlockSpec((None, block, width), lambda i: (index, i // span, 0)),
            pl.BlockSpec((block, width), lambda i: (i // span, 0)),
            jax.ShapeDtypeStruct((rows, width), BF16))


def _ffn_layer(lay, x, m, w_out, b_out, mod, norm_ffn, weights, final_norm=None, fnet_next=None, cast_next=None):
    xs = x if isinstance(x, tuple) else (x,)
    ms = m if isinstance(m, tuple) else (m,)
    d = xs[0].shape[1]
    w_gate, w_up, w_down = weights
    dff = w_gate.shape[1]
    assert dff % FFN_CHUNK == 0
    tm = FFN_TILE
    n_steps = lay.t // tm
    npt = lay.prompt_tiles(tm)
    tile = pl.BlockSpec((tm, d), lambda i: (i, 0))
    split = [pl.BlockSpec((tm, d), lambda i: (jnp.minimum(i, npt - 1), 0)),
             pl.BlockSpec((tm, d), lambda i: (jnp.maximum(i - npt, 0), 0))]
    in_specs = (split if len(xs) == 2 else [tile]) + (split if len(ms) == 2 else [tile]) + [
        _resident((d, d)), _resident((1, d)),
        lay.mod_spec(tm, 2, d), lay.mod_spec(tm, 3, d), lay.mod_spec(tm, 4, d), lay.mod_spec(tm, 5, d),
        _resident((1, d)), _resident((d, dff)), _resident((d, dff)), _resident((dff, d))]
    args = list(xs) + list(ms) + [w_out, b_out, mod, mod, mod, mod, norm_ffn, w_gate, w_up, w_down]
    cast_out_specs, cast_out_shapes = [], []
    if cast_next is not None:
        index, stacks = cast_next
        for stack in stacks:
            src_spec, dst_spec, dst_shape = _cast_specs(stack, index, n_steps)
            in_specs.append(src_spec)
            args.append(stack)
            cast_out_specs.append(dst_spec)
            cast_out_shapes.append(dst_shape)
    out_shapes, out_specs, tail = [jax.ShapeDtypeStruct((lay.t, d), F32)], [tile], None
    if final_norm is not None:
        tail = "final"
        in_specs.append(_resident((1, d)))
        args.append(final_norm)
        out_shapes = [jax.ShapeDtypeStruct((lay.tp, d), F32), jax.ShapeDtypeStruct((lay.ts, d), F32)]
        out_specs = list(split)
    elif fnet_next is not None:
        tail = "fnet"
        norm_next, mod_next = fnet_next
        dft = _fnet_channel_table(d)
        in_specs += [_resident((1, d)), lay.mod_spec(tm, 0, d), lay.mod_spec(tm, 1, d), _resident(dft.shape)]
        args += [norm_next, mod_next, mod_next, dft]
        out_shapes += [jax.ShapeDtypeStruct((lay.t, d), BF16), jax.ShapeDtypeStruct((lay.t, d), BF16)]
        out_specs += [tile, tile]
    n_main = len(out_shapes)
    outs = pl.pallas_call(
        functools.partial(_ffn_kernel, n_chunks=dff // FFN_CHUNK, n_x=len(xs), n_m=len(ms), tail=tail,
                          n_prompt_tiles=npt, n_cast=len(cast_out_specs)),
        out_shape=tuple(out_shapes + cast_out_shapes),
        grid=(n_steps,),
        in_specs=in_specs,
        out_specs=tuple(out_specs + cast_out_specs),
        scratch_shapes=[pltpu.VMEM((tm, dff), BF16)],
        compiler_params=_params(1, "arbitrary"),
        name="ffn",
    )(*args)
    main = outs[0] if n_main == 1 else tuple(outs[:n_main])
    return main, tuple(outs[n_main:])


def _mla_qkv_kernel(xp_ref, xs_ref, nm_ref, sh_ref, sc_ref, cos_ref, sin_ref, wdq_ref, qn_ref, wqn_ref, wqr_ref, wqx_ref,
                    wdkv_ref, wdkvx_ref, kvn_ref, wk_ref, wv_ref, q_ref, z_ref, k_ref, v_ref, *, scale, n_prompt_tiles):
    x_all = _read_tokens((xp_ref, xs_ref), n_prompt_tiles)
    for j in range(x_all.shape[0] // SEQ_TILE):
        rows = slice(j * SEQ_TILE, (j + 1) * SEQ_TILE)
        h = _norm_mod(x_all[rows], nm_ref[...], sh_ref[...], sc_ref[...]).astype(BF16)
        cos, sin = cos_ref[rows, :], sin_ref[rows, :]
        cq = _rms(_dot(h, wdq_ref[...]), qn_ref[...]).astype(BF16)
        q_nope = _dot(cq, wqn_ref[...])
        q_rope = _dot(cq, wqr_ref[...])
        q_swap = _dot(cq, wqx_ref[...])
        for hd in range(MLA_HEADS):
            sl = slice(hd * LANES, (hd + 1) * LANES)
            q_ref[rows, hd * HEAD_W:hd * HEAD_W + LANES] = (q_nope[:, sl] * scale).astype(BF16)
            q_ref[rows, hd * HEAD_W + LANES:(hd + 1) * HEAD_W] = (
                (q_rope[:, sl] * cos + q_swap[:, sl] * sin) * scale).astype(BF16)
        z = _dot(h, wdkv_ref[...])
        z_swap = _dot(h, wdkvx_ref[...])
        ckv = _rms(z[:, :KV_LORA_RANK], kvn_ref[...])
        k_rope = z[:, KV_LORA_RANK:] * cos + z_swap * sin
        z_ref[rows, :KV_LORA_RANK] = ckv
        z_ref[rows, KV_LORA_RANK:] = k_rope
        _expand_kv(ckv, k_rope, wk_ref, wv_ref, k_ref, v_ref, rows)


def _expand_kv(ckv, k_rope, wk_ref, wv_ref, k_ref, v_ref, rows=slice(None)):
    ckv = ckv.astype(BF16)
    k_rope = k_rope.astype(BF16)
    k_nope = _dot(ckv, wk_ref[...])
    for hd in range(MLA_HEADS):
        k_ref[rows, hd * HEAD_W:hd * HEAD_W + LANES] = k_nope[:, hd * LANES:(hd + 1) * LANES].astype(BF16)
        k_ref[rows, hd * HEAD_W + LANES:(hd + 1) * HEAD_W] = k_rope
    v_ref[rows, :] = _dot(ckv, wv_ref[...]).astype(BF16)


def _kv_expand_kernel(z_ref, wk_ref, wv_ref, k_ref, v_ref):
    z = z_ref[...]
    _expand_kv(z[:, :KV_LORA_RANK], z[:, KV_LORA_RANK:], wk_ref, wv_ref, k_ref, v_ref)


def _kv_expand(z, wk, wv, n=None):
    zw = z.shape[1]
    n = z.shape[0] if n is None else n
    tm = 2 * SEQ_TILE if n % (2 * SEQ_TILE) == 0 else SEQ_TILE
    return pl.pallas_call(
        _kv_expand_kernel,
        out_shape=(jax.ShapeDtypeStruct((n, MLA_HEADS * HEAD_W), BF16),
                   jax.ShapeDtypeStruct((n, MLA_HEADS * V_HEAD_DIM), BF16)),
        grid=(n // tm,),
        in_specs=[pl.BlockSpec((tm, zw), lambda i: (i, 0)), _resident(wk.shape), _resident(wv.shape)],
        out_specs=(pl.BlockSpec((tm, MLA_HEADS * HEAD_W), lambda i: (i, 0)),
                   pl.BlockSpec((tm, MLA_HEADS * V_HEAD_DIM), lambda i: (i, 0))),
        compiler_params=_params(1),
        name="mla_kv_expand",
    )(z, wk, wv)


def _attn_kernel(*refs, heads, sub, n_kv):
    q_ref, kv_refs, o_ref = refs[0], refs[1:1 + 2 * n_kv], refs[-1]
    tq = q_ref.shape[0]
    for hd in range(heads):
        ks = [r[:, hd * HEAD_W:(hd + 1) * HEAD_W] for r in kv_refs[0::2]]
        vs = [r[:, hd * V_HEAD_DIM:(hd + 1) * V_HEAD_DIM] for r in kv_refs[1::2]]
        for j in range(tq // sub):
            rows = slice(j * sub, (j + 1) * sub)
            q = q_ref[rows, hd * HEAD_W:(hd + 1) * HEAD_W]
            ss = [lax.dot_general(q, k, (((1,), (1,)), ((), ())), preferred_element_type=F32) for k in ks]
            top = functools.reduce(jnp.maximum, [jnp.max(s, axis=-1, keepdims=True) for s in ss])
            ps = [jnp.exp(s - top) for s in ss]
            den = sum(jnp.sum(p, axis=-1, keepdims=True) for p in ps)
            o = sum(_dot(p.astype(BF16), v) for p, v in zip(ps, vs))
            o_ref[rows, hd * V_HEAD_DIM:(hd + 1) * V_HEAD_DIM] = (o / den).astype(BF16)


def _attention(q, kv, *, n_seq, q_len, q_row0, tq, heads):
    hg = MLA_HEADS // heads
    assert q_row0 % tq == 0 and q_len % tq == 0
    qb0, qt = q_row0 // tq, q_len // tq
    in_specs = [pl.BlockSpec((tq, heads * HEAD_W), lambda b, g, j: (qb0 + b * qt + j, g))]
    args = [q]
    for k, v, k_len, row0 in kv:
        assert row0 % k_len == 0
        kb0 = row0 // k_len
        in_specs += [pl.BlockSpec((k_len, heads * HEAD_W), lambda b, g, j, kb0=kb0: (kb0 + b, g)),
                     pl.BlockSpec((k_len, heads * V_HEAD_DIM), lambda b, g, j, kb0=kb0: (kb0 + b, g))]
        args += [k, v]
    return pl.pallas_call(
        functools.partial(_attn_kernel, heads=heads, sub=min(ATTN_SUB_TILE, tq), n_kv=len(kv)),
        out_shape=jax.ShapeDtypeStruct((n_seq * q_len, MLA_HEADS * V_HEAD_DIM), BF16),
        grid=(n_seq, hg, qt),
        in_specs=in_specs,
        out_specs=pl.BlockSpec((tq, heads * V_HEAD_DIM), lambda b, g, j: (b * qt + j, g)),
        compiler_params=_params(3),
        name="mla_attention",
    )(*args)


def _rope_tables(lay, tile):
    pos = np.arange(lay.ss)
    n_freq = QK_ROPE_DIM // 4
    inv_freq = ROPE_THETA ** (-np.arange(n_freq, dtype=np.float64) / n_freq)
    ang = np.concatenate([(pos // GRID_W)[:, None] * inv_freq, (pos % GRID_W)[:, None] * inv_freq], axis=-1)
    pad = np.zeros((lay.ss, LANES - QK_ROPE_DIM))
    cos = np.concatenate([np.cos(ang), np.cos(ang), pad], axis=-1)
    sin = np.concatenate([np.sin(ang), np.sin(ang), pad], axis=-1)
    ident_cos = np.concatenate([np.ones((tile, QK_ROPE_DIM)), np.zeros((tile, LANES - QK_ROPE_DIM))], axis=-1)
    cos = np.concatenate([ident_cos, cos], axis=0)
    sin = np.concatenate([np.zeros((tile, LANES)), sin], axis=0)
    return jnp.asarray(cos, F32), jnp.asarray(sin, F32)


def _swap_halves(w):
    half = QK_ROPE_DIM // 2
    return jnp.concatenate([-w[..., half:], w[..., :half]], axis=-1)


def _mla_layer(lay, x_prompt, x_sample, mod, norm_mix, cache_ckv, cache_krope, w_dq, q_norm, w_uq, w_dkv, kv_norm,
               w_ukv):
    t, d = lay.t, x_prompt.shape[1]
    tm = MLA_ROW_TILE
    rq = w_dq.shape[1]
    qk = QK_NOPE_DIM + QK_ROPE_DIM
    zw = KV_LORA_RANK + LANES
    rope_pad = [(0, 0)] * 2 + [(0, LANES - QK_ROPE_DIM)]

    wq = w_uq.reshape(rq, MLA_HEADS, qk)
    wq_nope = wq[:, :, :QK_NOPE_DIM].reshape(rq, -1).astype(BF16)
    wq_rope = jnp.pad(wq[:, :, QK_NOPE_DIM:], rope_pad).reshape(rq, -1).astype(BF16)
    wq_swap = jnp.pad(_swap_halves(wq[:, :, QK_NOPE_DIM:]), rope_pad).reshape(rq, -1).astype(BF16)
    wdkv = jnp.pad(w_dkv, [(0, 0), (0, LANES - QK_ROPE_DIM)]).astype(BF16)
    wdkv_swap = jnp.pad(_swap_halves(w_dkv[:, KV_LORA_RANK:]), [(0, 0), (0, LANES - QK_ROPE_DIM)]).astype(BF16)
    wkv = w_ukv.reshape(KV_LORA_RANK, MLA_HEADS, QK_NOPE_DIM + V_HEAD_DIM)
    wk = wkv[:, :, :QK_NOPE_DIM].reshape(KV_LORA_RANK, -1).astype(BF16)
    wv = wkv[:, :, QK_NOPE_DIM:].reshape(KV_LORA_RANK, -1).astype(BF16)

    cos, sin = _rope_tables(lay, tm)
    npt, spb = lay.prompt_tiles(tm), lay.sample_tiles_per_seq(tm)
    rope_spec = pl.BlockSpec((tm, LANES), lambda i: (jnp.where(i < npt, 0, 1 + (i - npt) % spb), 0))
    def rows(width):
        return pl.BlockSpec((tm, width), lambda i: (i, 0))

    q, z, k_new, v_new = pl.pallas_call(
        functools.partial(_mla_qkv_kernel, scale=qk ** -0.5, n_prompt_tiles=npt),
        out_shape=(jax.ShapeDtypeStruct((t, MLA_HEADS * HEAD_W), BF16), jax.ShapeDtypeStruct((t, zw), F32),
                   jax.ShapeDtypeStruct((t, MLA_HEADS * HEAD_W), BF16),
                   jax.ShapeDtypeStruct((t, MLA_HEADS * V_HEAD_DIM), BF16)),
        grid=(t // tm,),
        in_specs=[pl.BlockSpec((tm, d), lambda i: (jnp.minimum(i, npt - 1), 0)),
                  pl.BlockSpec((tm, d), lambda i: (jnp.maximum(i - npt, 0), 0)), _resident((1, d)),
                  lay.mod_spec(tm, 0, d), lay.mod_spec(tm, 1, d), rope_spec, rope_spec,
                  _resident(w_dq.shape), _resident((1, rq)), _resident(wq_nope.shape), _resident(wq_rope.shape),
                  _resident(wq_swap.shape), _resident(wdkv.shape), _resident(wdkv_swap.shape),
                  _resident((1, KV_LORA_RANK)), _resident(wk.shape), _resident(wv.shape)],
        out_specs=(rows(MLA_HEADS * HEAD_W), rows(zw), rows(MLA_HEADS * HEAD_W), rows(MLA_HEADS * V_HEAD_DIM)),
        compiler_params=_params(1),
        name="mla_qkv",
    )(x_prompt, x_sample, norm_mix, mod, mod, cos, sin, w_dq.astype(BF16), q_norm, wq_nope, wq_rope, wq_swap, wdkv,
      wdkv_swap, kv_norm, wk, wv)

    past = cache_ckv.shape[1]
    z_cache = jnp.concatenate(
        [cache_ckv, cache_krope, jnp.zeros((lay.bs, past, LANES - QK_ROPE_DIM), F32)], axis=-1)
    k_past, v_past = _kv_expand(z_cache.reshape(lay.bs * past, zw), wk, wv)

    o_prompt = _attention(q, [(k_new, v_new, lay.sp, 0)], n_seq=lay.bp, q_len=lay.sp, q_row0=0, tq=lay.sp,
                          heads=MLA_HEADS)
    o_sample = _attention(q, [(k_past, v_past, past, 0), (k_new, v_new, lay.ss, lay.tp)], n_seq=lay.bs, q_len=lay.ss,
                          q_row0=lay.tp, tq=min(ATTN_Q_TILE, lay.ss), heads=1)
    ckv_state = z[:lay.tp, :KV_LORA_RANK].reshape(lay.bp, 1, lay.sp, KV_LORA_RANK)
    krope_state = z[:lay.tp, KV_LORA_RANK:KV_LORA_RANK + QK_ROPE_DIM].reshape(lay.bp, 1, lay.sp, QK_ROPE_DIM)
    return (o_prompt, o_sample), ckv_state, krope_state


def _seq_position(lay, tile, t):
    npt, spb = lay.prompt_tiles(tile), lay.sample_tiles_per_seq(tile)
    ppb = lay.sp // tile
    r = jnp.where(t < npt, t % ppb, (t - npt) % spb)
    n = jnp.where(t < npt, ppb, spb)
    return r, n


def _softplus(x):
    return jnp.maximum(x, 0.0) + jnp.log1p(jnp.exp(-jnp.abs(x)))


def _rglru_kernel(*refs, lay, n_tiles, reverse):
    if reverse:
        (xc_ref, hc_ref, yf_ref, h0_ref, wa_ref, ba_ref, wi_ref, bi_ref, lam_ref, wy_ref,
         o_ref, hl_ref, a_scr, u_scr, perm_scr, carry_scr) = refs
    else:
        (xp_ref, x_ref, xn_ref, nm_ref, sh_ref, sc_ref, h0_ref, wx_ref, cw_ref, cb_ref,
         wa_ref, ba_ref, wi_ref, bi_ref, lam_ref,
         o_ref, xc_ref, hc_ref, hl_ref, xw_scr, a_scr, u_scr, perm_scr, carry_scr) = refs
    tm, d = xc_ref.shape
    groups = tm // SUBLANES
    bw = d // RG_BLOCKS
    i = pl.program_id(0)
    t = n_tiles - 1 - i if reverse else i
    r, n = _seq_position(lay, tm, t)

    @pl.when(i == 0)
    def _():
        carry_scr[...] = jnp.zeros_like(carry_scr)

    if reverse:
        xc = xc_ref[...]
    else:
        has_prev, has_next = r > 0, r < n - 1

        def pre(x):
            return _norm_mod(x, nm_ref[...], sh_ref[...], sc_ref[...]).astype(BF16)

        x_nat = x_ref[...]
        for c in range(d // LANES):
            for s in range(SUBLANES):
                perm_scr[c, pl.ds(s, groups, stride=SUBLANES), :] = x_nat[s * groups:(s + 1) * groups,
                                                                         c * LANES:(c + 1) * LANES]
        hc = pre(jnp.concatenate([perm_scr[c] for c in range(d // LANES)], axis=-1))
        hc_ref[...] = hc
        xw = _dot(jnp.concatenate([pre(xp_ref[...]), hc, pre(xn_ref[...])], axis=0), wx_ref[...])
        sub = lax.broadcasted_iota(jnp.int32, (SUBLANES, d), 0)

        def tile_group(j):
            return xw[RG_HALO + j * SUBLANES:RG_HALO + (j + 1) * SUBLANES]

        def halo_row(k, valid):
            return jnp.broadcast_to(jnp.where(valid, xw[k:k + 1], 0.0), (SUBLANES, d))

        before = jnp.where(sub == 0, halo_row(RG_HALO - 1, has_prev), pltpu.roll(tile_group(groups - 1), 1, 0))
        after = [jnp.where(sub == SUBLANES - 1, halo_row(RG_HALO + tm + j, has_next),
                           pltpu.roll(tile_group(j), SUBLANES - 1, 0)) for j in range(RG_CONV_W - 2)]
        xw_scr[0:SUBLANES] = before
        xw_scr[SUBLANES:SUBLANES + tm] = xw[RG_HALO:RG_HALO + tm]
        for j, grp in enumerate(after):
            xw_scr[(groups + 1 + j) * SUBLANES:(groups + 2 + j) * SUBLANES] = grp
        xc = cb_ref[...]
        for k in range(RG_CONV_W):
            xc = xc + cw_ref[k:k + 1, :] * xw_scr[k * SUBLANES:k * SUBLANES + tm]
        xc_ref[...] = xc

    for nb in range(RG_BLOCKS):
        sl = slice(nb * bw, (nb + 1) * bw)
        xcn = xc[:, sl]
        xcb = xcn.astype(BF16)
        rg = jax.nn.sigmoid(_dot(xcb, wa_ref[nb]) + ba_ref[:, sl])
        ig = jax.nn.sigmoid(_dot(xcb, wi_ref[nb]) + bi_ref[:, sl])
        log_a = -RG_C * rg * _softplus(-lam_ref[:, sl])
        a = jnp.exp(log_a)
        u = jnp.sqrt(-jnp.tanh(log_a) * (a * a + 1.0)) * (ig * xcn)
        a_scr[:, :, sl] = a.reshape(groups, SUBLANES, bw)
        u_scr[:, :, sl] = u.reshape(groups, SUBLANES, bw)

    def local_step(g, carry):
        h, p = carry
        j = groups - 1 - g if reverse else g
        a = a_scr[j]
        h = a * h + u_scr[j]
        p = a * p
        u_scr[j] = h
        a_scr[j] = p
        return h, p

    h_seg, p_seg = lax.fori_loop(0, groups, local_step, (jnp.zeros((SUBLANES, d), F32), jnp.ones((SUBLANES, d), F32)))
    is_start = r == n - 1 if reverse else r == 0
    state = jnp.where(is_start, h0_ref[...], carry_scr[...])
    entering = [None] * SUBLANES
    for s in (range(SUBLANES - 1, -1, -1) if reverse else range(SUBLANES)):
        entering[s] = state
        state = h_seg[s:s + 1] + p_seg[s:s + 1] * state
    carry_scr[...] = state
    hl_ref[...] = state
    y = u_scr[...] + a_scr[...] * jnp.concatenate(entering, axis=0)[None]
    y = y.reshape(tm, d)
    if reverse:
        gate = jax.nn.gelu(_dot(hc_ref[...], wy_ref[...]))
        mixed = (yf_ref[...] + y) * gate
        for c in range(d // LANES):
            perm_scr[c] = mixed[:, c * LANES:(c + 1) * LANES]
        for s in range(SUBLANES):
            for c in range(d // LANES):
                o_ref[s * groups:(s + 1) * groups, c * LANES:(c + 1) * LANES] = (
                    perm_scr[c, pl.ds(s, groups, stride=SUBLANES), :].astype(BF16))
    else:
        o_ref[...] = y


def _rglru_layer(lay, x, mod, norm_mix, state, w_x, w_y, conv_w, conv_b, w_a, b_a, w_i, b_i, lam):
    t, d = x.shape
    tm = SEQ_TILE
    n_tiles = t // tm
    hb = tm // SUBLANES
    halo_blocks, n_halo = tm // RG_HALO, t // RG_HALO
    bw = d // RG_BLOCKS

    def run(reverse, *fwd_out):
        dr = int(reverse)
        order = (lambda i: n_tiles - 1 - i) if reverse else (lambda i: i)
        h0 = jnp.concatenate([jnp.zeros((1, d), F32), state[:, dr], jnp.zeros((SUBLANES - 1 - lay.bs, d), F32)])
        h0_spec = lay.row_spec(tm, order)(d)
        tile = pl.BlockSpec((tm, d), lambda i: (order(i), 0))
        gate_specs = [_resident((RG_BLOCKS, bw, bw)), _resident((1, d)), _resident((RG_BLOCKS, bw, bw)),
                      _resident((1, d)), _resident((1, d))]
        gate_args = [w_a[dr].astype(BF16), b_a[dr:dr + 1], w_i[dr].astype(BF16), b_i[dr:dr + 1], lam[dr:dr + 1]]
        scan_scratch = [pltpu.VMEM((hb, SUBLANES, d), F32), pltpu.VMEM((hb, SUBLANES, d), F32),
                        pltpu.VMEM((d // LANES, tm, LANES), F32)]
        last_spec = pl.BlockSpec((None, 1, d), lambda i: (order(i), 0, 0))
        last_shape = jax.ShapeDtypeStruct((n_tiles, 1, d), F32)
        if reverse:
            xc, hc, y_fwd = fwd_out
            in_specs = [tile, tile, tile, h0_spec] + gate_specs + [_resident((d, d))]
            args = [xc, hc, y_fwd, h0.reshape(SUBLANES, 1, d)] + gate_args + [w_y.astype(BF16)]
            out_shape = (jax.ShapeDtypeStruct((t, d), BF16), last_shape)
            out_specs = (tile, last_spec)
            scratch = list(scan_scratch)
        else:
            in_specs = [pl.BlockSpec((RG_HALO, d), lambda i: (jnp.maximum(i * halo_blocks - 1, 0), 0)),
                        tile,
                        pl.BlockSpec((RG_HALO, d), lambda i: (jnp.minimum((i + 1) * halo_blocks, n_halo - 1), 0)),
                        _resident((1, d)), lay.mod_spec(tm, 0, d), lay.mod_spec(tm, 1, d), h0_spec,
                        _resident((d, d)), _resident((RG_CONV_W, d)), _resident((1, d))] + gate_specs
            args = [x, x, x, norm_mix, mod, mod, h0.reshape(SUBLANES, 1, d), w_x.astype(BF16), conv_w, conv_b] + gate_args
            out_shape = (jax.ShapeDtypeStruct((t, d), F32), jax.ShapeDtypeStruct((t, d), F32),
                         jax.ShapeDtypeStruct((t, d), BF16), last_shape)
            out_specs = (tile, tile, tile, last_spec)
            scratch = [pltpu.VMEM((tm + 2 * RG_HALO, d), F32)] + scan_scratch
        scratch.append(pltpu.VMEM((1, d), F32))
        return pl.pallas_call(
            functools.partial(_rglru_kernel, lay=lay, n_tiles=n_tiles, reverse=reverse),
            out_shape=out_shape,
            grid=(n_tiles,),
            in_specs=in_specs,
            out_specs=out_specs,
            scratch_shapes=scratch,
            compiler_params=_params(1, "arbitrary"),
            name="rglru_bwd" if reverse else "rglru_fwd",
        )(*args)

    y_fwd, xc, hc, last_f = run(False)
    m, last_b = run(True, xc, hc, y_fwd)
    ppb = lay.sp // tm
    tail = last_f[:lay.prompt_tiles(tm)].reshape(lay.bp, ppb, d)[:, ppb - 1]
    head = last_b[:lay.prompt_tiles(tm)].reshape(lay.bp, ppb, d)[:, 0]
    return m, jnp.stack([tail, head], axis=1)[:, None]


def _dft_tables(n, scale):
    jk = np.outer(np.arange(n), np.arange(n)) % n
    ang = 2.0 * np.pi * jk / n
    return np.cos(ang) * scale, np.sin(ang) * scale


def _fnet_channel_table(d):
    gw = d // FNET_GROUPS
    cos, sin = _dft_tables(gw, gw ** -0.5)
    return jnp.asarray(np.concatenate([cos, sin], axis=1), F32).astype(BF16)


def _fnet_position_kernel(wc_ref, ws_ref, zc_ref, zs_ref, o_ref, *, seq_len):
    for i in range(zc_ref.shape[0] // seq_len):
        rows = slice(i * seq_len, (i + 1) * seq_len)
        o_ref[i * wc_ref.shape[0]:(i + 1) * wc_ref.shape[0], :] = (
            _dot(wc_ref[...], zc_ref[rows, :]) - _dot(ws_ref[...], zs_ref[rows, :])).astype(BF16)


def _fnet_position(zc, zs, *, n_seq, seq_len, row0, tm, seqs_per_step=1):
    d = zc.shape[1]
    g = seqs_per_step
    assert row0 % (g * seq_len) == 0 and seq_len % tm == 0 and n_seq % g == 0 and (g == 1 or tm == seq_len)
    cos, sin = _dft_tables(seq_len, seq_len ** -0.5)
    wc, ws = jnp.asarray(cos, F32).astype(BF16), jnp.asarray(sin, F32).astype(BF16)
    sb0, rt = row0 // (g * seq_len), seq_len // tm
    w_spec = pl.BlockSpec((tm, seq_len), lambda b, j: (j, 0))
    z_spec = pl.BlockSpec((g * seq_len, d), lambda b, j: (sb0 + b, 0))
    return pl.pallas_call(
        functools.partial(_fnet_position_kernel, seq_len=seq_len),
        out_shape=jax.ShapeDtypeStruct((n_seq * seq_len, d), BF16),
        grid=(n_seq // g, rt),
        in_specs=[w_spec, w_spec, z_spec, z_spec],
        out_specs=pl.BlockSpec((g * tm, d), lambda b, j: (b * rt + j, 0)),
        compiler_params=_params(2),
        name="fnet_position",
    )(wc, ws, zc, zs)


def _fnet_layer(lay, zc, zs):
    group = FNET_PROMPT_GROUP if lay.bp % FNET_PROMPT_GROUP == 0 else 1
    return (_fnet_position(zc, zs, n_seq=lay.bp, seq_len=lay.sp, row0=0, tm=lay.sp, seqs_per_step=group),
            _fnet_position(zc, zs, n_seq=lay.bs, seq_len=lay.ss, row0=lay.tp, tm=min(FNET_ROW_TILE, lay.ss)))


def _conformer_kernel(xp_ref, xc_ref, xn_ref, nm_ref, sh_ref, sc_ref, w1_ref, b1_ref, dw_ref, db_ref, lg_ref, lb_ref,
                      o_ref, src_scr, acc_scr, *, lay):
    tm, d = xc_ref.shape
    r, n = _seq_position(lay, tm, pl.program_id(0))
    has_prev, has_next = r > 0, r < n - 1

    def pre(ref):
        return _norm_mod(ref[...], nm_ref[...], sh_ref[...], sc_ref[...]).astype(BF16)

    h = jnp.concatenate([pre(xp_ref), pre(xc_ref), pre(xn_ref)], axis=0)
    groups = tm // SUBLANES
    n_before = (CONF_CONV_W - 1) // 2
    n_after = CONF_CONV_W - 1 - n_before
    sub = lax.broadcasted_iota(jnp.int32, (SUBLANES, LANES), 0)

    def group_rows(g):
        return slice(g * SUBLANES, (g + 1) * SUBLANES)

    for j in range(d // CONF_GLU_CHUNK):
        val = slice(j * CONF_GLU_CHUNK, (j + 1) * CONF_GLU_CHUNK)
        gate = slice(d + j * CONF_GLU_CHUNK, d + (j + 1) * CONF_GLU_CHUNK)
        glu = (_dot(h, w1_ref[:, val]) + b1_ref[:, val]) * jax.nn.sigmoid(_dot(h, w1_ref[:, gate]) + b1_ref[:, gate])
        for i in range(CONF_GLU_CHUNK // LANES):
            c = j * CONF_GLU_CHUNK // LANES + i
            col = glu[:, i * LANES:(i + 1) * LANES]
            for s in range(SUBLANES):
                src_scr[c, pl.ds(n_before * SUBLANES + s, groups, stride=SUBLANES), :] = (
                    col[CONF_HALO + s * groups:CONF_HALO + (s + 1) * groups])
            for k in range(1, n_before + 1):
                edge = jnp.where(has_prev, col[CONF_HALO - k:CONF_HALO - k + 1], 0.0)
                inner = pltpu.roll(src_scr[c, group_rows(n_before + groups - k), :], 1, 0)
                src_scr[c, group_rows(n_before - k), :] = jnp.where(sub == 0, edge, inner)
            for k in range(n_after):
                edge = jnp.where(has_next, col[CONF_HALO + tm + k:CONF_HALO + tm + k + 1], 0.0)
                inner = pltpu.roll(src_scr[c, group_rows(n_before + k), :], SUBLANES - 1, 0)
                src_scr[c, group_rows(n_before + groups + k), :] = jnp.where(sub == SUBLANES - 1, edge, inner)
    for c in range(d // LANES):
        lanes = slice(c * LANES, (c + 1) * LANES)
        w = [jnp.broadcast_to(dw_ref[k:k + 1, lanes], (SUBLANES, LANES)) for k in range(CONF_CONV_W)]
        bias = jnp.broadcast_to(db_ref[:, lanes], (SUBLANES, LANES))

        def conv_rows(g, carry, c=c, w=w, bias=bias):
            row0 = pl.multiple_of(g * SUBLANES, SUBLANES)
            part = [bias] + [None] * (CONF_CHAINS - 1)
            for k in range(CONF_CONV_W):
                term = w[k] * src_scr[c, pl.ds(row0 + k * SUBLANES, SUBLANES), :]
                j = k % CONF_CHAINS
                part[j] = term if part[j] is None else part[j] + term
            while len(part) > 1:
                part = [a + b for a, b in zip(part[0::2], part[1::2])]
            acc_scr[c, pl.ds(row0, SUBLANES), :] = part[0]
            return carry

        lax.fori_loop(0, groups, conv_rows, 0, unroll=8)
    acc = jnp.concatenate([acc_scr[c] for c in range(d // LANES)], axis=-1)
    mu = jnp.mean(acc, axis=-1, keepdims=True)
    cen = acc - mu
    var = jnp.mean(cen * cen, axis=-1, keepdims=True)
    y = cen * lax.rsqrt(var + EPS) * lg_ref[...] + lb_ref[...]
    y = y * jax.nn.sigmoid(y)
    for c in range(d // LANES):
        acc_scr[c] = y[:, c * LANES:(c + 1) * LANES]
    for s in range(SUBLANES):
        for c in range(d // LANES):
            o_ref[s * groups:(s + 1) * groups, c * LANES:(c + 1) * LANES] = (
                acc_scr[c, pl.ds(s, groups, stride=SUBLANES), :].astype(BF16))


def _conformer_layer(lay, x, mod, norm_mix, w_pw1, b_pw1, dw_w, dw_b, ln_g, ln_b):
    t, d = x.shape
    tm = SEQ_TILE
    hb = tm // CONF_HALO
    n_halo = t // CONF_HALO
    tile = pl.BlockSpec((tm, d), lambda i: (i, 0))
    return pl.pallas_call(
        functools.partial(_conformer_kernel, lay=lay),
        out_shape=jax.ShapeDtypeStruct((t, d), BF16),
        grid=(t // tm,),
        in_specs=[pl.BlockSpec((CONF_HALO, d), lambda i: (jnp.maximum(i * hb - 1, 0), 0)),
                  tile,
                  pl.BlockSpec((CONF_HALO, d), lambda i: (jnp.minimum((i + 1) * hb, n_halo - 1), 0)),
                  _resident((1, d)), lay.mod_spec(tm, 0, d), lay.mod_spec(tm, 1, d),
                  _resident((d, 2 * d)), _resident((1, 2 * d)), _resident((CONF_CONV_W, d)), _resident((1, d)),
                  _resident((1, d)), _resident((1, d))],
        out_specs=tile,
        scratch_shapes=[pltpu.VMEM((d // LANES, tm + (CONF_CONV_W - 1) * SUBLANES, LANES), F32),
                        pltpu.VMEM((d // LANES, tm, LANES), F32)],
        compiler_params=_params(1),
        name="conformer",
    )(x, x, x, norm_mix, mod, mod, w_pw1.astype(BF16), b_pw1, dw_w, dw_b, ln_g, ln_b)


def kernel(x_prompt, x_sample, cache_mla_ckv, cache_mla_krope, state_rglru, c, c_ctx, ada_w, ada_b, norm_mix, norm_ffn, mla_w_dq, mla_q_norm, mla_w_uq, mla_w_dkv, mla_kv_norm, mla_w_ukv, mla_w_o, rg_w_x, rg_w_y, rg_conv_w, rg_conv_b, rg_w_a, rg_b_a, rg_w_i, rg_b_i, rg_lam, rg_w_o, fn_w_o, fn_b_o, cf_w_pw1, cf_b_pw1, cf_dw_w, cf_dw_b, cf_ln_g, cf_ln_b, cf_w_pw2, cf_b_pw2, ffn_w_gate, ffn_w_up, ffn_w_down, final_norm):
    bp, sp, d = x_prompt.shape
    bs, ss, _ = x_sample.shape
    depth = ada_w.shape[0]
    assert depth == 4 and bs < SUBLANES and sp == SEQ_TILE and ss % GRID_W == 0
    lay = _Layout(bp, sp, bs, ss)
    x_in = (x_prompt.reshape(lay.tp, d), x_sample.reshape(lay.ts, d))

    cvec = jnp.concatenate([c_ctx[None], c, jnp.zeros((SUBLANES - 1 - bs, d), F32)], axis=0)
    mods = _ada_tables(cvec, ada_w, ada_b).reshape(depth, SUBLANES, 1, -1)
    zero_bias = jnp.zeros((1, d), F32)
    ffn_stacks = (ffn_w_gate, ffn_w_up, ffn_w_down)
    weights = tuple(w[0].astype(BF16) for w in ffn_stacks)

    def ffn(layer, weights, x, m, w_out, b_out, **tail):
        cast_next = (layer + 1, ffn_stacks) if layer + 1 < depth else None
        return _ffn_layer(lay, x, m, w_out.astype(BF16), b_out, mods[layer], norm_ffn[layer:layer + 1], weights,
                          cast_next=cast_next, **tail)

    m, ckv_state, krope_state = _mla_layer(
        lay, *x_in, mods[0], norm_mix[0:1], cache_mla_ckv[:, 0], cache_mla_krope[:, 0], mla_w_dq[0], mla_q_norm[0:1],
        mla_w_uq[0], mla_w_dkv[0], mla_kv_norm[0:1], mla_w_ukv[0])
    x, weights = ffn(0, weights, x_in, m, mla_w_o[0], zero_bias)

    m, rg_state = _rglru_layer(lay, x, mods[1], norm_mix[1:2], state_rglru[:, 0], rg_w_x[0], rg_w_y[0], rg_conv_w[0],
                               rg_conv_b[0:1], rg_w_a[0], rg_b_a[0], rg_w_i[0], rg_b_i[0], rg_lam[0])
    (x, zc, zs), weights = ffn(1, weights, x, m, rg_w_o[0], zero_bias, fnet_next=(norm_mix[2:3], mods[2]))

    m = _fnet_layer(lay, zc, zs)
    x, weights = ffn(2, weights, x, m, fn_w_o[0], fn_b_o[0:1])

    m = _conformer_layer(lay, x, mods[3], norm_mix[3:4], cf_w_pw1[0], cf_b_pw1[0:1], cf_dw_w[0], cf_dw_b[0:1],
                         cf_ln_g[0:1], cf_ln_b[0:1])
    (y_prompt, y_sample), _ = ffn(3, weights, x, m, cf_w_pw2[0], cf_b_pw2[0:1], final_norm=final_norm[None])

    return (y_prompt.reshape(bp, sp, d), y_sample.reshape(bs, ss, d), ckv_state, krope_state, rg_state)
```

```python
import functools

import numpy as np
import jax
import jax.numpy as jnp
from jax import lax
from jax.experimental import pallas as pl
from jax.experimental.pallas import tpu as pltpu

F32 = jnp.float32
BF16 = jnp.bfloat16

EPS = 1e-6
GRID_W = 64
MLA_HEADS = 8
KV_LORA_RANK = 256
QK_NOPE_DIM = 128
QK_ROPE_DIM = 64
V_HEAD_DIM = 128
ROPE_THETA = 10000.0
RG_BLOCKS = 4
RG_CONV_W = 4
RG_C = 8.0
FNET_GROUPS = 4
CONF_CONV_W = 31

LANES = 128
SUBLANES = 8
BF16_SUBLANES = 16
HEAD_W = 2 * LANES
SEQ_TILE = 256
FFN_TILE = 512
MLA_ROW_TILE = 512
ATTN_Q_TILE = 2048
ATTN_SUB_TILE = 256
FNET_ROW_TILE = 512
FNET_PROMPT_GROUP = 4
FFN_CHUNK = 256
CONF_HALO = 16
RG_HALO = 16
CONF_CHAINS = 4
CONF_GLU_CHUNK = 256
VMEM_LIMIT = 52 * 1024 * 1024


def _dot(a, b):
    return jnp.dot(a, b, preferred_element_type=F32)


def _rms(x, g):
    return x * lax.rsqrt(jnp.mean(x * x, axis=-1, keepdims=True) + EPS) * g


def _norm_mod(x, g, shift, scale):
    return _rms(x, g) * (1.0 + scale) + shift


def _resident(shape):
    nd = len(shape)
    return pl.BlockSpec(shape, lambda *_: (0,) * nd, pipeline_mode=pl.Buffered(1))


def _resident_at(index, shape):
    nd = len(shape)
    return pl.BlockSpec((None,) + tuple(shape), lambda *_: (index,) + (0,) * nd, pipeline_mode=pl.Buffered(1))


def _params(n_axes, semantics="parallel"):
    return pltpu.CompilerParams(dimension_semantics=(semantics,) * n_axes, vmem_limit_bytes=VMEM_LIMIT)


class _Layout:
    def __init__(self, n_prompt_seq, prompt_len, n_sample_seq, sample_len):
        self.bp, self.sp, self.bs, self.ss = n_prompt_seq, prompt_len, n_sample_seq, sample_len
        self.tp = n_prompt_seq * prompt_len
        self.ts = n_sample_seq * sample_len
        self.t = self.tp + self.ts

    def prompt_tiles(self, tile):
        assert self.tp % tile == 0 and self.ss % tile == 0
        return self.tp // tile

    def sample_tiles_per_seq(self, tile):
        return self.ss // tile

    def mod_row(self, tile):
        npt, spb = self.prompt_tiles(tile), self.sample_tiles_per_seq(tile)
        return lambda i: jnp.where(i < npt, 0, 1 + (i - npt) // spb)

    def mod_spec(self, tile, chunk, d, order=lambda i: i):
        row = self.mod_row(tile)
        return pl.BlockSpec((None, 1, d), lambda i: (row(order(i)), 0, chunk))

    def row_spec(self, tile, order=lambda i: i):
        row = self.mod_row(tile)
        return lambda d: pl.BlockSpec((None, 1, d), lambda i: (row(order(i)), 0, 0))


def _ada_kernel(c_ref, w_ref, b_ref, o_ref):
    c = c_ref[...]
    s = (c * jax.nn.sigmoid(c)).astype(BF16)
    o_ref[...] = _dot(s, w_ref[...].astype(BF16)) + b_ref[...]


def _ada_tables(cvec, ada_w, ada_b):
    depth, d, n = ada_w.shape
    tn = n // 4
    return pl.pallas_call(
        _ada_kernel,
        out_shape=jax.ShapeDtypeStruct((depth, SUBLANES, n), F32),
        grid=(depth, n // tn),
        in_specs=[
            pl.BlockSpec((SUBLANES, d), lambda l, j: (0, 0)),
            pl.BlockSpec((None, d, tn), lambda l, j: (l, 0, j)),
            pl.BlockSpec((None, 1, tn), lambda l, j: (l, 0, j)),
        ],
        out_specs=pl.BlockSpec((None, SUBLANES, tn), lambda l, j: (l, 0, j)),
        compiler_params=_params(2),
        name="ada_tables",
    )(cvec, ada_w, ada_b.reshape(depth, 1, n))


def _read_tokens(refs, n_prompt_tiles):
    if len(refs) == 1:
        return refs[0][...]
    return jnp.where(pl.program_id(0) < n_prompt_tiles, refs[0][...], refs[1][...])


def _fnet_channel_dft(h, w_ref, zc_ref, zs_ref):
    gw = w_ref.shape[0]
    for g in range(FNET_GROUPS):
        sl = slice(g * gw, (g + 1) * gw)
        f = _dot(h[:, sl], w_ref[...])
        zc_ref[:, sl] = f[:, :gw].astype(BF16)
        zs_ref[:, sl] = f[:, gw:].astype(BF16)


def _ffn_kernel(*refs, n_chunks, n_x, n_m, tail, n_prompt_tiles, n_cast):
    x_refs, m_refs, refs = refs[:n_x], refs[n_x:n_x + n_m], refs[n_x + n_m:]
    wo_ref, bo_ref, g1_ref, sh_ref, sc_ref, g2_ref, nf_ref, wg_ref, wu_ref, wd_ref = refs[:10]
    refs, a_ref = refs[10:-1], refs[-1]
    if n_cast:
        for src, dst in zip(refs[:n_cast], refs[-n_cast:]):
            dst[...] = src[...].astype(BF16)
        refs = refs[n_cast:-n_cast]
    m = _read_tokens(m_refs, n_prompt_tiles)
    x1 = _read_tokens(x_refs, n_prompt_tiles) + g1_ref[...] * (_dot(m, wo_ref[...]) + bo_ref[...])
    h = _norm_mod(x1, nf_ref[...], sh_ref[...], sc_ref[...]).astype(BF16)
    for c in range(n_chunks):
        sl = slice(c * FFN_CHUNK, (c + 1) * FFN_CHUNK)
        g = _dot(h, wg_ref[:, sl])
        u = _dot(h, wu_ref[:, sl])
        a_ref[:, sl] = (g * jax.nn.sigmoid(g) * u).astype(BF16)
    x2 = x1 + g2_ref[...] * _dot(a_ref[...], wd_ref[...])
    if tail is None:
        refs[0][...] = x2
    elif tail == "fnet":
        nm_ref, sh1_ref, sc1_ref, dft_ref, o_ref, zc_ref, zs_ref = refs
        o_ref[...] = x2
        h_next = _norm_mod(x2, nm_ref[...], sh1_ref[...], sc1_ref[...]).astype(BF16)
        _fnet_channel_dft(h_next, dft_ref, zc_ref, zs_ref)
    else:
        fin_ref, op_ref, os_ref = refs
        y = _rms(x2, fin_ref[...])
        is_prompt = pl.program_id(0) < n_prompt_tiles

        @pl.when(is_prompt)
        def _():
            op_ref[...] = y

        @pl.when(jnp.logical_not(is_prompt))
        def _():
            os_ref[...] = y


def _cast_specs(stack, index, n_steps):
    _, rows, width = stack.shape
    span = 1
    while (rows * span) % n_steps or (rows * span // n_steps) % BF16_SUBLANES:
        span *= 2
    assert n_steps % span == 0
    block = rows * span // n_steps
    return (pl.BlockSpec((None, block, width), lambda i: (index, i // span, 0)),
            pl.BlockSpec((block, width), lambda i: (i // span, 0)),
            jax.ShapeDtypeStruct((rows, width), BF16))


def _ffn_layer(lay, x, m, w_out, b_out, mod, norm_ffn, weights, final_norm=None, fnet_next=None, cast_next=None):
    xs = x if isinstance(x, tuple) else (x,)
    ms = m if isinstance(m, tuple) else (m,)
    d = xs[0].shape[1]
    w_gate, w_up, w_down = weights
    dff = w_gate.shape[1]
    assert dff % FFN_CHUNK == 0
    tm = FFN_TILE
    n_steps = lay.t // tm
    npt = lay.prompt_tiles(tm)
    tile = pl.BlockSpec((tm, d), lambda i: (i, 0))
    split = [pl.BlockSpec((tm, d), lambda i: (jnp.minimum(i, npt - 1), 0)),
             pl.BlockSpec((tm, d), lambda i: (jnp.maximum(i - npt, 0), 0))]
    in_specs = (split if len(xs) == 2 else [tile]) + (split if len(ms) == 2 else [tile]) + [
        _resident((d, d)), _resident((1, d)),
        lay.mod_spec(tm, 2, d), lay.mod_spec(tm, 3, d), lay.mod_spec(tm, 4, d), lay.mod_spec(tm, 5, d),
        _resident((1, d)), _resident((d, dff)), _resident((d, dff)), _resident((dff, d))]
    args = list(xs) + list(ms) + [w_out, b_out, mod, mod, mod, mod, norm_ffn, w_gate, w_up, w_down]
    cast_out_specs, cast_out_shapes = [], []
    for stack, index in cast_next or ():
        src_spec, dst_spec, dst_shape = _cast_specs(stack, index, n_steps)
        in_specs.append(src_spec)
        args.append(stack)
        cast_out_specs.append(dst_spec)
        cast_out_shapes.append(dst_shape)
    out_shapes, out_specs, tail = [jax.ShapeDtypeStruct((lay.t, d), F32)], [tile], None
    if final_norm is not None:
        tail = "final"
        in_specs.append(_resident((1, d)))
        args.append(final_norm)
        out_shapes = [jax.ShapeDtypeStruct((lay.tp, d), F32), jax.ShapeDtypeStruct((lay.ts, d), F32)]
        out_specs = list(split)
    elif fnet_next is not None:
        tail = "fnet"
        norm_next, mod_next = fnet_next
        dft = _fnet_channel_table(d)
        in_specs += [_resident((1, d)), lay.mod_spec(tm, 0, d), lay.mod_spec(tm, 1, d), _resident(dft.shape)]
        args += [norm_next, mod_next, mod_next, dft]
        out_shapes += [jax.ShapeDtypeStruct((lay.t, d), BF16), jax.ShapeDtypeStruct((lay.t, d), BF16)]
        out_specs += [tile, tile]
    n_main = len(out_shapes)
    outs = pl.pallas_call(
        functools.partial(_ffn_kernel, n_chunks=dff // FFN_CHUNK, n_x=len(xs), n_m=len(ms), tail=tail,
                          n_prompt_tiles=npt, n_cast=len(cast_out_specs)),
        out_shape=tuple(out_shapes + cast_out_shapes),
        grid=(n_steps,),
        in_specs=in_specs,
        out_specs=tuple(out_specs + cast_out_specs),
        scratch_shapes=[pltpu.VMEM((tm, dff), BF16)],
        compiler_params=_params(1, "arbitrary"),
        name="ffn",
    )(*args)
    main = outs[0] if n_main == 1 else tuple(outs[:n_main])
    return main, tuple(outs[n_main:])


def _mla_qkv_kernel(*refs, scale, n_prompt_tiles, n_cast):
    (xp_ref, xs_ref, nm_ref, sh_ref, sc_ref, cos_ref, sin_ref, wdq_ref, qn_ref, wqn_ref, wqr_ref, wqx_ref,
     wdkv_ref, wdkvx_ref, kvn_ref, wk_ref, wv_ref) = refs[:17]
    q_ref, z_ref, k_ref, v_ref = refs[17 + n_cast:21 + n_cast]
    for src, dst in zip(refs[17:17 + n_cast], refs[21 + n_cast:]):
        dst[...] = src[...].astype(BF16)
    x_all = _read_tokens((xp_ref, xs_ref), n_prompt_tiles)
    for j in range(x_all.shape[0] // SEQ_TILE):
        rows = slice(j * SEQ_TILE, (j + 1) * SEQ_TILE)
        h = _norm_mod(x_all[rows], nm_ref[...], sh_ref[...], sc_ref[...]).astype(BF16)
        cos, sin = cos_ref[rows, :], sin_ref[rows, :]
        cq = _rms(_dot(h, wdq_ref[...]), qn_ref[...]).astype(BF16)
        q_nope = _dot(cq, wqn_ref[...])
        q_rope = _dot(cq, wqr_ref[...])
        q_swap = _dot(cq, wqx_ref[...])
        for hd in range(MLA_HEADS):
            sl = slice(hd * LANES, (hd + 1) * LANES)
            q_ref[rows, hd * HEAD_W:hd * HEAD_W + LANES] = (q_nope[:, sl] * scale).astype(BF16)
            q_ref[rows, hd * HEAD_W + LANES:(hd + 1) * HEAD_W] = (
                (q_rope[:, sl] * cos + q_swap[:, sl] * sin) * scale).astype(BF16)
        z = _dot(h, wdkv_ref[...])
        z_swap = _dot(h, wdkvx_ref[...])
        ckv = _rms(z[:, :KV_LORA_RANK], kvn_ref[...])
        k_rope = z[:, KV_LORA_RANK:] * cos + z_swap * sin
        z_ref[rows, :KV_LORA_RANK] = ckv
        z_ref[rows, KV_LORA_RANK:] = k_rope
        _expand_kv(ckv, k_rope, wk_ref, wv_ref, k_ref, v_ref, rows)


def _expand_kv(ckv, k_rope, wk_ref, wv_ref, k_ref, v_ref, rows=slice(None)):
    ckv = ckv.astype(BF16)
    k_rope = k_rope.astype(BF16)
    k_nope = _dot(ckv, wk_ref[...])
    for hd in range(MLA_HEADS):
        k_ref[rows, hd * HEAD_W:hd * HEAD_W + LANES] = k_nope[:, hd * LANES:(hd + 1) * LANES].astype(BF16)
        k_ref[rows, hd * HEAD_W + LANES:(hd + 1) * HEAD_W] = k_rope
    v_ref[rows, :] = _dot(ckv, wv_ref[...]).astype(BF16)


def _kv_expand_kernel(z_ref, wk_ref, wv_ref, k_ref, v_ref):
    z = z_ref[...]
    _expand_kv(z[:, :KV_LORA_RANK], z[:, KV_LORA_RANK:], wk_ref, wv_ref, k_ref, v_ref)


def _kv_expand(z, wk, wv, n=None):
    zw = z.shape[1]
    n = z.shape[0] if n is None else n
    tm = 2 * SEQ_TILE if n % (2 * SEQ_TILE) == 0 else SEQ_TILE
    return pl.pallas_call(
        _kv_expand_kernel,
        out_shape=(jax.ShapeDtypeStruct((n, MLA_HEADS * HEAD_W), BF16),
                   jax.ShapeDtypeStruct((n, MLA_HEADS * V_HEAD_DIM), BF16)),
        grid=(n // tm,),
        in_specs=[pl.BlockSpec((tm, zw), lambda i: (i, 0)), _resident(wk.shape), _resident(wv.shape)],
        out_specs=(pl.BlockSpec((tm, MLA_HEADS * HEAD_W), lambda i: (i, 0)),
                   pl.BlockSpec((tm, MLA_HEADS * V_HEAD_DIM), lambda i: (i, 0))),
        compiler_params=_params(1),
        name="mla_kv_expand",
    )(z, wk, wv)


def _attn_kernel(*refs, heads, sub, n_kv):
    q_ref, kv_refs, o_ref = refs[0], refs[1:1 + 2 * n_kv], refs[-1]
    tq = q_ref.shape[0]
    for hd in range(heads):
        ks = [r[:, hd * HEAD_W:(hd + 1) * HEAD_W] for r in kv_refs[0::2]]
        vs = [r[:, hd * V_HEAD_DIM:(hd + 1) * V_HEAD_DIM] for r in kv_refs[1::2]]
        for j in range(tq // sub):
            rows = slice(j * sub, (j + 1) * sub)
            q = q_ref[rows, hd * HEAD_W:(hd + 1) * HEAD_W]
            ss = [lax.dot_general(q, k, (((1,), (1,)), ((), ())), preferred_element_type=F32) for k in ks]
            top = functools.reduce(jnp.maximum, [jnp.max(s, axis=-1, keepdims=True) for s in ss])
            ps = [jnp.exp(s - top) for s in ss]
            den = sum(jnp.sum(p, axis=-1, keepdims=True) for p in ps)
            o = sum(_dot(p.astype(BF16), v) for p, v in zip(ps, vs))
            o_ref[rows, hd * V_HEAD_DIM:(hd + 1) * V_HEAD_DIM] = (o / den).astype(BF16)


def _attention(q, kv, *, n_seq, q_len, q_row0, tq, heads):
    hg = MLA_HEADS // heads
    assert q_row0 % tq == 0 and q_len % tq == 0
    qb0, qt = q_row0 // tq, q_len // tq
    in_specs = [pl.BlockSpec((tq, heads * HEAD_W), lambda b, g, j: (qb0 + b * qt + j, g))]
    args = [q]
    for k, v, k_len, row0 in kv:
        assert row0 % k_len == 0
        kb0 = row0 // k_len
        in_specs += [pl.BlockSpec((k_len, heads * HEAD_W), lambda b, g, j, kb0=kb0: (kb0 + b, g)),
                     pl.BlockSpec((k_len, heads * V_HEAD_DIM), lambda b, g, j, kb0=kb0: (kb0 + b, g))]
        args += [k, v]
    return pl.pallas_call(
        functools.partial(_attn_kernel, heads=heads, sub=min(ATTN_SUB_TILE, tq), n_kv=len(kv)),
        out_shape=jax.ShapeDtypeStruct((n_seq * q_len, MLA_HEADS * V_HEAD_DIM), BF16),
        grid=(n_seq, hg, qt),
        in_specs=in_specs,
        out_specs=pl.BlockSpec((tq, heads * V_HEAD_DIM), lambda b, g, j: (b * qt + j, g)),
        compiler_params=_params(3),
        name="mla_attention",
    )(*args)


def _rope_tables(lay, tile):
    pos = np.arange(lay.ss)
    n_freq = QK_ROPE_DIM // 4
    inv_freq = ROPE_THETA ** (-np.arange(n_freq, dtype=np.float64) / n_freq)
    ang = np.concatenate([(pos // GRID_W)[:, None] * inv_freq, (pos % GRID_W)[:, None] * inv_freq], axis=-1)
    pad = np.zeros((lay.ss, LANES - QK_ROPE_DIM))
    cos = np.concatenate([np.cos(ang), np.cos(ang), pad], axis=-1)
    sin = np.concatenate([np.sin(ang), np.sin(ang), pad], axis=-1)
    ident_cos = np.concatenate([np.ones((tile, QK_ROPE_DIM)), np.zeros((tile, LANES - QK_ROPE_DIM))], axis=-1)
    cos = np.concatenate([ident_cos, cos], axis=0)
    sin = np.concatenate([np.zeros((tile, LANES)), sin], axis=0)
    return jnp.asarray(cos, F32), jnp.asarray(sin, F32)


def _swap_halves(w):
    half = QK_ROPE_DIM // 2
    return jnp.concatenate([-w[..., half:], w[..., :half]], axis=-1)


def _mla_layer(lay, x_prompt, x_sample, mod, norm_mix, cache_ckv, cache_krope, w_dq, q_norm, w_uq, w_dkv, kv_norm,
               w_ukv, cast_next=()):
    t, d = lay.t, x_prompt.shape[1]
    tm = MLA_ROW_TILE
    rq = w_dq.shape[1]
    qk = QK_NOPE_DIM + QK_ROPE_DIM
    zw = KV_LORA_RANK + LANES
    rope_pad = [(0, 0)] * 2 + [(0, LANES - QK_ROPE_DIM)]

    wq = w_uq.reshape(rq, MLA_HEADS, qk)
    wq_nope = wq[:, :, :QK_NOPE_DIM].reshape(rq, -1).astype(BF16)
    wq_rope = jnp.pad(wq[:, :, QK_NOPE_DIM:], rope_pad).reshape(rq, -1).astype(BF16)
    wq_swap = jnp.pad(_swap_halves(wq[:, :, QK_NOPE_DIM:]), rope_pad).reshape(rq, -1).astype(BF16)
    wdkv = jnp.pad(w_dkv, [(0, 0), (0, LANES - QK_ROPE_DIM)]).astype(BF16)
    wdkv_swap = jnp.pad(_swap_halves(w_dkv[:, KV_LORA_RANK:]), [(0, 0), (0, LANES - QK_ROPE_DIM)]).astype(BF16)
    wkv = w_ukv.reshape(KV_LORA_RANK, MLA_HEADS, QK_NOPE_DIM + V_HEAD_DIM)
    wk = wkv[:, :, :QK_NOPE_DIM].reshape(KV_LORA_RANK, -1).astype(BF16)
    wv = wkv[:, :, QK_NOPE_DIM:].reshape(KV_LORA_RANK, -1).astype(BF16)

    cos, sin = _rope_tables(lay, tm)
    npt, spb = lay.prompt_tiles(tm), lay.sample_tiles_per_seq(tm)
    rope_spec = pl.BlockSpec((tm, LANES), lambda i: (jnp.where(i < npt, 0, 1 + (i - npt) % spb), 0))
    def rows(width):
        return pl.BlockSpec((tm, width), lambda i: (i, 0))

    casts = [_cast_specs(stack, index, t // tm) for stack, index in cast_next]
    q, z, k_new, v_new, *converted = pl.pallas_call(
        functools.partial(_mla_qkv_kernel, scale=qk ** -0.5, n_prompt_tiles=npt, n_cast=len(casts)),
        out_shape=(jax.ShapeDtypeStruct((t, MLA_HEADS * HEAD_W), BF16), jax.ShapeDtypeStruct((t, zw), F32),
                   jax.ShapeDtypeStruct((t, MLA_HEADS * HEAD_W), BF16),
                   jax.ShapeDtypeStruct((t, MLA_HEADS * V_HEAD_DIM), BF16)) + tuple(c[2] for c in casts),
        grid=(t // tm,),
        in_specs=[pl.BlockSpec((tm, d), lambda i: (jnp.minimum(i, npt - 1), 0)),
                  pl.BlockSpec((tm, d), lambda i: (jnp.maximum(i - npt, 0), 0)), _resident((1, d)),
                  lay.mod_spec(tm, 0, d), lay.mod_spec(tm, 1, d), rope_spec, rope_spec,
                  _resident(w_dq.shape), _resident((1, rq)), _resident(wq_nope.shape), _resident(wq_rope.shape),
                  _resident(wq_swap.shape), _resident(wdkv.shape), _resident(wdkv_swap.shape),
                  _resident((1, KV_LORA_RANK)), _resident(wk.shape), _resident(wv.shape)] + [c[0] for c in casts],
        out_specs=(rows(MLA_HEADS * HEAD_W), rows(zw), rows(MLA_HEADS * HEAD_W), rows(MLA_HEADS * V_HEAD_DIM))
        + tuple(c[1] for c in casts),
        compiler_params=_params(1, "arbitrary"),
        name="mla_qkv",
    )(x_prompt, x_sample, norm_mix, mod, mod, cos, sin, w_dq.astype(BF16), q_norm, wq_nope, wq_rope, wq_swap, wdkv,
      wdkv_swap, kv_norm, wk, wv, *[stack for stack, _ in cast_next])

    past = cache_ckv.shape[1]
    z_cache = jnp.concatenate(
        [cache_ckv, cache_krope, jnp.zeros((lay.bs, past, LANES - QK_ROPE_DIM), F32)], axis=-1)
    k_past, v_past = _kv_expand(z_cache.reshape(lay.bs * past, zw), wk, wv)

    o_prompt = _attention(q, [(k_new, v_new, lay.sp, 0)], n_seq=lay.bp, q_len=lay.sp, q_row0=0, tq=lay.sp,
                          heads=MLA_HEADS)
    o_sample = _attention(q, [(k_past, v_past, past, 0), (k_new, v_new, lay.ss, lay.tp)], n_seq=lay.bs, q_len=lay.ss,
                          q_row0=lay.tp, tq=min(ATTN_Q_TILE, lay.ss), heads=1)
    ckv_state = z[:lay.tp, :KV_LORA_RANK].reshape(lay.bp, 1, lay.sp, KV_LORA_RANK)
    krope_state = z[:lay.tp, KV_LORA_RANK:KV_LORA_RANK + QK_ROPE_DIM].reshape(lay.bp, 1, lay.sp, QK_ROPE_DIM)
    return (o_prompt, o_sample), ckv_state, krope_state, tuple(converted)


def _seq_position(lay, tile, t):
    npt, spb = lay.prompt_tiles(tile), lay.sample_tiles_per_seq(tile)
    ppb = lay.sp // tile
    r = jnp.where(t < npt, t % ppb, (t - npt) % spb)
    n = jnp.where(t < npt, ppb, spb)
    return r, n


def _softplus(x):
    return jnp.maximum(x, 0.0) + jnp.log1p(jnp.exp(-jnp.abs(x)))


def _rglru_kernel(*refs, lay, n_tiles, reverse):
    if reverse:
        (xc_ref, hc_ref, yf_ref, h0_ref, wa_ref, ba_ref, wi_ref, bi_ref, lam_ref, wy_ref,
         o_ref, hl_ref, a_scr, u_scr, perm_scr, carry_scr) = refs
    else:
        (xp_ref, x_ref, xn_ref, nm_ref, sh_ref, sc_ref, h0_ref, wx_ref, cw_ref, cb_ref,
         wa_ref, ba_ref, wi_ref, bi_ref, lam_ref,
         o_ref, xc_ref, hc_ref, hl_ref, xw_scr, a_scr, u_scr, perm_scr, carry_scr) = refs
    tm, d = xc_ref.shape
    groups = tm // SUBLANES
    bw = d // RG_BLOCKS
    i = pl.program_id(0)
    t = n_tiles - 1 - i if reverse else i
    r, n = _seq_position(lay, tm, t)

    @pl.when(i == 0)
    def _():
        carry_scr[...] = jnp.zeros_like(carry_scr)

    if reverse:
        xc = xc_ref[...]
    else:
        has_prev, has_next = r > 0, r < n - 1

        def pre(x):
            return _norm_mod(x, nm_ref[...], sh_ref[...], sc_ref[...]).astype(BF16)

        x_nat = x_ref[...]
        for c in range(d // LANES):
            for s in range(SUBLANES):
                perm_scr[c, pl.ds(s, groups, stride=SUBLANES), :] = x_nat[s * groups:(s + 1) * groups,
                                                                         c * LANES:(c + 1) * LANES]
        hc = pre(jnp.concatenate([perm_scr[c] for c in range(d // LANES)], axis=-1))
        hc_ref[...] = hc
        xw = _dot(jnp.concatenate([pre(xp_ref[...]), hc, pre(xn_ref[...])], axis=0), wx_ref[...])
        sub = lax.broadcasted_iota(jnp.int32, (SUBLANES, d), 0)

        def tile_group(j):
            return xw[RG_HALO + j * SUBLANES:RG_HALO + (j + 1) * SUBLANES]

        def halo_row(k, valid):
            return jnp.broadcast_to(jnp.where(valid, xw[k:k + 1], 0.0), (SUBLANES, d))

        before = jnp.where(sub == 0, halo_row(RG_HALO - 1, has_prev), pltpu.roll(tile_group(groups - 1), 1, 0))
        after = [jnp.where(sub == SUBLANES - 1, halo_row(RG_HALO + tm + j, has_next),
                           pltpu.roll(tile_group(j), SUBLANES - 1, 0)) for j in range(RG_CONV_W - 2)]
        xw_scr[0:SUBLANES] = before
        xw_scr[SUBLANES:SUBLANES + tm] = xw[RG_HALO:RG_HALO + tm]
        for j, grp in enumerate(after):
            xw_scr[(groups + 1 + j) * SUBLANES:(groups + 2 + j) * SUBLANES] = grp
        xc = cb_ref[...]
        for k in range(RG_CONV_W):
            xc = xc + cw_ref[k:k + 1, :] * xw_scr[k * SUBLANES:k * SUBLANES + tm]
        xc_ref[...] = xc

    for nb in range(RG_BLOCKS):
        sl = slice(nb * bw, (nb + 1) * bw)
        xcn = xc[:, sl]
        xcb = xcn.astype(BF16)
        rg = jax.nn.sigmoid(_dot(xcb, wa_ref[nb]) + ba_ref[:, sl])
        ig = jax.nn.sigmoid(_dot(xcb, wi_ref[nb]) + bi_ref[:, sl])
        log_a = -RG_C * rg * _softplus(-lam_ref[:, sl])
        a = jnp.exp(log_a)
        u = jnp.sqrt(-jnp.tanh(log_a) * (a * a + 1.0)) * (ig * xcn)
        a_scr[:, :, sl] = a.reshape(groups, SUBLANES, bw)
        u_scr[:, :, sl] = u.reshape(groups, SUBLANES, bw)

    def local_step(g, carry):
        h, p = carry
        j = groups - 1 - g if reverse else g
        a = a_scr[j]
        h = a * h + u_scr[j]
        p = a * p
        u_scr[j] = h
        a_scr[j] = p
        return h, p

    h_seg, p_seg = lax.fori_loop(0, groups, local_step, (jnp.zeros((SUBLANES, d), F32), jnp.ones((SUBLANES, d), F32)))
    is_start = r == n - 1 if reverse else r == 0
    state = jnp.where(is_start, h0_ref[...], carry_scr[...])
    entering = [None] * SUBLANES
    for s in (range(SUBLANES - 1, -1, -1) if reverse else range(SUBLANES)):
        entering[s] = state
        state = h_seg[s:s + 1] + p_seg[s:s + 1] * state
    carry_scr[...] = state
    hl_ref[...] = state
    y = u_scr[...] + a_scr[...] * jnp.concatenate(entering, axis=0)[None]
    y = y.reshape(tm, d)
    if reverse:
        gate = jax.nn.gelu(_dot(hc_ref[...], wy_ref[...]))
        mixed = (yf_ref[...] + y) * gate
        for c in range(d // LANES):
            perm_scr[c] = mixed[:, c * LANES:(c + 1) * LANES]
        for s in range(SUBLANES):
            for c in range(d // LANES):
                o_ref[s * groups:(s + 1) * groups, c * LANES:(c + 1) * LANES] = (
                    perm_scr[c, pl.ds(s, groups, stride=SUBLANES), :].astype(BF16))
    else:
        o_ref[...] = y


def _rglru_layer(lay, x, mod, norm_mix, state, w_x, w_y, conv_w, conv_b, w_a, b_a, w_i, b_i, lam):
    t, d = x.shape
    tm = SEQ_TILE
    n_tiles = t // tm
    hb = tm // SUBLANES
    halo_blocks, n_halo = tm // RG_HALO, t // RG_HALO
    bw = d // RG_BLOCKS

    def run(reverse, *fwd_out):
        dr = int(reverse)
        order = (lambda i: n_tiles - 1 - i) if reverse else (lambda i: i)
        h0 = jnp.concatenate([jnp.zeros((1, d), F32), state[:, dr], jnp.zeros((SUBLANES - 1 - lay.bs, d), F32)])
        h0_spec = lay.row_spec(tm, order)(d)
        tile = pl.BlockSpec((tm, d), lambda i: (order(i), 0))
        gate_specs = [_resident((RG_BLOCKS, bw, bw)), _resident((1, d)), _resident((RG_BLOCKS, bw, bw)),
                      _resident((1, d)), _resident((1, d))]
        gate_args = [w_a[dr].astype(BF16), b_a[dr:dr + 1], w_i[dr].astype(BF16), b_i[dr:dr + 1], lam[dr:dr + 1]]
        scan_scratch = [pltpu.VMEM((hb, SUBLANES, d), F32), pltpu.VMEM((hb, SUBLANES, d), F32),
                        pltpu.VMEM((d // LANES, tm, LANES), F32)]
        last_spec = pl.BlockSpec((None, 1, d), lambda i: (order(i), 0, 0))
        last_shape = jax.ShapeDtypeStruct((n_tiles, 1, d), F32)
        if reverse:
            xc, hc, y_fwd = fwd_out
            in_specs = [tile, tile, tile, h0_spec] + gate_specs + [_resident((d, d))]
            args = [xc, hc, y_fwd, h0.reshape(SUBLANES, 1, d)] + gate_args + [w_y.astype(BF16)]
            out_shape = (jax.ShapeDtypeStruct((t, d), BF16), last_shape)
            out_specs = (tile, last_spec)
            scratch = list(scan_scratch)
        else:
            in_specs = [pl.BlockSpec((RG_HALO, d), lambda i: (jnp.maximum(i * halo_blocks - 1, 0), 0)),
                        tile,
                        pl.BlockSpec((RG_HALO, d), lambda i: (jnp.minimum((i + 1) * halo_blocks, n_halo - 1), 0)),
                        _resident((1, d)), lay.mod_spec(tm, 0, d), lay.mod_spec(tm, 1, d), h0_spec,
                        _resident((d, d)), _resident((RG_CONV_W, d)), _resident((1, d))] + gate_specs
            args = [x, x, x, norm_mix, mod, mod, h0.reshape(SUBLANES, 1, d), w_x.astype(BF16), conv_w, conv_b] + gate_args
            out_shape = (jax.ShapeDtypeStruct((t, d), F32), jax.ShapeDtypeStruct((t, d), F32),
                         jax.ShapeDtypeStruct((t, d), BF16), last_shape)
            out_specs = (tile, tile, tile, last_spec)
            scratch = [pltpu.VMEM((tm + 2 * RG_HALO, d), F32)] + scan_scratch
        scratch.append(pltpu.VMEM((1, d), F32))
        return pl.pallas_call(
            functools.partial(_rglru_kernel, lay=lay, n_tiles=n_tiles, reverse=reverse),
            out_shape=out_shape,
            grid=(n_tiles,),
            in_specs=in_specs,
            out_specs=out_specs,
            scratch_shapes=scratch,
            compiler_params=_params(1, "arbitrary"),
            name="rglru_bwd" if reverse else "rglru_fwd",
        )(*args)

    y_fwd, xc, hc, last_f = run(False)
    m, last_b = run(True, xc, hc, y_fwd)
    ppb = lay.sp // tm
    tail = last_f[:lay.prompt_tiles(tm)].reshape(lay.bp, ppb, d)[:, ppb - 1]
    head = last_b[:lay.prompt_tiles(tm)].reshape(lay.bp, ppb, d)[:, 0]
    return m, jnp.stack([tail, head], axis=1)[:, None]


def _dft_tables(n, scale):
    jk = np.outer(np.arange(n), np.arange(n)) % n
    ang = 2.0 * np.pi * jk / n
    return np.cos(ang) * scale, np.sin(ang) * scale


def _fnet_channel_table(d):
    gw = d // FNET_GROUPS
    cos, sin = _dft_tables(gw, gw ** -0.5)
    return jnp.asarray(np.concatenate([cos, sin], axis=1), F32).astype(BF16)


def _fnet_position_kernel(wc_ref, ws_ref, zc_ref, zs_ref, o_ref, *, seq_len):
    for i in range(zc_ref.shape[0] // seq_len):
        rows = slice(i * seq_len, (i + 1) * seq_len)
        o_ref[i * wc_ref.shape[0]:(i + 1) * wc_ref.shape[0], :] = (
            _dot(wc_ref[...], zc_ref[rows, :]) - _dot(ws_ref[...], zs_ref[rows, :])).astype(BF16)


def _fnet_position(zc, zs, *, n_seq, seq_len, row0, tm, seqs_per_step=1):
    d = zc.shape[1]
    g = seqs_per_step
    assert row0 % (g * seq_len) == 0 and seq_len % tm == 0 and n_seq % g == 0 and (g == 1 or tm == seq_len)
    cos, sin = _dft_tables(seq_len, seq_len ** -0.5)
    wc, ws = jnp.asarray(cos, F32).astype(BF16), jnp.asarray(sin, F32).astype(BF16)
    sb0, rt = row0 // (g * seq_len), seq_len // tm
    w_spec = pl.BlockSpec((tm, seq_len), lambda b, j: (j, 0))
    z_spec = pl.BlockSpec((g * seq_len, d), lambda b, j: (sb0 + b, 0))
    return pl.pallas_call(
        functools.partial(_fnet_position_kernel, seq_len=seq_len),
        out_shape=jax.ShapeDtypeStruct((n_seq * seq_len, d), BF16),
        grid=(n_seq // g, rt),
        in_specs=[w_spec, w_spec, z_spec, z_spec],
        out_specs=pl.BlockSpec((g * tm, d), lambda b, j: (b * rt + j, 0)),
        compiler_params=_params(2),
        name="fnet_position",
    )(wc, ws, zc, zs)


def _fnet_layer(lay, zc, zs):
    group = FNET_PROMPT_GROUP if lay.bp % FNET_PROMPT_GROUP == 0 else 1
    return (_fnet_position(zc, zs, n_seq=lay.bp, seq_len=lay.sp, row0=0, tm=lay.sp, seqs_per_step=group),
            _fnet_position(zc, zs, n_seq=lay.bs, seq_len=lay.ss, row0=lay.tp, tm=min(FNET_ROW_TILE, lay.ss)))


def _conformer_kernel(xp_ref, xc_ref, xn_ref, nm_ref, sh_ref, sc_ref, w1_ref, b1_ref, dw_ref, db_ref, lg_ref, lb_ref,
                      o_ref, src_scr, acc_scr, *, lay):
    tm, d = xc_ref.shape
    r, n = _seq_position(lay, tm, pl.program_id(0))
    has_prev, has_next = r > 0, r < n - 1

    def pre(ref):
        return _norm_mod(ref[...], nm_ref[...], sh_ref[...], sc_ref[...]).astype(BF16)

    h = jnp.concatenate([pre(xp_ref), pre(xc_ref), pre(xn_ref)], axis=0)
    groups = tm // SUBLANES
    n_before = (CONF_CONV_W - 1) // 2
    n_after = CONF_CONV_W - 1 - n_before
    sub = lax.broadcasted_iota(jnp.int32, (SUBLANES, LANES), 0)

    def group_rows(g):
        return slice(g * SUBLANES, (g + 1) * SUBLANES)

    for j in range(d // CONF_GLU_CHUNK):
        val = slice(j * CONF_GLU_CHUNK, (j + 1) * CONF_GLU_CHUNK)
        gate = slice(d + j * CONF_GLU_CHUNK, d + (j + 1) * CONF_GLU_CHUNK)
        glu = (_dot(h, w1_ref[:, val]) + b1_ref[:, val]) * jax.nn.sigmoid(_dot(h, w1_ref[:, gate]) + b1_ref[:, gate])
        for i in range(CONF_GLU_CHUNK // LANES):
            c = j * CONF_GLU_CHUNK // LANES + i
            col = glu[:, i * LANES:(i + 1) * LANES]
            for s in range(SUBLANES):
                src_scr[c, pl.ds(n_before * SUBLANES + s, groups, stride=SUBLANES), :] = (
                    col[CONF_HALO + s * groups:CONF_HALO + (s + 1) * groups])
            for k in range(1, n_before + 1):
                edge = jnp.where(has_prev, col[CONF_HALO - k:CONF_HALO - k + 1], 0.0)
                inner = pltpu.roll(src_scr[c, group_rows(n_before + groups - k), :], 1, 0)
                src_scr[c, group_rows(n_before - k), :] = jnp.where(sub == 0, edge, inner)
            for k in range(n_after):
                edge = jnp.where(has_next, col[CONF_HALO + tm + k:CONF_HALO + tm + k + 1], 0.0)
                inner = pltpu.roll(src_scr[c, group_rows(n_before + k), :], SUBLANES - 1, 0)
                src_scr[c, group_rows(n_before + groups + k), :] = jnp.where(sub == SUBLANES - 1, edge, inner)
    for c in range(d // LANES):
        lanes = slice(c * LANES, (c + 1) * LANES)
        w = [jnp.broadcast_to(dw_ref[k:k + 1, lanes], (SUBLANES, LANES)) for k in range(CONF_CONV_W)]
        bias = jnp.broadcast_to(db_ref[:, lanes], (SUBLANES, LANES))

        def conv_rows(g, carry, c=c, w=w, bias=bias):
            row0 = pl.multiple_of(g * SUBLANES, SUBLANES)
            part = [bias] + [None] * (CONF_CHAINS - 1)
            for k in range(CONF_CONV_W):
                term = w[k] * src_scr[c, pl.ds(row0 + k * SUBLANES, SUBLANES), :]
                j = k % CONF_CHAINS
                part[j] = term if part[j] is None else part[j] + term
            while len(part) > 1:
                part = [a + b for a, b in zip(part[0::2], part[1::2])]
            acc_scr[c, pl.ds(row0, SUBLANES), :] = part[0]
            return carry

        lax.fori_loop(0, groups, conv_rows, 0, unroll=8)
    acc = jnp.concatenate([acc_scr[c] for c in range(d // LANES)], axis=-1)
    mu = jnp.mean(acc, axis=-1, keepdims=True)
    cen = acc - mu
    var = jnp.mean(cen * cen, axis=-1, keepdims=True)
    y = cen * lax.rsqrt(var + EPS) * lg_ref[...] + lb_ref[...]
    y = y * jax.nn.sigmoid(y)
    for c in range(d // LANES):
        acc_scr[c] = y[:, c * LANES:(c + 1) * LANES]
    for s in range(SUBLANES):
        for c in range(d // LANES):
            o_ref[s * groups:(s + 1) * groups, c * LANES:(c + 1) * LANES] = (
                acc_scr[c, pl.ds(s, groups, stride=SUBLANES), :].astype(BF16))


def _conformer_layer(lay, x, mod, norm_mix, w_pw1, b_pw1, dw_w, dw_b, ln_g, ln_b):
    t, d = x.shape
    tm = SEQ_TILE
    hb = tm // CONF_HALO
    n_halo = t // CONF_HALO
    tile = pl.BlockSpec((tm, d), lambda i: (i, 0))
    return pl.pallas_call(
        functools.partial(_conformer_kernel, lay=lay),
        out_shape=jax.ShapeDtypeStruct((t, d), BF16),
        grid=(t // tm,),
        in_specs=[pl.BlockSpec((CONF_HALO, d), lambda i: (jnp.maximum(i * hb - 1, 0), 0)),
                  tile,
                  pl.BlockSpec((CONF_HALO, d), lambda i: (jnp.minimum((i + 1) * hb, n_halo - 1), 0)),
                  _resident((1, d)), lay.mod_spec(tm, 0, d), lay.mod_spec(tm, 1, d),
                  _resident((d, 2 * d)), _resident((1, 2 * d)), _resident((CONF_CONV_W, d)), _resident((1, d)),
                  _resident((1, d)), _resident((1, d))],
        out_specs=tile,
        scratch_shapes=[pltpu.VMEM((d // LANES, tm + (CONF_CONV_W - 1) * SUBLANES, LANES), F32),
                        pltpu.VMEM((d // LANES, tm, LANES), F32)],
        compiler_params=_params(1),
        name="conformer",
    )(x, x, x, norm_mix, mod, mod, w_pw1.astype(BF16), b_pw1, dw_w, dw_b, ln_g, ln_b)


def kernel(x_prompt, x_sample, cache_mla_ckv, cache_mla_krope, state_rglru, c, c_ctx, ada_w, ada_b, norm_mix, norm_ffn, mla_w_dq, mla_q_norm, mla_w_uq, mla_w_dkv, mla_kv_norm, mla_w_ukv, mla_w_o, rg_w_x, rg_w_y, rg_conv_w, rg_conv_b, rg_w_a, rg_b_a, rg_w_i, rg_b_i, rg_lam, rg_w_o, fn_w_o, fn_b_o, cf_w_pw1, cf_b_pw1, cf_dw_w, cf_dw_b, cf_ln_g, cf_ln_b, cf_w_pw2, cf_b_pw2, ffn_w_gate, ffn_w_up, ffn_w_down, final_norm):
    bp, sp, d = x_prompt.shape
    bs, ss, _ = x_sample.shape
    depth = ada_w.shape[0]
    assert depth == 4 and bs < SUBLANES and sp == SEQ_TILE and ss % GRID_W == 0
    lay = _Layout(bp, sp, bs, ss)
    x_in = (x_prompt.reshape(lay.tp, d), x_sample.reshape(lay.ts, d))

    cvec = jnp.concatenate([c_ctx[None], c, jnp.zeros((SUBLANES - 1 - bs, d), F32)], axis=0)
    mods = _ada_tables(cvec, ada_w, ada_b).reshape(depth, SUBLANES, 1, -1)
    zero_bias = jnp.zeros((1, d), F32)
    ffn_stacks = (ffn_w_gate, ffn_w_up, ffn_w_down)

    def ffn(layer, weights, x, m, w_out, b_out, extra=(), **tail):
        cast_next = [(w, layer + 1) for w in ffn_stacks] if layer + 1 < depth else []
        out, converted = _ffn_layer(lay, x, m, w_out, b_out, mods[layer], norm_ffn[layer:layer + 1], weights,
                                    cast_next=cast_next + [(w, 0) for w in extra], **tail)
        return out, converted[:len(ffn_stacks)], converted[len(ffn_stacks):]

    m, ckv_state, krope_state, converted = _mla_layer(
        lay, *x_in, mods[0], norm_mix[0:1], cache_mla_ckv[:, 0], cache_mla_krope[:, 0], mla_w_dq[0], mla_q_norm[0:1],
        mla_w_uq[0], mla_w_dkv[0], mla_kv_norm[0:1], mla_w_ukv[0],
        cast_next=[(w, 0) for w in ffn_stacks] + [(mla_w_o, 0)])
    weights, w_o = converted[:3], converted[3]
    x, weights, (w_x, w_y, w_o) = ffn(0, weights, x_in, m, w_o, zero_bias, extra=(rg_w_x, rg_w_y, rg_w_o))

    m, rg_state = _rglru_layer(lay, x, mods[1], norm_mix[1:2], state_rglru[:, 0], w_x, w_y, rg_conv_w[0],
                               rg_conv_b[0:1], rg_w_a[0], rg_b_a[0], rg_w_i[0], rg_b_i[0], rg_lam[0])
    (x, zc, zs), weights, (w_o,) = ffn(1, weights, x, m, w_o, zero_bias, extra=(fn_w_o,),
                                       fnet_next=(norm_mix[2:3], mods[2]))

    m = _fnet_layer(lay, zc, zs)
    x, weights, (w_pw1, w_o) = ffn(2, weights, x, m, w_o, fn_b_o[0:1], extra=(cf_w_pw1, cf_w_pw2))

    m = _conformer_layer(lay, x, mods[3], norm_mix[3:4], w_pw1, cf_b_pw1[0:1], cf_dw_w[0], cf_dw_b[0:1],
                         cf_ln_g[0:1], cf_ln_b[0:1])
    (y_prompt, y_sample), _, _ = ffn(3, weights, x, m, w_o, cf_b_pw2[0:1], final_norm=final_norm[None])

    return (y_prompt.reshape(bp, sp, d), y_sample.reshape(bs, ss, d), ckv_state, krope_state, rg_state)
```

```python
import functools

import numpy as np
import jax
import jax.numpy as jnp
from jax import lax
from jax.experimental import pallas as pl
from jax.experimental.pallas import tpu as pltpu

F32 = jnp.float32
BF16 = jnp.bfloat16

EPS = 1e-6
GRID_W = 64
MLA_HEADS = 8
KV_LORA_RANK = 256
QK_NOPE_DIM = 128
QK_ROPE_DIM = 64
V_HEAD_DIM = 128
ROPE_THETA = 10000.0
RG_BLOCKS = 4
RG_CONV_W = 4
RG_C = 8.0
FNET_GROUPS = 4
CONF_CONV_W = 31

LANES = 128
SUBLANES = 8
BF16_SUBLANES = 16
HEAD_W = 2 * LANES
SEQ_TILE = 256
FFN_TILE = 512
MLA_ROW_TILE = 512
ATTN_Q_TILE = 2048
ATTN_SUB_TILE = 256
FNET_ROW_TILE = 512
FNET_PROMPT_GROUP = 4
FFN_CHUNK = 256
CONF_HALO = 16
RG_HALO = 16
CONF_CHAINS = 4
CONF_GLU_CHUNK = 256
VMEM_LIMIT = 52 * 1024 * 1024


def _dot(a, b):
    return jnp.dot(a, b, preferred_element_type=F32)


def _rms(x, g):
    return x * lax.rsqrt(jnp.mean(x * x, axis=-1, keepdims=True) + EPS) * g


def _norm_mod(x, g, shift, scale):
    return _rms(x, g) * (1.0 + scale) + shift


def _resident(shape):
    nd = len(shape)
    return pl.BlockSpec(shape, lambda *_: (0,) * nd, pipeline_mode=pl.Buffered(1))


def _params(n_axes, semantics="parallel"):
    return pltpu.CompilerParams(dimension_semantics=(semantics,) * n_axes, vmem_limit_bytes=VMEM_LIMIT)


class _Layout:
    def __init__(self, n_prompt_seq, prompt_len, n_sample_seq, sample_len):
        self.bp, self.sp, self.bs, self.ss = n_prompt_seq, prompt_len, n_sample_seq, sample_len
        self.tp = n_prompt_seq * prompt_len
        self.ts = n_sample_seq * sample_len
        self.t = self.tp + self.ts

    def prompt_tiles(self, tile):
        assert self.tp % tile == 0 and self.ss % tile == 0
        return self.tp // tile

    def sample_tiles_per_seq(self, tile):
        return self.ss // tile

    def mod_row(self, tile):
        npt, spb = self.prompt_tiles(tile), self.sample_tiles_per_seq(tile)
        return lambda i: jnp.where(i < npt, 0, 1 + (i - npt) // spb)

    def mod_spec(self, tile, chunk, d, order=lambda i: i):
        row = self.mod_row(tile)
        return pl.BlockSpec((None, 1, d), lambda i: (row(order(i)), 0, chunk))

    def row_spec(self, tile, order=lambda i: i):
        row = self.mod_row(tile)
        return lambda d: pl.BlockSpec((None, 1, d), lambda i: (row(order(i)), 0, 0))


def _ada_kernel(c_ref, w_ref, b_ref, o_ref):
    c = c_ref[...]
    s = (c * jax.nn.sigmoid(c)).astype(BF16)
    o_ref[...] = _dot(s, w_ref[...].astype(BF16)) + b_ref[...]


def _ada_tables(cvec, ada_w, ada_b):
    depth, d, n = ada_w.shape
    tn = n // 4
    return pl.pallas_call(
        _ada_kernel,
        out_shape=jax.ShapeDtypeStruct((depth, SUBLANES, n), F32),
        grid=(depth, n // tn),
        in_specs=[
            pl.BlockSpec((SUBLANES, d), lambda l, j: (0, 0)),
            pl.BlockSpec((None, d, tn), lambda l, j: (l, 0, j)),
            pl.BlockSpec((None, 1, tn), lambda l, j: (l, 0, j)),
        ],
        out_specs=pl.BlockSpec((None, SUBLANES, tn), lambda l, j: (l, 0, j)),
        compiler_params=_params(2),
        name="ada_tables",
    )(cvec, ada_w, ada_b.reshape(depth, 1, n))


def _read_tokens(refs, n_prompt_tiles):
    if len(refs) == 1:
        return refs[0][...]
    return jnp.where(pl.program_id(0) < n_prompt_tiles, refs[0][...], refs[1][...])


def _fnet_channel_dft(h, w_ref, zc_ref, zs_ref):
    gw = w_ref.shape[0]
    for g in range(FNET_GROUPS):
        sl = slice(g * gw, (g + 1) * gw)
        f = _dot(h[:, sl], w_ref[...])
        zc_ref[:, sl] = f[:, :gw].astype(BF16)
        zs_ref[:, sl] = f[:, gw:].astype(BF16)


def _ffn_kernel(*refs, n_chunks, n_x, n_m, tail, n_prompt_tiles, n_cast):
    x_refs, m_refs, refs = refs[:n_x], refs[n_x:n_x + n_m], refs[n_x + n_m:]
    wo_ref, bo_ref, g1_ref, sh_ref, sc_ref, g2_ref, nf_ref, wg_ref, wu_ref, wd_ref = refs[:10]
    refs, a_ref = refs[10:-1], refs[-1]
    if n_cast:
        for src, dst in zip(refs[:n_cast], refs[-n_cast:]):
            dst[...] = src[...].astype(BF16)
        refs = refs[n_cast:-n_cast]
    m = _read_tokens(m_refs, n_prompt_tiles)
    x1 = _read_tokens(x_refs, n_prompt_tiles) + g1_ref[...] * (_dot(m, wo_ref[...]) + bo_ref[...])
    h = _norm_mod(x1, nf_ref[...], sh_ref[...], sc_ref[...]).astype(BF16)
    for c in range(n_chunks):
        sl = slice(c * FFN_CHUNK, (c + 1) * FFN_CHUNK)
        g = _dot(h, wg_ref[:, sl])
        u = _dot(h, wu_ref[:, sl])
        a_ref[:, sl] = (g * jax.nn.sigmoid(g) * u).astype(BF16)
    x2 = x1 + g2_ref[...] * _dot(a_ref[...], wd_ref[...])
    if tail is None:
        refs[0][...] = x2
    elif tail == "fnet":
        nm_ref, sh1_ref, sc1_ref, dft_ref, o_ref, zc_ref, zs_ref = refs
        o_ref[...] = x2
        h_next = _norm_mod(x2, nm_ref[...], sh1_ref[...], sc1_ref[...]).astype(BF16)
        _fnet_channel_dft(h_next, dft_ref, zc_ref, zs_ref)
    else:
        fin_ref, op_ref, os_ref = refs
        y = _rms(x2, fin_ref[...])
        is_prompt = pl.program_id(0) < n_prompt_tiles

        @pl.when(is_prompt)
        def _():
            op_ref[...] = y

        @pl.when(jnp.logical_not(is_prompt))
        def _():
            os_ref[...] = y


def _cast_specs(stack, index, n_steps):
    _, rows, width = stack.shape
    span = 1
    while (rows * span) % n_steps or (rows * span // n_steps) % BF16_SUBLANES:
        span *= 2
    assert n_steps % span == 0
    block = rows * span // n_steps
    return (pl.BlockSpec((None, block, width), lambda i: (index, i // span, 0)),
            pl.BlockSpec((block, width), lambda i: (i // span, 0)),
            jax.ShapeDtypeStruct((rows, width), BF16))


def _ffn_layer(lay, x, m, w_out, b_out, mod, norm_ffn, weights, final_norm=None, fnet_next=None, cast_next=None):
    xs = x if isinstance(x, tuple) else (x,)
    ms = m if isinstance(m, tuple) else (m,)
    d = xs[0].shape[1]
    w_gate, w_up, w_down = weights
    dff = w_gate.shape[1]
    assert dff % FFN_CHUNK == 0
    tm = FFN_TILE
    n_steps = lay.t // tm
    npt = lay.prompt_tiles(tm)
    tile = pl.BlockSpec((tm, d), lambda i: (i, 0))
    split = [pl.BlockSpec((tm, d), lambda i: (jnp.minimum(i, npt - 1), 0)),
             pl.BlockSpec((tm, d), lambda i: (jnp.maximum(i - npt, 0), 0))]
    in_specs = (split if len(xs) == 2 else [tile]) + (split if len(ms) == 2 else [tile]) + [
        _resident((d, d)), _resident((1, d)),
        lay.mod_spec(tm, 2, d), lay.mod_spec(tm, 3, d), lay.mod_spec(tm, 4, d), lay.mod_spec(tm, 5, d),
        _resident((1, d)), _resident((d, dff)), _resident((d, dff)), _resident((dff, d))]
    args = list(xs) + list(ms) + [w_out, b_out, mod, mod, mod, mod, norm_ffn, w_gate, w_up, w_down]
    cast_out_specs, cast_out_shapes = [], []
    for stack, index in cast_next or ():
        src_spec, dst_spec, dst_shape = _cast_specs(stack, index, n_steps)
        in_specs.append(src_spec)
        args.append(stack)
        cast_out_specs.append(dst_spec)
        cast_out_shapes.append(dst_shape)
    out_shapes, out_specs, tail = [jax.ShapeDtypeStruct((lay.t, d), F32)], [tile], None
    if final_norm is not None:
        tail = "final"
        in_specs.append(_resident((1, d)))
        args.append(final_norm)
        out_shapes = [jax.ShapeDtypeStruct((lay.tp, d), F32), jax.ShapeDtypeStruct((lay.ts, d), F32)]
        out_specs = list(split)
    elif fnet_next is not None:
        tail = "fnet"
        norm_next, mod_next = fnet_next
        dft = _fnet_channel_table(d)
        in_specs += [_resident((1, d)), lay.mod_spec(tm, 0, d), lay.mod_spec(tm, 1, d), _resident(dft.shape)]
        args += [norm_next, mod_next, mod_next, dft]
        out_shapes += [jax.ShapeDtypeStruct((lay.t, d), BF16), jax.ShapeDtypeStruct((lay.t, d), BF16)]
        out_specs += [tile, tile]
    n_main = len(out_shapes)
    outs = pl.pallas_call(
        functools.partial(_ffn_kernel, n_chunks=dff // FFN_CHUNK, n_x=len(xs), n_m=len(ms), tail=tail,
                          n_prompt_tiles=npt, n_cast=len(cast_out_specs)),
        out_shape=tuple(out_shapes + cast_out_shapes),
        grid=(n_steps,),
        in_specs=in_specs,
        out_specs=tuple(out_specs + cast_out_specs),
        scratch_shapes=[pltpu.VMEM((tm, dff), BF16)],
        compiler_params=_params(1, "arbitrary"),
        name="ffn",
    )(*args)
    main = outs[0] if n_main == 1 else tuple(outs[:n_main])
    return main, tuple(outs[n_main:])


def _mla_qkv_kernel(*refs, scale, n_prompt_tiles, n_cast):
    (xp_ref, xs_ref, nm_ref, sh_ref, sc_ref, cos_ref, sin_ref, wdq_ref, qn_ref, wqn_ref, wqr_ref, wqx_ref,
     wdkv_ref, wdkvx_ref, kvn_ref, wk_ref, wv_ref) = refs[:17]
    q_ref, z_ref, k_ref, v_ref = refs[17 + n_cast:21 + n_cast]
    for src, dst in zip(refs[17:17 + n_cast], refs[21 + n_cast:]):
        dst[...] = src[...].astype(BF16)
    x_all = _read_tokens((xp_ref, xs_ref), n_prompt_tiles)
    for j in range(x_all.shape[0] // SEQ_TILE):
        rows = slice(j * SEQ_TILE, (j + 1) * SEQ_TILE)
        h = _norm_mod(x_all[rows], nm_ref[...], sh_ref[...], sc_ref[...]).astype(BF16)
        cos, sin = cos_ref[rows, :], sin_ref[rows, :]
        cq = _rms(_dot(h, wdq_ref[...]), qn_ref[...]).astype(BF16)
        q_nope = _dot(cq, wqn_ref[...])
        q_rope = _dot(cq, wqr_ref[...])
        q_swap = _dot(cq, wqx_ref[...])
        for hd in range(MLA_HEADS):
            sl = slice(hd * LANES, (hd + 1) * LANES)
            q_ref[rows, hd * HEAD_W:hd * HEAD_W + LANES] = (q_nope[:, sl] * scale).astype(BF16)
            q_ref[rows, hd * HEAD_W + LANES:(hd + 1) * HEAD_W] = (
                (q_rope[:, sl] * cos + q_swap[:, sl] * sin) * scale).astype(BF16)
        z = _dot(h, wdkv_ref[...])
        z_swap = _dot(h, wdkvx_ref[...])
        ckv = _rms(z[:, :KV_LORA_RANK], kvn_ref[...])
        k_rope = z[:, KV_LORA_RANK:] * cos + z_swap * sin
        z_ref[rows, :KV_LORA_RANK] = ckv
        z_ref[rows, KV_LORA_RANK:] = k_rope
        _expand_kv(ckv, k_rope, wk_ref, wv_ref, k_ref, v_ref, rows)


def _expand_kv(ckv, k_rope, wk_ref, wv_ref, k_ref, v_ref, rows=slice(None)):
    ckv = ckv.astype(BF16)
    k_rope = k_rope.astype(BF16)
    k_nope = _dot(ckv, wk_ref[...])
    for hd in range(MLA_HEADS):
        k_ref[rows, hd * HEAD_W:hd * HEAD_W + LANES] = k_nope[:, hd * LANES:(hd + 1) * LANES].astype(BF16)
        k_ref[rows, hd * HEAD_W + LANES:(hd + 1) * HEAD_W] = k_rope
    v_ref[rows, :] = _dot(ckv, wv_ref[...]).astype(BF16)


def _kv_expand_kernel(z_ref, wk_ref, wv_ref, k_ref, v_ref):
    z = z_ref[...]
    _expand_kv(z[:, :KV_LORA_RANK], z[:, KV_LORA_RANK:], wk_ref, wv_ref, k_ref, v_ref)


def _kv_expand(z, wk, wv, n=None):
    zw = z.shape[1]
    n = z.shape[0] if n is None else n
    tm = 2 * SEQ_TILE if n % (2 * SEQ_TILE) == 0 else SEQ_TILE
    return pl.pallas_call(
        _kv_expand_kernel,
        out_shape=(jax.ShapeDtypeStruct((n, MLA_HEADS * HEAD_W), BF16),
                   jax.ShapeDtypeStruct((n, MLA_HEADS * V_HEAD_DIM), BF16)),
        grid=(n // tm,),
        in_specs=[pl.BlockSpec((tm, zw), lambda i: (i, 0)), _resident(wk.shape), _resident(wv.shape)],
        out_specs=(pl.BlockSpec((tm, MLA_HEADS * HEAD_W), lambda i: (i, 0)),
                   pl.BlockSpec((tm, MLA_HEADS * V_HEAD_DIM), lambda i: (i, 0))),
        compiler_params=_params(1),
        name="mla_kv_expand",
    )(z, wk, wv)


def _attn_kernel(*refs, heads, sub, n_kv):
    q_ref, kv_refs, o_ref = refs[0], refs[1:1 + 2 * n_kv], refs[-1]
    tq = q_ref.shape[0]
    for hd in range(heads):
        ks = [r[:, hd * HEAD_W:(hd + 1) * HEAD_W] for r in kv_refs[0::2]]
        vs = [r[:, hd * V_HEAD_DIM:(hd + 1) * V_HEAD_DIM] for r in kv_refs[1::2]]
        vs = [jnp.concatenate([v, jnp.ones_like(v)], axis=-1) for v in vs]
        for j in range(tq // sub):
            rows = slice(j * sub, (j + 1) * sub)
            q = q_ref[rows, hd * HEAD_W:(hd + 1) * HEAD_W]
            ss = [lax.dot_general(q, k, (((1,), (1,)), ((), ())), preferred_element_type=F32) for k in ks]
            top = functools.reduce(jnp.maximum, [jnp.max(s, axis=-1, keepdims=True) for s in ss])
            ps = [jnp.exp(s - top) for s in ss]
            acc = sum(_dot(p.astype(BF16), v) for p, v in zip(ps, vs))
            o_ref[rows, hd * V_HEAD_DIM:(hd + 1) * V_HEAD_DIM] = (
                acc[:, :V_HEAD_DIM] / acc[:, V_HEAD_DIM:V_HEAD_DIM + 1]).astype(BF16)


def _attention(q, kv, *, n_seq, q_len, q_row0, tq, heads):
    hg = MLA_HEADS // heads
    assert q_row0 % tq == 0 and q_len % tq == 0
    qb0, qt = q_row0 // tq, q_len // tq
    in_specs = [pl.BlockSpec((tq, heads * HEAD_W), lambda b, g, j: (qb0 + b * qt + j, g))]
    args = [q]
    for k, v, k_len, row0 in kv:
        assert row0 % k_len == 0
        kb0 = row0 // k_len
        in_specs += [pl.BlockSpec((k_len, heads * HEAD_W), lambda b, g, j, kb0=kb0: (kb0 + b, g)),
                     pl.BlockSpec((k_len, heads * V_HEAD_DIM), lambda b, g, j, kb0=kb0: (kb0 + b, g))]
        args += [k, v]
    return pl.pallas_call(
        functools.partial(_attn_kernel, heads=heads, sub=min(ATTN_SUB_TILE, tq), n_kv=len(kv)),
        out_shape=jax.ShapeDtypeStruct((n_seq * q_len, MLA_HEADS * V_HEAD_DIM), BF16),
        grid=(n_seq, hg, qt),
        in_specs=in_specs,
        out_specs=pl.BlockSpec((tq, heads * V_HEAD_DIM), lambda b, g, j: (b * qt + j, g)),
        compiler_params=_params(3),
        name="mla_attention",
    )(*args)


def _rope_tables(lay, tile):
    pos = np.arange(lay.ss)
    n_freq = QK_ROPE_DIM // 4
    inv_freq = ROPE_THETA ** (-np.arange(n_freq, dtype=np.float64) / n_freq)
    ang = np.concatenate([(pos // GRID_W)[:, None] * inv_freq, (pos % GRID_W)[:, None] * inv_freq], axis=-1)
    pad = np.zeros((lay.ss, LANES - QK_ROPE_DIM))
    cos = np.concatenate([np.cos(ang), np.cos(ang), pad], axis=-1)
    sin = np.concatenate([np.sin(ang), np.sin(ang), pad], axis=-1)
    ident_cos = np.concatenate([np.ones((tile, QK_ROPE_DIM)), np.zeros((tile, LANES - QK_ROPE_DIM))], axis=-1)
    cos = np.concatenate([ident_cos, cos], axis=0)
    sin = np.concatenate([np.zeros((tile, LANES)), sin], axis=0)
    return jnp.asarray(cos, F32), jnp.asarray(sin, F32)


def _swap_halves(w):
    half = QK_ROPE_DIM // 2
    return jnp.concatenate([-w[..., half:], w[..., :half]], axis=-1)


def _mla_layer(lay, x_prompt, x_sample, mod, norm_mix, cache_ckv, cache_krope, w_dq, q_norm, w_uq, w_dkv, kv_norm,
               w_ukv, cast_next=()):
    t, d = lay.t, x_prompt.shape[1]
    tm = MLA_ROW_TILE
    rq = w_dq.shape[1]
    qk = QK_NOPE_DIM + QK_ROPE_DIM
    zw = KV_LORA_RANK + LANES
    rope_pad = [(0, 0)] * 2 + [(0, LANES - QK_ROPE_DIM)]

    wq = w_uq.reshape(rq, MLA_HEADS, qk)
    wq_nope = wq[:, :, :QK_NOPE_DIM].reshape(rq, -1).astype(BF16)
    wq_rope = jnp.pad(wq[:, :, QK_NOPE_DIM:], rope_pad).reshape(rq, -1).astype(BF16)
    wq_swap = jnp.pad(_swap_halves(wq[:, :, QK_NOPE_DIM:]), rope_pad).reshape(rq, -1).astype(BF16)
    wdkv = jnp.pad(w_dkv, [(0, 0), (0, LANES - QK_ROPE_DIM)]).astype(BF16)
    wdkv_swap = jnp.pad(_swap_halves(w_dkv[:, KV_LORA_RANK:]), [(0, 0), (0, LANES - QK_ROPE_DIM)]).astype(BF16)
    wkv = w_ukv.reshape(KV_LORA_RANK, MLA_HEADS, QK_NOPE_DIM + V_HEAD_DIM)
    wk = wkv[:, :, :QK_NOPE_DIM].reshape(KV_LORA_RANK, -1).astype(BF16)
    wv = wkv[:, :, QK_NOPE_DIM:].reshape(KV_LORA_RANK, -1).astype(BF16)

    cos, sin = _rope_tables(lay, tm)
    npt, spb = lay.prompt_tiles(tm), lay.sample_tiles_per_seq(tm)
    rope_spec = pl.BlockSpec((tm, LANES), lambda i: (jnp.where(i < npt, 0, 1 + (i - npt) % spb), 0))
    def rows(width):
        return pl.BlockSpec((tm, width), lambda i: (i, 0))

    casts = [_cast_specs(stack, index, t // tm) for stack, index in cast_next]
    q, z, k_new, v_new, *converted = pl.pallas_call(
        functools.partial(_mla_qkv_kernel, scale=qk ** -0.5, n_prompt_tiles=npt, n_cast=len(casts)),
        out_shape=(jax.ShapeDtypeStruct((t, MLA_HEADS * HEAD_W), BF16), jax.ShapeDtypeStruct((t, zw), F32),
                   jax.ShapeDtypeStruct((t, MLA_HEADS * HEAD_W), BF16),
                   jax.ShapeDtypeStruct((t, MLA_HEADS * V_HEAD_DIM), BF16)) + tuple(c[2] for c in casts),
        grid=(t // tm,),
        in_specs=[pl.BlockSpec((tm, d), lambda i: (jnp.minimum(i, npt - 1), 0)),
                  pl.BlockSpec((tm, d), lambda i: (jnp.maximum(i - npt, 0), 0)), _resident((1, d)),
                  lay.mod_spec(tm, 0, d), lay.mod_spec(tm, 1, d), rope_spec, rope_spec,
                  _resident(w_dq.shape), _resident((1, rq)), _resident(wq_nope.shape), _resident(wq_rope.shape),
                  _resident(wq_swap.shape), _resident(wdkv.shape), _resident(wdkv_swap.shape),
                  _resident((1, KV_LORA_RANK)), _resident(wk.shape), _resident(wv.shape)] + [c[0] for c in casts],
        out_specs=(rows(MLA_HEADS * HEAD_W), rows(zw), rows(MLA_HEADS * HEAD_W), rows(MLA_HEADS * V_HEAD_DIM))
        + tuple(c[1] for c in casts),
        compiler_params=_params(1, "arbitrary"),
        name="mla_qkv",
    )(x_prompt, x_sample, norm_mix, mod, mod, cos, sin, w_dq.astype(BF16), q_norm, wq_nope, wq_rope, wq_swap, wdkv,
      wdkv_swap, kv_norm, wk, wv, *[stack for stack, _ in cast_next])

    past = cache_ckv.shape[1]
    z_cache = jnp.concatenate(
        [cache_ckv, cache_krope, jnp.zeros((lay.bs, past, LANES - QK_ROPE_DIM), F32)], axis=-1)
    k_past, v_past = _kv_expand(z_cache.reshape(lay.bs * past, zw), wk, wv)

    o_prompt = _attention(q, [(k_new, v_new, lay.sp, 0)], n_seq=lay.bp, q_len=lay.sp, q_row0=0, tq=lay.sp,
                          heads=MLA_HEADS)
    o_sample = _attention(q, [(k_past, v_past, past, 0), (k_new, v_new, lay.ss, lay.tp)], n_seq=lay.bs, q_len=lay.ss,
                          q_row0=lay.tp, tq=min(ATTN_Q_TILE, lay.ss), heads=1)
    ckv_state = z[:lay.tp, :KV_LORA_RANK].reshape(lay.bp, 1, lay.sp, KV_LORA_RANK)
    krope_state = z[:lay.tp, KV_LORA_RANK:KV_LORA_RANK + QK_ROPE_DIM].reshape(lay.bp, 1, lay.sp, QK_ROPE_DIM)
    return (o_prompt, o_sample), ckv_state, krope_state, tuple(converted)


def _seq_position(lay, tile, t):
    npt, spb = lay.prompt_tiles(tile), lay.sample_tiles_per_seq(tile)
    ppb = lay.sp // tile
    r = jnp.where(t < npt, t % ppb, (t - npt) % spb)
    n = jnp.where(t < npt, ppb, spb)
    return r, n


def _softplus(x):
    return jnp.maximum(x, 0.0) + jnp.log1p(jnp.exp(-jnp.abs(x)))


def _rglru_kernel(*refs, lay, n_tiles, reverse):
    if reverse:
        (xc_ref, hc_ref, yf_ref, h0_ref, wa_ref, ba_ref, wi_ref, bi_ref, lam_ref, wy_ref,
         o_ref, hl_ref, a_scr, u_scr, perm_scr, carry_scr) = refs
    else:
        (xp_ref, x_ref, xn_ref, nm_ref, sh_ref, sc_ref, h0_ref, wx_ref, cw_ref, cb_ref,
         wa_ref, ba_ref, wi_ref, bi_ref, lam_ref,
         o_ref, xc_ref, hc_ref, hl_ref, xw_scr, a_scr, u_scr, perm_scr, carry_scr) = refs
    tm, d = xc_ref.shape
    groups = tm // SUBLANES
    bw = d // RG_BLOCKS
    i = pl.program_id(0)
    t = n_tiles - 1 - i if reverse else i
    r, n = _seq_position(lay, tm, t)

    @pl.when(i == 0)
    def _():
        carry_scr[...] = jnp.zeros_like(carry_scr)

    if reverse:
        xc = xc_ref[...]
    else:
        has_prev, has_next = r > 0, r < n - 1

        def pre(x):
            return _norm_mod(x, nm_ref[...], sh_ref[...], sc_ref[...]).astype(BF16)

        x_nat = x_ref[...]
        for c in range(d // LANES):
            for s in range(SUBLANES):
                perm_scr[c, pl.ds(s, groups, stride=SUBLANES), :] = x_nat[s * groups:(s + 1) * groups,
                                                                         c * LANES:(c + 1) * LANES]
        hc = pre(jnp.concatenate([perm_scr[c] for c in range(d // LANES)], axis=-1))
        hc_ref[...] = hc
        xw = _dot(jnp.concatenate([pre(xp_ref[...]), hc, pre(xn_ref[...])], axis=0), wx_ref[...])
        sub = lax.broadcasted_iota(jnp.int32, (SUBLANES, d), 0)

        def tile_group(j):
            return xw[RG_HALO + j * SUBLANES:RG_HALO + (j + 1) * SUBLANES]

        def halo_row(k, valid):
            return jnp.broadcast_to(jnp.where(valid, xw[k:k + 1], 0.0), (SUBLANES, d))

        before = jnp.where(sub == 0, halo_row(RG_HALO - 1, has_prev), pltpu.roll(tile_group(groups - 1), 1, 0))
        after = [jnp.where(sub == SUBLANES - 1, halo_row(RG_HALO + tm + j, has_next),
                           pltpu.roll(tile_group(j), SUBLANES - 1, 0)) for j in range(RG_CONV_W - 2)]
        xw_scr[0:SUBLANES] = before
        xw_scr[SUBLANES:SUBLANES + tm] = xw[RG_HALO:RG_HALO + tm]
        for j, grp in enumerate(after):
            xw_scr[(groups + 1 + j) * SUBLANES:(groups + 2 + j) * SUBLANES] = grp
        xc = cb_ref[...]
        for k in range(RG_CONV_W):
            xc = xc + cw_ref[k:k + 1, :] * xw_scr[k * SUBLANES:k * SUBLANES + tm]
        xc_ref[...] = xc

    for nb in range(RG_BLOCKS):
        sl = slice(nb * bw, (nb + 1) * bw)
        xcn = xc[:, sl]
        xcb = xcn.astype(BF16)
        rg = jax.nn.sigmoid(_dot(xcb, wa_ref[nb]) + ba_ref[:, sl])
        ig = jax.nn.sigmoid(_dot(xcb, wi_ref[nb]) + bi_ref[:, sl])
        log_a = -RG_C * rg * _softplus(-lam_ref[:, sl])
        a = jnp.exp(log_a)
        u = jnp.sqrt(-jnp.tanh(log_a) * (a * a + 1.0)) * (ig * xcn)
        a_scr[:, :, sl] = a.reshape(groups, SUBLANES, bw)
        u_scr[:, :, sl] = u.reshape(groups, SUBLANES, bw)

    def local_step(g, carry):
        h, p = carry
        j = groups - 1 - g if reverse else g
        a = a_scr[j]
        h = a * h + u_scr[j]
        p = a * p
        u_scr[j] = h
        a_scr[j] = p
        return h, p

    h_seg, p_seg = lax.fori_loop(0, groups, local_step, (jnp.zeros((SUBLANES, d), F32), jnp.ones((SUBLANES, d), F32)))
    is_start = r == n - 1 if reverse else r == 0
    state = jnp.where(is_start, h0_ref[...], carry_scr[...])
    entering = [None] * SUBLANES
    for s in (range(SUBLANES - 1, -1, -1) if reverse else range(SUBLANES)):
        entering[s] = state
        state = h_seg[s:s + 1] + p_seg[s:s + 1] * state
    carry_scr[...] = state
    hl_ref[...] = state
    y = u_scr[...] + a_scr[...] * jnp.concatenate(entering, axis=0)[None]
    y = y.reshape(tm, d)
    if reverse:
        gate = jax.nn.gelu(_dot(hc_ref[...], wy_ref[...]))
        mixed = (yf_ref[...] + y) * gate
        for c in range(d // LANES):
            perm_scr[c] = mixed[:, c * LANES:(c + 1) * LANES]
        for s in range(SUBLANES):
            for c in range(d // LANES):
                o_ref[s * groups:(s + 1) * groups, c * LANES:(c + 1) * LANES] = (
                    perm_scr[c, pl.ds(s, groups, stride=SUBLANES), :].astype(BF16))
    else:
        o_ref[...] = y


def _rglru_layer(lay, x, mod, norm_mix, state, w_x, w_y, conv_w, conv_b, w_a, b_a, w_i, b_i, lam):
    t, d = x.shape
    tm = SEQ_TILE
    n_tiles = t // tm
    hb = tm // SUBLANES
    halo_blocks, n_halo = tm // RG_HALO, t // RG_HALO
    bw = d // RG_BLOCKS

    def run(reverse, *fwd_out):
        dr = int(reverse)
        order = (lambda i: n_tiles - 1 - i) if reverse else (lambda i: i)
        h0 = jnp.concatenate([jnp.zeros((1, d), F32), state[:, dr], jnp.zeros((SUBLANES - 1 - lay.bs, d), F32)])
        h0_spec = lay.row_spec(tm, order)(d)
        tile = pl.BlockSpec((tm, d), lambda i: (order(i), 0))
        gate_specs = [_resident((RG_BLOCKS, bw, bw)), _resident((1, d)), _resident((RG_BLOCKS, bw, bw)),
                      _resident((1, d)), _resident((1, d))]
        gate_args = [w_a[dr].astype(BF16), b_a[dr:dr + 1], w_i[dr].astype(BF16), b_i[dr:dr + 1], lam[dr:dr + 1]]
        scan_scratch = [pltpu.VMEM((hb, SUBLANES, d), F32), pltpu.VMEM((hb, SUBLANES, d), F32),
                        pltpu.VMEM((d // LANES, tm, LANES), F32)]
        last_spec = pl.BlockSpec((None, 1, d), lambda i: (order(i), 0, 0))
        last_shape = jax.ShapeDtypeStruct((n_tiles, 1, d), F32)
        if reverse:
            xc, hc, y_fwd = fwd_out
            in_specs = [tile, tile, tile, h0_spec] + gate_specs + [_resident((d, d))]
            args = [xc, hc, y_fwd, h0.reshape(SUBLANES, 1, d)] + gate_args + [w_y.astype(BF16)]
            out_shape = (jax.ShapeDtypeStruct((t, d), BF16), last_shape)
            out_specs = (tile, last_spec)
            scratch = list(scan_scratch)
        else:
            in_specs = [pl.BlockSpec((RG_HALO, d), lambda i: (jnp.maximum(i * halo_blocks - 1, 0), 0)),
                        tile,
                        pl.BlockSpec((RG_HALO, d), lambda i: (jnp.minimum((i + 1) * halo_blocks, n_halo - 1), 0)),
                        _resident((1, d)), lay.mod_spec(tm, 0, d), lay.mod_spec(tm, 1, d), h0_spec,
                        _resident((d, d)), _resident((RG_CONV_W, d)), _resident((1, d))] + gate_specs
            args = [x, x, x, norm_mix, mod, mod, h0.reshape(SUBLANES, 1, d), w_x.astype(BF16), conv_w, conv_b] + gate_args
            out_shape = (jax.ShapeDtypeStruct((t, d), F32), jax.ShapeDtypeStruct((t, d), F32),
                         jax.ShapeDtypeStruct((t, d), BF16), last_shape)
            out_specs = (tile, tile, tile, last_spec)
            scratch = [pltpu.VMEM((tm + 2 * RG_HALO, d), F32)] + scan_scratch
        scratch.append(pltpu.VMEM((1, d), F32))
        return pl.pallas_call(
            functools.partial(_rglru_kernel, lay=lay, n_tiles=n_tiles, reverse=reverse),
            out_shape=out_shape,
            grid=(n_tiles,),
            in_specs=in_specs,
            out_specs=out_specs,
            scratch_shapes=scratch,
            compiler_params=_params(1, "arbitrary"),
            name="rglru_bwd" if reverse else "rglru_fwd",
        )(*args)

    y_fwd, xc, hc, last_f = run(False)
    m, last_b = run(True, xc, hc, y_fwd)
    ppb = lay.sp // tm
    tail = last_f[:lay.prompt_tiles(tm)].reshape(lay.bp, ppb, d)[:, ppb - 1]
    head = last_b[:lay.prompt_tiles(tm)].reshape(lay.bp, ppb, d)[:, 0]
    return m, jnp.stack([tail, head], axis=1)[:, None]


def _dft_tables(n, scale):
    jk = np.outer(np.arange(n), np.arange(n)) % n
    ang = 2.0 * np.pi * jk / n
    return np.cos(ang) * scale, np.sin(ang) * scale


def _fnet_channel_table(d):
    gw = d // FNET_GROUPS
    cos, sin = _dft_tables(gw, gw ** -0.5)
    return jnp.asarray(np.concatenate([cos, sin], axis=1), F32).astype(BF16)


def _fnet_position_kernel(wc_ref, ws_ref, zc_ref, zs_ref, o_ref, *, seq_len):
    for i in range(zc_ref.shape[0] // seq_len):
        rows = slice(i * seq_len, (i + 1) * seq_len)
        o_ref[i * wc_ref.shape[0]:(i + 1) * wc_ref.shape[0], :] = (
            _dot(wc_ref[...], zc_ref[rows, :]) - _dot(ws_ref[...], zs_ref[rows, :])).astype(BF16)


def _fnet_position(zc, zs, *, n_seq, seq_len, row0, tm, seqs_per_step=1):
    d = zc.shape[1]
    g = seqs_per_step
    assert row0 % (g * seq_len) == 0 and seq_len % tm == 0 and n_seq % g == 0 and (g == 1 or tm == seq_len)
    cos, sin = _dft_tables(seq_len, seq_len ** -0.5)
    wc, ws = jnp.asarray(cos, F32).astype(BF16), jnp.asarray(sin, F32).astype(BF16)
    sb0, rt = row0 // (g * seq_len), seq_len // tm
    w_spec = pl.BlockSpec((tm, seq_len), lambda b, j: (j, 0))
    z_spec = pl.BlockSpec((g * seq_len, d), lambda b, j: (sb0 + b, 0))
    return pl.pallas_call(
        functools.partial(_fnet_position_kernel, seq_len=seq_len),
        out_shape=jax.ShapeDtypeStruct((n_seq * seq_len, d), BF16),
        grid=(n_seq // g, rt),
        in_specs=[w_spec, w_spec, z_spec, z_spec],
        out_specs=pl.BlockSpec((g * tm, d), lambda b, j: (b * rt + j, 0)),
        compiler_params=_params(2),
        name="fnet_position",
    )(wc, ws, zc, zs)


def _fnet_layer(lay, zc, zs):
    group = FNET_PROMPT_GROUP if lay.bp % FNET_PROMPT_GROUP == 0 else 1
    return (_fnet_position(zc, zs, n_seq=lay.bp, seq_len=lay.sp, row0=0, tm=lay.sp, seqs_per_step=group),
            _fnet_position(zc, zs, n_seq=lay.bs, seq_len=lay.ss, row0=lay.tp, tm=min(FNET_ROW_TILE, lay.ss)))


def _conformer_kernel(xp_ref, xc_ref, xn_ref, nm_ref, sh_ref, sc_ref, w1_ref, b1_ref, dw_ref, db_ref, lg_ref, lb_ref,
                      o_ref, src_scr, acc_scr, *, lay):
    tm, d = xc_ref.shape
    r, n = _seq_position(lay, tm, pl.program_id(0))
    has_prev, has_next = r > 0, r < n - 1

    def pre(ref):
        return _norm_mod(ref[...], nm_ref[...], sh_ref[...], sc_ref[...]).astype(BF16)

    h = jnp.concatenate([pre(xp_ref), pre(xc_ref), pre(xn_ref)], axis=0)
    groups = tm // SUBLANES
    n_before = (CONF_CONV_W - 1) // 2
    n_after = CONF_CONV_W - 1 - n_before
    sub = lax.broadcasted_iota(jnp.int32, (SUBLANES, LANES), 0)

    def group_rows(g):
        return slice(g * SUBLANES, (g + 1) * SUBLANES)

    for j in range(d // CONF_GLU_CHUNK):
        val = slice(j * CONF_GLU_CHUNK, (j + 1) * CONF_GLU_CHUNK)
        gate = slice(d + j * CONF_GLU_CHUNK, d + (j + 1) * CONF_GLU_CHUNK)
        glu = (_dot(h, w1_ref[:, val]) + b1_ref[:, val]) * jax.nn.sigmoid(_dot(h, w1_ref[:, gate]) + b1_ref[:, gate])
        for i in range(CONF_GLU_CHUNK // LANES):
            c = j * CONF_GLU_CHUNK // LANES + i
            col = glu[:, i * LANES:(i + 1) * LANES]
            for s in range(SUBLANES):
                src_scr[c, pl.ds(n_before * SUBLANES + s, groups, stride=SUBLANES), :] = (
                    col[CONF_HALO + s * groups:CONF_HALO + (s + 1) * groups])
            for k in range(1, n_before + 1):
                edge = jnp.where(has_prev, col[CONF_HALO - k:CONF_HALO - k + 1], 0.0)
                inner = pltpu.roll(src_scr[c, group_rows(n_before + groups - k), :], 1, 0)
                src_scr[c, group_rows(n_before - k), :] = jnp.where(sub == 0, edge, inner)
            for k in range(n_after):
                edge = jnp.where(has_next, col[CONF_HALO + tm + k:CONF_HALO + tm + k + 1], 0.0)
                inner = pltpu.roll(src_scr[c, group_rows(n_before + k), :], SUBLANES - 1, 0)
                src_scr[c, group_rows(n_before + groups + k), :] = jnp.where(sub == SUBLANES - 1, edge, inner)
    for c in range(d // LANES):
        lanes = slice(c * LANES, (c + 1) * LANES)
        w = [jnp.broadcast_to(dw_ref[k:k + 1, lanes], (SUBLANES, LANES)) for k in range(CONF_CONV_W)]
        bias = jnp.broadcast_to(db_ref[:, lanes], (SUBLANES, LANES))

        def conv_rows(g, carry, c=c, w=w, bias=bias):
            row0 = pl.multiple_of(g * SUBLANES, SUBLANES)
            part = [bias] + [None] * (CONF_CHAINS - 1)
            for k in range(CONF_CONV_W):
                term = w[k] * src_scr[c, pl.ds(row0 + k * SUBLANES, SUBLANES), :]
                j = k % CONF_CHAINS
                part[j] = term if part[j] is None else part[j] + term
            while len(part) > 1:
                part = [a + b for a, b in zip(part[0::2], part[1::2])]
            acc_scr[c, pl.ds(row0, SUBLANES), :] = part[0]
            return carry

        lax.fori_loop(0, groups, conv_rows, 0, unroll=8)
    acc = jnp.concatenate([acc_scr[c] for c in range(d // LANES)], axis=-1)
    mu = jnp.mean(acc, axis=-1, keepdims=True)
    cen = acc - mu
    var = jnp.mean(cen * cen, axis=-1, keepdims=True)
    y = cen * lax.rsqrt(var + EPS) * lg_ref[...] + lb_ref[...]
    y = y * jax.nn.sigmoid(y)
    for c in range(d // LANES):
        acc_scr[c] = y[:, c * LANES:(c + 1) * LANES]
    for s in range(SUBLANES):
        for c in range(d // LANES):
            o_ref[s * groups:(s + 1) * groups, c * LANES:(c + 1) * LANES] = (
                acc_scr[c, pl.ds(s, groups, stride=SUBLANES), :].astype(BF16))


def _conformer_layer(lay, x, mod, norm_mix, w_pw1, b_pw1, dw_w, dw_b, ln_g, ln_b):
    t, d = x.shape
    tm = SEQ_TILE
    hb = tm // CONF_HALO
    n_halo = t // CONF_HALO
    tile = pl.BlockSpec((tm, d), lambda i: (i, 0))
    return pl.pallas_call(
        functools.partial(_conformer_kernel, lay=lay),
        out_shape=jax.ShapeDtypeStruct((t, d), BF16),
        grid=(t // tm,),
        in_specs=[pl.BlockSpec((CONF_HALO, d), lambda i: (jnp.maximum(i * hb - 1, 0), 0)),
                  tile,
                  pl.BlockSpec((CONF_HALO, d), lambda i: (jnp.minimum((i + 1) * hb, n_halo - 1), 0)),
                  _resident((1, d)), lay.mod_spec(tm, 0, d), lay.mod_spec(tm, 1, d),
                  _resident((d, 2 * d)), _resident((1, 2 * d)), _resident((CONF_CONV_W, d)), _resident((1, d)),
                  _resident((1, d)), _resident((1, d))],
        out_specs=tile,
        scratch_shapes=[pltpu.VMEM((d // LANES, tm + (CONF_CONV_W - 1) * SUBLANES, LANES), F32),
                        pltpu.VMEM((d // LANES, tm, LANES), F32)],
        compiler_params=_params(1),
        name="conformer",
    )(x, x, x, norm_mix, mod, mod, w_pw1.astype(BF16), b_pw1, dw_w, dw_b, ln_g, ln_b)


def kernel(x_prompt, x_sample, cache_mla_ckv, cache_mla_krope, state_rglru, c, c_ctx, ada_w, ada_b, norm_mix, norm_ffn, mla_w_dq, mla_q_norm, mla_w_uq, mla_w_dkv, mla_kv_norm, mla_w_ukv, mla_w_o, rg_w_x, rg_w_y, rg_conv_w, rg_conv_b, rg_w_a, rg_b_a, rg_w_i, rg_b_i, rg_lam, rg_w_o, fn_w_o, fn_b_o, cf_w_pw1, cf_b_pw1, cf_dw_w, cf_dw_b, cf_ln_g, cf_ln_b, cf_w_pw2, cf_b_pw2, ffn_w_gate, ffn_w_up, ffn_w_down, final_norm):
    bp, sp, d = x_prompt.shape
    bs, ss, _ = x_sample.shape
    depth = ada_w.shape[0]
    assert depth == 4 and bs < SUBLANES and sp == SEQ_TILE and ss % GRID_W == 0
    lay = _Layout(bp, sp, bs, ss)
    x_in = (x_prompt.reshape(lay.tp, d), x_sample.reshape(lay.ts, d))

    cvec = jnp.concatenate([c_ctx[None], c, jnp.zeros((SUBLANES - 1 - bs, d), F32)], axis=0)
    mods = _ada_tables(cvec, ada_w, ada_b).reshape(depth, SUBLANES, 1, -1)
    zero_bias = jnp.zeros((1, d), F32)
    ffn_stacks = (ffn_w_gate, ffn_w_up, ffn_w_down)

    def ffn(layer, weights, x, m, w_out, b_out, extra=(), **tail):
        cast_next = [(w, layer + 1) for w in ffn_stacks] if layer + 1 < depth else []
        out, converted = _ffn_layer(lay, x, m, w_out, b_out, mods[layer], norm_ffn[layer:layer + 1], weights,
                                    cast_next=cast_next + [(w, 0) for w in extra], **tail)
        return out, converted[:len(ffn_stacks)], converted[len(ffn_stacks):]

    m, ckv_state, krope_state, converted = _mla_layer(
        lay, *x_in, mods[0], norm_mix[0:1], cache_mla_ckv[:, 0], cache_mla_krope[:, 0], mla_w_dq[0], mla_q_norm[0:1],
        mla_w_uq[0], mla_w_dkv[0], mla_kv_norm[0:1], mla_w_ukv[0],
        cast_next=[(w, 0) for w in ffn_stacks] + [(mla_w_o, 0)])
    weights, w_o = converted[:3], converted[3]
    x, weights, (w_x, w_y, w_o) = ffn(0, weights, x_in, m, w_o, zero_bias, extra=(rg_w_x, rg_w_y, rg_w_o))

    m, rg_state = _rglru_layer(lay, x, mods[1], norm_mix[1:2], state_rglru[:, 0], w_x, w_y, rg_conv_w[0],
                               rg_conv_b[0:1], rg_w_a[0], rg_b_a[0], rg_w_i[0], rg_b_i[0], rg_lam[0])
    (x, zc, zs), weights, (w_o,) = ffn(1, weights, x, m, w_o, zero_bias, extra=(fn_w_o,),
                                       fnet_next=(norm_mix[2:3], mods[2]))

    m = _fnet_layer(lay, zc, zs)
    x, weights, (w_pw1, w_o) = ffn(2, weights, x, m, w_o, fn_b_o[0:1], extra=(cf_w_pw1, cf_w_pw2))

    m = _conformer_layer(lay, x, mods[3], norm_mix[3:4], w_pw1, cf_b_pw1[0:1], cf_dw_w[0], cf_dw_b[0:1],
                         cf_ln_g[0:1], cf_ln_b[0:1])
    (y_prompt, y_sample), _, _ = ffn(3, weights, x, m, w_o, cf_b_pw2[0:1], final_norm=final_norm[None])

    return (y_prompt.reshape(bp, sp, d), y_sample.reshape(bs, ss, d), ckv_state, krope_state, rg_state)
```

```python
import functools

import numpy as np
import jax
import jax.numpy as jnp
from jax import lax
from jax.experimental import pallas as pl
from jax.experimental.pallas import tpu as pltpu

F32 = jnp.float32
BF16 = jnp.bfloat16

EPS = 1e-6
GRID_W = 64
MLA_HEADS = 8
KV_LORA_RANK = 256
QK_NOPE_DIM = 128
QK_ROPE_DIM = 64
V_HEAD_DIM = 128
ROPE_THETA = 10000.0
RG_BLOCKS = 4
RG_CONV_W = 4
RG_C = 8.0
FNET_GROUPS = 4
CONF_CONV_W = 31

LANES = 128
SUBLANES = 8
BF16_SUBLANES = 16
HEAD_W = 2 * LANES
SEQ_TILE = 256
FFN_TILE = 512
MLA_ROW_TILE = 512
ATTN_Q_TILE = 2048
ATTN_SUB_TILE = 256
ATTN_MXU_SUM_MIN_KEYS = 1024
FNET_ROW_TILE = 512
FNET_PROMPT_GROUP = 4
FFN_CHUNK = 256
CONF_HALO = 16
RG_HALO = 16
CONF_CHAINS = 4
CONF_GLU_CHUNK = 256
VMEM_LIMIT = 52 * 1024 * 1024


def _dot(a, b):
    return jnp.dot(a, b, preferred_element_type=F32)


def _rms(x, g):
    return x * lax.rsqrt(jnp.mean(x * x, axis=-1, keepdims=True) + EPS) * g


def _norm_mod(x, g, shift, scale):
    return _rms(x, g) * (1.0 + scale) + shift


def _resident(shape):
    nd = len(shape)
    return pl.BlockSpec(shape, lambda *_: (0,) * nd, pipeline_mode=pl.Buffered(1))


def _params(n_axes, semantics="parallel"):
    return pltpu.CompilerParams(dimension_semantics=(semantics,) * n_axes, vmem_limit_bytes=VMEM_LIMIT)


class _Layout:
    def __init__(self, n_prompt_seq, prompt_len, n_sample_seq, sample_len):
        self.bp, self.sp, self.bs, self.ss = n_prompt_seq, prompt_len, n_sample_seq, sample_len
        self.tp = n_prompt_seq * prompt_len
        self.ts = n_sample_seq * sample_len
        self.t = self.tp + self.ts

    def prompt_tiles(self, tile):
        assert self.tp % tile == 0 and self.ss % tile == 0
        return self.tp // tile

    def sample_tiles_per_seq(self, tile):
        return self.ss // tile

    def mod_row(self, tile):
        npt, spb = self.prompt_tiles(tile), self.sample_tiles_per_seq(tile)
        return lambda i: jnp.where(i < npt, 0, 1 + (i - npt) // spb)

    def mod_spec(self, tile, chunk, d, order=lambda i: i):
        row = self.mod_row(tile)
        return pl.BlockSpec((None, 1, d), lambda i: (row(order(i)), 0, chunk))

    def row_spec(self, tile, order=lambda i: i):
        row = self.mod_row(tile)
        return lambda d: pl.BlockSpec((None, 1, d), lambda i: (row(order(i)), 0, 0))


def _ada_kernel(c_ref, w_ref, b_ref, o_ref):
    c = c_ref[...]
    s = (c * jax.nn.sigmoid(c)).astype(BF16)
    o_ref[...] = _dot(s, w_ref[...].astype(BF16)) + b_ref[...]


def _ada_tables(cvec, ada_w, ada_b):
    depth, d, n = ada_w.shape
    tn = n // 4
    return pl.pallas_call(
        _ada_kernel,
        out_shape=jax.ShapeDtypeStruct((depth, SUBLANES, n), F32),
        grid=(depth, n // tn),
        in_specs=[
            pl.BlockSpec((SUBLANES, d), lambda l, j: (0, 0)),
            pl.BlockSpec((None, d, tn), lambda l, j: (l, 0, j)),
            pl.BlockSpec((None, 1, tn), lambda l, j: (l, 0, j)),
        ],
        out_specs=pl.BlockSpec((None, SUBLANES, tn), lambda l, j: (l, 0, j)),
        compiler_params=_params(2),
        name="ada_tables",
    )(cvec, ada_w, ada_b.reshape(depth, 1, n))


def _read_tokens(refs, n_prompt_tiles):
    if len(refs) == 1:
        return refs[0][...]
    return jnp.where(pl.program_id(0) < n_prompt_tiles, refs[0][...], refs[1][...])


def _fnet_channel_dft(h, w_ref, zc_ref, zs_ref):
    gw = w_ref.shape[0]
    for g in range(FNET_GROUPS):
        sl = slice(g * gw, (g + 1) * gw)
        f = _dot(h[:, sl], w_ref[...])
        zc_ref[:, sl] = f[:, :gw].astype(BF16)
        zs_ref[:, sl] = f[:, gw:].astype(BF16)


def _ffn_kernel(*refs, n_chunks, n_x, n_m, tail, n_prompt_tiles, n_cast):
    x_refs, m_refs, refs = refs[:n_x], refs[n_x:n_x + n_m], refs[n_x + n_m:]
    wo_ref, bo_ref, g1_ref, sh_ref, sc_ref, g2_ref, nf_ref, wg_ref, wu_ref, wd_ref = refs[:10]
    refs, a_ref = refs[10:-1], refs[-1]
    if n_cast:
        for src, dst in zip(refs[:n_cast], refs[-n_cast:]):
            dst[...] = src[...].astype(BF16)
        refs = refs[n_cast:-n_cast]
    m = _read_tokens(m_refs, n_prompt_tiles)
    x1 = _read_tokens(x_refs, n_prompt_tiles) + g1_ref[...] * (_dot(m, wo_ref[...]) + bo_ref[...])
    h = _norm_mod(x1, nf_ref[...], sh_ref[...], sc_ref[...]).astype(BF16)
    for c in range(n_chunks):
        sl = slice(c * FFN_CHUNK, (c + 1) * FFN_CHUNK)
        g = _dot(h, wg_ref[:, sl])
        u = _dot(h, wu_ref[:, sl])
        a_ref[:, sl] = (g * jax.nn.sigmoid(g) * u).astype(BF16)
    x2 = x1 + g2_ref[...] * _dot(a_ref[...], wd_ref[...])
    if tail is None:
        refs[0][...] = x2
    elif tail == "fnet":
        nm_ref, sh1_ref, sc1_ref, dft_ref, o_ref, zc_ref, zs_ref = refs
        o_ref[...] = x2
        h_next = _norm_mod(x2, nm_ref[...], sh1_ref[...], sc1_ref[...]).astype(BF16)
        _fnet_channel_dft(h_next, dft_ref, zc_ref, zs_ref)
    else:
        fin_ref, op_ref, os_ref = refs
        y = _rms(x2, fin_ref[...])
        is_prompt = pl.program_id(0) < n_prompt_tiles

        @pl.when(is_prompt)
        def _():
            op_ref[...] = y

        @pl.when(jnp.logical_not(is_prompt))
        def _():
            os_ref[...] = y


def _cast_specs(stack, index, n_steps):
    _, rows, width = stack.shape
    span = 1
    while (rows * span) % n_steps or (rows * span // n_steps) % BF16_SUBLANES:
        span *= 2
    assert n_steps % span == 0
    block = rows * span // n_steps
    return (pl.BlockSpec((None, block, width), lambda i: (index, i // span, 0)),
            pl.BlockSpec((block, width), lambda i: (i // span, 0)),
            jax.ShapeDtypeStruct((rows, width), BF16))


def _ffn_layer(lay, x, m, w_out, b_out, mod, norm_ffn, weights, final_norm=None, fnet_next=None, cast_next=None):
    xs = x if isinstance(x, tuple) else (x,)
    ms = m if isinstance(m, tuple) else (m,)
    d = xs[0].shape[1]
    w_gate, w_up, w_down = weights
    dff = w_gate.shape[1]
    assert dff % FFN_CHUNK == 0
    tm = FFN_TILE
    n_steps = lay.t // tm
    npt = lay.prompt_tiles(tm)
    tile = pl.BlockSpec((tm, d), lambda i: (i, 0))
    split = [pl.BlockSpec((tm, d), lambda i: (jnp.minimum(i, npt - 1), 0)),
             pl.BlockSpec((tm, d), lambda i: (jnp.maximum(i - npt, 0), 0))]
    in_specs = (split if len(xs) == 2 else [tile]) + (split if len(ms) == 2 else [tile]) + [
        _resident((d, d)), _resident((1, d)),
        lay.mod_spec(tm, 2, d), lay.mod_spec(tm, 3, d), lay.mod_spec(tm, 4, d), lay.mod_spec(tm, 5, d),
        _resident((1, d)), _resident((d, dff)), _resident((d, dff)), _resident((dff, d))]
    args = list(xs) + list(ms) + [w_out, b_out, mod, mod, mod, mod, norm_ffn, w_gate, w_up, w_down]
    cast_out_specs, cast_out_shapes = [], []
    for stack, index in cast_next or ():
        src_spec, dst_spec, dst_shape = _cast_specs(stack, index, n_steps)
        in_specs.append(src_spec)
        args.append(stack)
        cast_out_specs.append(dst_spec)
        cast_out_shapes.append(dst_shape)
    out_shapes, out_specs, tail = [jax.ShapeDtypeStruct((lay.t, d), F32)], [tile], None
    if final_norm is not None:
        tail = "final"
        in_specs.append(_resident((1, d)))
        args.append(final_norm)
        out_shapes = [jax.ShapeDtypeStruct((lay.tp, d), F32), jax.ShapeDtypeStruct((lay.ts, d), F32)]
        out_specs = list(split)
    elif fnet_next is not None:
        tail = "fnet"
        norm_next, mod_next = fnet_next
        dft = _fnet_channel_table(d)
        in_specs += [_resident((1, d)), lay.mod_spec(tm, 0, d), lay.mod_spec(tm, 1, d), _resident(dft.shape)]
        args += [norm_next, mod_next, mod_next, dft]
        out_shapes += [jax.ShapeDtypeStruct((lay.t, d), BF16), jax.ShapeDtypeStruct((lay.t, d), BF16)]
        out_specs += [tile, tile]
    n_main = len(out_shapes)
    outs = pl.pallas_call(
        functools.partial(_ffn_kernel, n_chunks=dff // FFN_CHUNK, n_x=len(xs), n_m=len(ms), tail=tail,
                          n_prompt_tiles=npt, n_cast=len(cast_out_specs)),
        out_shape=tuple(out_shapes + cast_out_shapes),
        grid=(n_steps,),
        in_specs=in_specs,
        out_specs=tuple(out_specs + cast_out_specs),
        scratch_shapes=[pltpu.VMEM((tm, dff), BF16)],
        compiler_params=_params(1, "arbitrary"),
        name="ffn",
    )(*args)
    main = outs[0] if n_main == 1 else tuple(outs[:n_main])
    return main, tuple(outs[n_main:])


def _mla_qkv_kernel(*refs, scale, n_prompt_tiles, n_cast):
    (xp_ref, xs_ref, nm_ref, sh_ref, sc_ref, cos_ref, sin_ref, wdq_ref, qn_ref, wqn_ref, wqr_ref, wqx_ref,
     wdkv_ref, wdkvx_ref, kvn_ref, wk_ref, wv_ref) = refs[:17]
    q_ref, z_ref, k_ref, v_ref = refs[17 + n_cast:21 + n_cast]
    for src, dst in zip(refs[17:17 + n_cast], refs[21 + n_cast:]):
        dst[...] = src[...].astype(BF16)
    x_all = _read_tokens((xp_ref, xs_ref), n_prompt_tiles)
    for j in range(x_all.shape[0] // SEQ_TILE):
        rows = slice(j * SEQ_TILE, (j + 1) * SEQ_TILE)
        h = _norm_mod(x_all[rows], nm_ref[...], sh_ref[...], sc_ref[...]).astype(BF16)
        cos, sin = cos_ref[rows, :], sin_ref[rows, :]
        cq = _rms(_dot(h, wdq_ref[...]), qn_ref[...]).astype(BF16)
        q_nope = _dot(cq, wqn_ref[...])
        q_rope = _dot(cq, wqr_ref[...])
        q_swap = _dot(cq, wqx_ref[...])
        for hd in range(MLA_HEADS):
            sl = slice(hd * LANES, (hd + 1) * LANES)
            q_ref[rows, hd * HEAD_W:hd * HEAD_W + LANES] = (q_nope[:, sl] * scale).astype(BF16)
            q_ref[rows, hd * HEAD_W + LANES:(hd + 1) * HEAD_W] = (
                (q_rope[:, sl] * cos + q_swap[:, sl] * sin) * scale).astype(BF16)
        z = _dot(h, wdkv_ref[...])
        z_swap = _dot(h, wdkvx_ref[...])
        ckv = _rms(z[:, :KV_LORA_RANK], kvn_ref[...])
        k_rope = z[:, KV_LORA_RANK:] * cos + z_swap * sin
        z_ref[rows, :KV_LORA_RANK] = ckv
        z_ref[rows, KV_LORA_RANK:] = k_rope
        _expand_kv(ckv, k_rope, wk_ref, wv_ref, k_ref, v_ref, rows)


def _expand_kv(ckv, k_rope, wk_ref, wv_ref, k_ref, v_ref, rows=slice(None)):
    ckv = ckv.astype(BF16)
    k_rope = k_rope.astype(BF16)
    k_nope = _dot(ckv, wk_ref[...])
    for hd in range(MLA_HEADS):
        k_ref[rows, hd * HEAD_W:hd * HEAD_W + LANES] = k_nope[:, hd * LANES:(hd + 1) * LANES].astype(BF16)
        k_ref[rows, hd * HEAD_W + LANES:(hd + 1) * HEAD_W] = k_rope
    v_ref[rows, :] = _dot(ckv, wv_ref[...]).astype(BF16)


def _kv_expand_kernel(z_ref, wk_ref, wv_ref, k_ref, v_ref):
    z = z_ref[...]
    _expand_kv(z[:, :KV_LORA_RANK], z[:, KV_LORA_RANK:], wk_ref, wv_ref, k_ref, v_ref)


def _kv_expand(z, wk, wv, n=None):
    zw = z.shape[1]
    n = z.shape[0] if n is None else n
    tm = 2 * SEQ_TILE if n % (2 * SEQ_TILE) == 0 else SEQ_TILE
    return pl.pallas_call(
        _kv_expand_kernel,
        out_shape=(jax.ShapeDtypeStruct((n, MLA_HEADS * HEAD_W), BF16),
                   jax.ShapeDtypeStruct((n, MLA_HEADS * V_HEAD_DIM), BF16)),
        grid=(n // tm,),
        in_specs=[pl.BlockSpec((tm, zw), lambda i: (i, 0)), _resident(wk.shape), _resident(wv.shape)],
        out_specs=(pl.BlockSpec((tm, MLA_HEADS * HEAD_W), lambda i: (i, 0)),
                   pl.BlockSpec((tm, MLA_HEADS * V_HEAD_DIM), lambda i: (i, 0))),
        compiler_params=_params(1),
        name="mla_kv_expand",
    )(z, wk, wv)


def _attn_kernel(*refs, heads, sub, n_kv, mxu_sum):
    q_ref, kv_refs, o_ref = refs[0], refs[1:1 + 2 * n_kv], refs[-1]
    tq = q_ref.shape[0]
    for hd in range(heads):
        ks = [r[:, hd * HEAD_W:(hd + 1) * HEAD_W] for r in kv_refs[0::2]]
        vs = [r[:, hd * V_HEAD_DIM:(hd + 1) * V_HEAD_DIM] for r in kv_refs[1::2]]
        if mxu_sum:
            vs = [jnp.concatenate([v, jnp.ones_like(v)], axis=-1) for v in vs]
        for j in range(tq // sub):
            rows = slice(j * sub, (j + 1) * sub)
            q = q_ref[rows, hd * HEAD_W:(hd + 1) * HEAD_W]
            ss = [lax.dot_general(q, k, (((1,), (1,)), ((), ())), preferred_element_type=F32) for k in ks]
            top = functools.reduce(jnp.maximum, [jnp.max(s, axis=-1, keepdims=True) for s in ss])
            ps = [jnp.exp(s - top) for s in ss]
            acc = sum(_dot(p.astype(BF16), v) for p, v in zip(ps, vs))
            if mxu_sum:
                den = acc[:, V_HEAD_DIM:V_HEAD_DIM + 1]
            else:
                den = sum(jnp.sum(p, axis=-1, keepdims=True) for p in ps)
            o_ref[rows, hd * V_HEAD_DIM:(hd + 1) * V_HEAD_DIM] = (acc[:, :V_HEAD_DIM] / den).astype(BF16)


def _attention(q, kv, *, n_seq, q_len, q_row0, tq, heads):
    hg = MLA_HEADS // heads
    assert q_row0 % tq == 0 and q_len % tq == 0
    qb0, qt = q_row0 // tq, q_len // tq
    in_specs = [pl.BlockSpec((tq, heads * HEAD_W), lambda b, g, j: (qb0 + b * qt + j, g))]
    args = [q]
    for k, v, k_len, row0 in kv:
        assert row0 % k_len == 0
        kb0 = row0 // k_len
        in_specs += [pl.BlockSpec((k_len, heads * HEAD_W), lambda b, g, j, kb0=kb0: (kb0 + b, g)),
                     pl.BlockSpec((k_len, heads * V_HEAD_DIM), lambda b, g, j, kb0=kb0: (kb0 + b, g))]
        args += [k, v]
    return pl.pallas_call(
        functools.partial(_attn_kernel, heads=heads, sub=min(ATTN_SUB_TILE, tq), n_kv=len(kv),
                          mxu_sum=sum(src[2] for src in kv) >= ATTN_MXU_SUM_MIN_KEYS),
        out_shape=jax.ShapeDtypeStruct((n_seq * q_len, MLA_HEADS * V_HEAD_DIM), BF16),
        grid=(n_seq, hg, qt),
        in_specs=in_specs,
        out_specs=pl.BlockSpec((tq, heads * V_HEAD_DIM), lambda b, g, j: (b * qt + j, g)),
        compiler_params=_params(3),
        name="mla_attention",
    )(*args)


def _rope_tables(lay, tile):
    pos = np.arange(lay.ss)
    n_freq = QK_ROPE_DIM // 4
    inv_freq = ROPE_THETA ** (-np.arange(n_freq, dtype=np.float64) / n_freq)
    ang = np.concatenate([(pos // GRID_W)[:, None] * inv_freq, (pos % GRID_W)[:, None] * inv_freq], axis=-1)
    pad = np.zeros((lay.ss, LANES - QK_ROPE_DIM))
    cos = np.concatenate([np.cos(ang), np.cos(ang), pad], axis=-1)
    sin = np.concatenate([np.sin(ang), np.sin(ang), pad], axis=-1)
    ident_cos = np.concatenate([np.ones((tile, QK_ROPE_DIM)), np.zeros((tile, LANES - QK_ROPE_DIM))], axis=-1)
    cos = np.concatenate([ident_cos, cos], axis=0)
    sin = np.concatenate([np.zeros((tile, LANES)), sin], axis=0)
    return jnp.asarray(cos, F32), jnp.asarray(sin, F32)


def _swap_halves(w):
    half = QK_ROPE_DIM // 2
    return jnp.concatenate([-w[..., half:], w[..., :half]], axis=-1)


def _mla_layer(lay, x_prompt, x_sample, mod, norm_mix, cache_ckv, cache_krope, w_dq, q_norm, w_uq, w_dkv, kv_norm,
               w_ukv, cast_next=()):
    t, d = lay.t, x_prompt.shape[1]
    tm = MLA_ROW_TILE
    rq = w_dq.shape[1]
    qk = QK_NOPE_DIM + QK_ROPE_DIM
    zw = KV_LORA_RANK + LANES
    rope_pad = [(0, 0)] * 2 + [(0, LANES - QK_ROPE_DIM)]

    wq = w_uq.reshape(rq, MLA_HEADS, qk)
    wq_nope = wq[:, :, :QK_NOPE_DIM].reshape(rq, -1).astype(BF16)
    wq_rope = jnp.pad(wq[:, :, QK_NOPE_DIM:], rope_pad).reshape(rq, -1).astype(BF16)
    wq_swap = jnp.pad(_swap_halves(wq[:, :, QK_NOPE_DIM:]), rope_pad).reshape(rq, -1).astype(BF16)
    wdkv = jnp.pad(w_dkv, [(0, 0), (0, LANES - QK_ROPE_DIM)]).astype(BF16)
    wdkv_swap = jnp.pad(_swap_halves(w_dkv[:, KV_LORA_RANK:]), [(0, 0), (0, LANES - QK_ROPE_DIM)]).astype(BF16)
    wkv = w_ukv.reshape(KV_LORA_RANK, MLA_HEADS, QK_NOPE_DIM + V_HEAD_DIM)
    wk = wkv[:, :, :QK_NOPE_DIM].reshape(KV_LORA_RANK, -1).astype(BF16)
    wv = wkv[:, :, QK_NOPE_DIM:].reshape(KV_LORA_RANK, -1).astype(BF16)

    cos, sin = _rope_tables(lay, tm)
    npt, spb = lay.prompt_tiles(tm), lay.sample_tiles_per_seq(tm)
    rope_spec = pl.BlockSpec((tm, LANES), lambda i: (jnp.where(i < npt, 0, 1 + (i - npt) % spb), 0))
    def rows(width):
        return pl.BlockSpec((tm, width), lambda i: (i, 0))

    casts = [_cast_specs(stack, index, t // tm) for stack, index in cast_next]
    q, z, k_new, v_new, *converted = pl.pallas_call(
        functools.partial(_mla_qkv_kernel, scale=qk ** -0.5, n_prompt_tiles=npt, n_cast=len(casts)),
        out_shape=(jax.ShapeDtypeStruct((t, MLA_HEADS * HEAD_W), BF16), jax.ShapeDtypeStruct((t, zw), F32),
                   jax.ShapeDtypeStruct((t, MLA_HEADS * HEAD_W), BF16),
                   jax.ShapeDtypeStruct((t, MLA_HEADS * V_HEAD_DIM), BF16)) + tuple(c[2] for c in casts),
        grid=(t // tm,),
        in_specs=[pl.BlockSpec((tm, d), lambda i: (jnp.minimum(i, npt - 1), 0)),
                  pl.BlockSpec((tm, d), lambda i: (jnp.maximum(i - npt, 0), 0)), _resident((1, d)),
                  lay.mod_spec(tm, 0, d), lay.mod_spec(tm, 1, d), rope_spec, rope_spec,
                  _resident(w_dq.shape), _resident((1, rq)), _resident(wq_nope.shape), _resident(wq_rope.shape),
                  _resident(wq_swap.shape), _resident(wdkv.shape), _resident(wdkv_swap.shape),
                  _resident((1, KV_LORA_RANK)), _resident(wk.shape), _resident(wv.shape)] + [c[0] for c in casts],
        out_specs=(rows(MLA_HEADS * HEAD_W), rows(zw), rows(MLA_HEADS * HEAD_W), rows(MLA_HEADS * V_HEAD_DIM))
        + tuple(c[1] for c in casts),
        compiler_params=_params(1, "arbitrary"),
        name="mla_qkv",
    )(x_prompt, x_sample, norm_mix, mod, mod, cos, sin, w_dq.astype(BF16), q_norm, wq_nope, wq_rope, wq_swap, wdkv,
      wdkv_swap, kv_norm, wk, wv, *[stack for stack, _ in cast_next])

    past = cache_ckv.shape[1]
    z_cache = jnp.concatenate(
        [cache_ckv, cache_krope, jnp.zeros((lay.bs, past, LANES - QK_ROPE_DIM), F32)], axis=-1)
    k_past, v_past = _kv_expand(z_cache.reshape(lay.bs * past, zw), wk, wv)

    o_prompt = _attention(q, [(k_new, v_new, lay.sp, 0)], n_seq=lay.bp, q_len=lay.sp, q_row0=0, tq=lay.sp,
                          heads=MLA_HEADS)
    o_sample = _attention(q, [(k_past, v_past, past, 0), (k_new, v_new, lay.ss, lay.tp)], n_seq=lay.bs, q_len=lay.ss,
                          q_row0=lay.tp, tq=min(ATTN_Q_TILE, lay.ss), heads=1)
    ckv_state = z[:lay.tp, :KV_LORA_RANK].reshape(lay.bp, 1, lay.sp, KV_LORA_RANK)
    krope_state = z[:lay.tp, KV_LORA_RANK:KV_LORA_RANK + QK_ROPE_DIM].reshape(lay.bp, 1, lay.sp, QK_ROPE_DIM)
    return (o_prompt, o_sample), ckv_state, krope_state, tuple(converted)


def _seq_position(lay, tile, t):
    npt, spb = lay.prompt_tiles(tile), lay.sample_tiles_per_seq(tile)
    ppb = lay.sp // tile
    r = jnp.where(t < npt, t % ppb, (t - npt) % spb)
    n = jnp.where(t < npt, ppb, spb)
    return r, n


def _softplus(x):
    return jnp.maximum(x, 0.0) + jnp.log1p(jnp.exp(-jnp.abs(x)))


def _rglru_kernel(*refs, lay, n_tiles, reverse):
    if reverse:
        (xc_ref, hc_ref, yf_ref, h0_ref, wa_ref, ba_ref, wi_ref, bi_ref, lam_ref, wy_ref,
         o_ref, hl_ref, a_scr, u_scr, perm_scr, carry_scr) = refs
    else:
        (xp_ref, x_ref, xn_ref, nm_ref, sh_ref, sc_ref, h0_ref, wx_ref, cw_ref, cb_ref,
         wa_ref, ba_ref, wi_ref, bi_ref, lam_ref,
         o_ref, xc_ref, hc_ref, hl_ref, xw_scr, a_scr, u_scr, perm_scr, carry_scr) = refs
    tm, d = xc_ref.shape
    groups = tm // SUBLANES
    bw = d // RG_BLOCKS
    i = pl.program_id(0)
    t = n_tiles - 1 - i if reverse else i
    r, n = _seq_position(lay, tm, t)

    @pl.when(i == 0)
    def _():
        carry_scr[...] = jnp.zeros_like(carry_scr)

    if reverse:
        xc = xc_ref[...]
    else:
        has_prev, has_next = r > 0, r < n - 1

        def pre(x):
            return _norm_mod(x, nm_ref[...], sh_ref[...], sc_ref[...]).astype(BF16)

        x_nat = x_ref[...]
        for c in range(d // LANES):
            for s in range(SUBLANES):
                perm_scr[c, pl.ds(s, groups, stride=SUBLANES), :] = x_nat[s * groups:(s + 1) * groups,
                                                                         c * LANES:(c + 1) * LANES]
        hc = pre(jnp.concatenate([perm_scr[c] for c in range(d // LANES)], axis=-1))
        hc_ref[...] = hc
        xw = _dot(jnp.concatenate([pre(xp_ref[...]), hc, pre(xn_ref[...])], axis=0), wx_ref[...])
        sub = lax.broadcasted_iota(jnp.int32, (SUBLANES, d), 0)

        def tile_group(j):
            return xw[RG_HALO + j * SUBLANES:RG_HALO + (j + 1) * SUBLANES]

        def halo_row(k, valid):
            return jnp.broadcast_to(jnp.where(valid, xw[k:k + 1], 0.0), (SUBLANES, d))

        before = jnp.where(sub == 0, halo_row(RG_HALO - 1, has_prev), pltpu.roll(tile_group(groups - 1), 1, 0))
        after = [jnp.where(sub == SUBLANES - 1, halo_row(RG_HALO + tm + j, has_next),
                           pltpu.roll(tile_group(j), SUBLANES - 1, 0)) for j in range(RG_CONV_W - 2)]
        xw_scr[0:SUBLANES] = before
        xw_scr[SUBLANES:SUBLANES + tm] = xw[RG_HALO:RG_HALO + tm]
        for j, grp in enumerate(after):
            xw_scr[(groups + 1 + j) * SUBLANES:(groups + 2 + j) * SUBLANES] = grp
        xc = cb_ref[...]
        for k in range(RG_CONV_W):
            xc = xc + cw_ref[k:k + 1, :] * xw_scr[k * SUBLANES:k * SUBLANES + tm]
        xc_ref[...] = xc

    for nb in range(RG_BLOCKS):
        sl = slice(nb * bw, (nb + 1) * bw)
        xcn = xc[:, sl]
        xcb = xcn.astype(BF16)
        rg = jax.nn.sigmoid(_dot(xcb, wa_ref[nb]) + ba_ref[:, sl])
        ig = jax.nn.sigmoid(_dot(xcb, wi_ref[nb]) + bi_ref[:, sl])
        log_a = -RG_C * rg * _softplus(-lam_ref[:, sl])
        a = jnp.exp(log_a)
        u = jnp.sqrt(-jnp.tanh(log_a) * (a * a + 1.0)) * (ig * xcn)
        a_scr[:, :, sl] = a.reshape(groups, SUBLANES, bw)
        u_scr[:, :, sl] = u.reshape(groups, SUBLANES, bw)

    def local_step(g, carry):
        h, p = carry
        j = groups - 1 - g if reverse else g
        a = a_scr[j]
        h = a * h + u_scr[j]
        p = a * p
        u_scr[j] = h
        a_scr[j] = p
        return h, p

    h_seg, p_seg = lax.fori_loop(0, groups, local_step, (jnp.zeros((SUBLANES, d), F32), jnp.ones((SUBLANES, d), F32)))
    is_start = r == n - 1 if reverse else r == 0
    state = jnp.where(is_start, h0_ref[...], carry_scr[...])
    entering = [None] * SUBLANES
    for s in (range(SUBLANES - 1, -1, -1) if reverse else range(SUBLANES)):
        entering[s] = state
        state = h_seg[s:s + 1] + p_seg[s:s + 1] * state
    carry_scr[...] = state
    hl_ref[...] = state
    y = u_scr[...] + a_scr[...] * jnp.concatenate(entering, axis=0)[None]
    y = y.reshape(tm, d)
    if reverse:
        gate = jax.nn.gelu(_dot(hc_ref[...], wy_ref[...]))
        mixed = (yf_ref[...] + y) * gate
        for c in range(d // LANES):
            perm_scr[c] = mixed[:, c * LANES:(c + 1) * LANES]
        for s in range(SUBLANES):
            for c in range(d // LANES):
                o_ref[s * groups:(s + 1) * groups, c * LANES:(c + 1) * LANES] = (
                    perm_scr[c, pl.ds(s, groups, stride=SUBLANES), :].astype(BF16))
    else:
        o_ref[...] = y


def _rglru_layer(lay, x, mod, norm_mix, state, w_x, w_y, conv_w, conv_b, w_a, b_a, w_i, b_i, lam):
    t, d = x.shape
    tm = SEQ_TILE
    n_tiles = t // tm
    hb = tm // SUBLANES
    halo_blocks, n_halo = tm // RG_HALO, t // RG_HALO
    bw = d // RG_BLOCKS

    def run(reverse, *fwd_out):
        dr = int(reverse)
        order = (lambda i: n_tiles - 1 - i) if reverse else (lambda i: i)
        h0 = jnp.concatenate([jnp.zeros((1, d), F32), state[:, dr], jnp.zeros((SUBLANES - 1 - lay.bs, d), F32)])
        h0_spec = lay.row_spec(tm, order)(d)
        tile = pl.BlockSpec((tm, d), lambda i: (order(i), 0))
        gate_specs = [_resident((RG_BLOCKS, bw, bw)), _resident((1, d)), _resident((RG_BLOCKS, bw, bw)),
                      _resident((1, d)), _resident((1, d))]
        gate_args = [w_a[dr].astype(BF16), b_a[dr:dr + 1], w_i[dr].astype(BF16), b_i[dr:dr + 1], lam[dr:dr + 1]]
        scan_scratch = [pltpu.VMEM((hb, SUBLANES, d), F32), pltpu.VMEM((hb, SUBLANES, d), F32),
                        pltpu.VMEM((d // LANES, tm, LANES), F32)]
        last_spec = pl.BlockSpec((None, 1, d), lambda i: (order(i), 0, 0))
        last_shape = jax.ShapeDtypeStruct((n_tiles, 1, d), F32)
        if reverse:
            xc, hc, y_fwd = fwd_out
            in_specs = [tile, tile, tile, h0_spec] + gate_specs + [_resident((d, d))]
            args = [xc, hc, y_fwd, h0.reshape(SUBLANES, 1, d)] + gate_args + [w_y.astype(BF16)]
            out_shape = (jax.ShapeDtypeStruct((t, d), BF16), last_shape)
            out_specs = (tile, last_spec)
            scratch = list(scan_scratch)
        else:
            in_specs = [pl.BlockSpec((RG_HALO, d), lambda i: (jnp.maximum(i * halo_blocks - 1, 0), 0)),
                        tile,
                        pl.BlockSpec((RG_HALO, d), lambda i: (jnp.minimum((i + 1) * halo_blocks, n_halo - 1), 0)),
                        _resident((1, d)), lay.mod_spec(tm, 0, d), lay.mod_spec(tm, 1, d), h0_spec,
                        _resident((d, d)), _resident((RG_CONV_W, d)), _resident((1, d))] + gate_specs
            args = [x, x, x, norm_mix, mod, mod, h0.reshape(SUBLANES, 1, d), w_x.astype(BF16), conv_w, conv_b] + gate_args
            out_shape = (jax.ShapeDtypeStruct((t, d), F32), jax.ShapeDtypeStruct((t, d), F32),
                         jax.ShapeDtypeStruct((t, d), BF16), last_shape)
            out_specs = (tile, tile, tile, last_spec)
            scratch = [pltpu.VMEM((tm + 2 * RG_HALO, d), F32)] + scan_scratch
        scratch.append(pltpu.VMEM((1, d), F32))
        return pl.pallas_call(
            functools.partial(_rglru_kernel, lay=lay, n_tiles=n_tiles, reverse=reverse),
            out_shape=out_shape,
            grid=(n_tiles,),
            in_specs=in_specs,
            out_specs=out_specs,
            scratch_shapes=scratch,
            compiler_params=_params(1, "arbitrary"),
            name="rglru_bwd" if reverse else "rglru_fwd",
        )(*args)

    y_fwd, xc, hc, last_f = run(False)
    m, last_b = run(True, xc, hc, y_fwd)
    ppb = lay.sp // tm
    tail = last_f[:lay.prompt_tiles(tm)].reshape(lay.bp, ppb, d)[:, ppb - 1]
    head = last_b[:lay.prompt_tiles(tm)].reshape(lay.bp, ppb, d)[:, 0]
    return m, jnp.stack([tail, head], axis=1)[:, None]


def _dft_tables(n, scale):
    jk = np.outer(np.arange(n), np.arange(n)) % n
    ang = 2.0 * np.pi * jk / n
    return np.cos(ang) * scale, np.sin(ang) * scale


def _fnet_channel_table(d):
    gw = d // FNET_GROUPS
    cos, sin = _dft_tables(gw, gw ** -0.5)
    return jnp.asarray(np.concatenate([cos, sin], axis=1), F32).astype(BF16)


def _fnet_position_kernel(wc_ref, ws_ref, zc_ref, zs_ref, o_ref, *, seq_len):
    for i in range(zc_ref.shape[0] // seq_len):
        rows = slice(i * seq_len, (i + 1) * seq_len)
        o_ref[i * wc_ref.shape[0]:(i + 1) * wc_ref.shape[0], :] = (
            _dot(wc_ref[...], zc_ref[rows, :]) - _dot(ws_ref[...], zs_ref[rows, :])).astype(BF16)


def _fnet_position(zc, zs, *, n_seq, seq_len, row0, tm, seqs_per_step=1):
    d = zc.shape[1]
    g = seqs_per_step
    assert row0 % (g * seq_len) == 0 and seq_len % tm == 0 and n_seq % g == 0 and (g == 1 or tm == seq_len)
    cos, sin = _dft_tables(seq_len, seq_len ** -0.5)
    wc, ws = jnp.asarray(cos, F32).astype(BF16), jnp.asarray(sin, F32).astype(BF16)
    sb0, rt = row0 // (g * seq_len), seq_len // tm
    w_spec = pl.BlockSpec((tm, seq_len), lambda b, j: (j, 0))
    z_spec = pl.BlockSpec((g * seq_len, d), lambda b, j: (sb0 + b, 0))
    return pl.pallas_call(
        functools.partial(_fnet_position_kernel, seq_len=seq_len),
        out_shape=jax.ShapeDtypeStruct((n_seq * seq_len, d), BF16),
        grid=(n_seq // g, rt),
        in_specs=[w_spec, w_spec, z_spec, z_spec],
        out_specs=pl.BlockSpec((g * tm, d), lambda b, j: (b * rt + j, 0)),
        compiler_params=_params(2),
        name="fnet_position",
    )(wc, ws, zc, zs)


def _fnet_layer(lay, zc, zs):
    group = FNET_PROMPT_GROUP if lay.bp % FNET_PROMPT_GROUP == 0 else 1
    return (_fnet_position(zc, zs, n_seq=lay.bp, seq_len=lay.sp, row0=0, tm=lay.sp, seqs_per_step=group),
            _fnet_position(zc, zs, n_seq=lay.bs, seq_len=lay.ss, row0=lay.tp, tm=min(FNET_ROW_TILE, lay.ss)))


def _conformer_kernel(xp_ref, xc_ref, xn_ref, nm_ref, sh_ref, sc_ref, w1_ref, b1_ref, dw_ref, db_ref, lg_ref, lb_ref,
                      o_ref, src_scr, acc_scr, *, lay):
    tm, d = xc_ref.shape
    r, n = _seq_position(lay, tm, pl.program_id(0))
    has_prev, has_next = r > 0, r < n - 1

    def pre(ref):
        return _norm_mod(ref[...], nm_ref[...], sh_ref[...], sc_ref[...]).astype(BF16)

    h = jnp.concatenate([pre(xp_ref), pre(xc_ref), pre(xn_ref)], axis=0)
    groups = tm // SUBLANES
    n_before = (CONF_CONV_W - 1) // 2
    n_after = CONF_CONV_W - 1 - n_before
    sub = lax.broadcasted_iota(jnp.int32, (SUBLANES, LANES), 0)

    def group_rows(g):
        return slice(g * SUBLANES, (g + 1) * SUBLANES)

    for j in range(d // CONF_GLU_CHUNK):
        val = slice(j * CONF_GLU_CHUNK, (j + 1) * CONF_GLU_CHUNK)
        gate = slice(d + j * CONF_GLU_CHUNK, d + (j + 1) * CONF_GLU_CHUNK)
        glu = (_dot(h, w1_ref[:, val]) + b1_ref[:, val]) * jax.nn.sigmoid(_dot(h, w1_ref[:, gate]) + b1_ref[:, gate])
        for i in range(CONF_GLU_CHUNK // LANES):
            c = j * CONF_GLU_CHUNK // LANES + i
            col = glu[:, i * LANES:(i + 1) * LANES]
            for s in range(SUBLANES):
                src_scr[c, pl.ds(n_before * SUBLANES + s, groups, stride=SUBLANES), :] = (
                    col[CONF_HALO + s * groups:CONF_HALO + (s + 1) * groups])
            for k in range(1, n_before + 1):
                edge = jnp.where(has_prev, col[CONF_HALO - k:CONF_HALO - k + 1], 0.0)
                inner = pltpu.roll(src_scr[c, group_rows(n_before + groups - k), :], 1, 0)
                src_scr[c, group_rows(n_before - k), :] = jnp.where(sub == 0, edge, inner)
            for k in range(n_after):
                edge = jnp.where(has_next, col[CONF_HALO + tm + k:CONF_HALO + tm + k + 1], 0.0)
                inner = pltpu.roll(src_scr[c, group_rows(n_before + k), :], SUBLANES - 1, 0)
                src_scr[c, group_rows(n_before + groups + k), :] = jnp.where(sub == SUBLANES - 1, edge, inner)
    for c in range(d // LANES):
        lanes = slice(c * LANES, (c + 1) * LANES)
        w = [jnp.broadcast_to(dw_ref[k:k + 1, lanes], (SUBLANES, LANES)) for k in range(CONF_CONV_W)]
        bias = jnp.broadcast_to(db_ref[:, lanes], (SUBLANES, LANES))

        def conv_rows(g, carry, c=c, w=w, bias=bias):
            row0 = pl.multiple_of(g * SUBLANES, SUBLANES)
            part = [bias] + [None] * (CONF_CHAINS - 1)
            for k in range(CONF_CONV_W):
                term = w[k] * src_scr[c, pl.ds(row0 + k * SUBLANES, SUBLANES), :]
                j = k % CONF_CHAINS
                part[j] = term if part[j] is None else part[j] + term
            while len(part) > 1:
                part = [a + b for a, b in zip(part[0::2], part[1::2])]
            acc_scr[c, pl.ds(row0, SUBLANES), :] = part[0]
            return carry

        lax.fori_loop(0, groups, conv_rows, 0, unroll=8)
    acc = jnp.concatenate([acc_scr[c] for c in range(d // LANES)], axis=-1)
    mu = jnp.mean(acc, axis=-1, keepdims=True)
    cen = acc - mu
    var = jnp.mean(cen * cen, axis=-1, keepdims=True)
    y = cen * lax.rsqrt(var + EPS) * lg_ref[...] + lb_ref[...]
    y = y * jax.nn.sigmoid(y)
    for c in range(d // LANES):
        acc_scr[c] = y[:, c * LANES:(c + 1) * LANES]
    for s in range(SUBLANES):
        for c in range(d // LANES):
            o_ref[s * groups:(s + 1) * groups, c * LANES:(c + 1) * LANES] = (
                acc_scr[c, pl.ds(s, groups, stride=SUBLANES), :].astype(BF16))


def _conformer_layer(lay, x, mod, norm_mix, w_pw1, b_pw1, dw_w, dw_b, ln_g, ln_b):
    t, d = x.shape
    tm = SEQ_TILE
    hb = tm // CONF_HALO
    n_halo = t // CONF_HALO
    tile = pl.BlockSpec((tm, d), lambda i: (i, 0))
    return pl.pallas_call(
        functools.partial(_conformer_kernel, lay=lay),
        out_shape=jax.ShapeDtypeStruct((t, d), BF16),
        grid=(t // tm,),
        in_specs=[pl.BlockSpec((CONF_HALO, d), lambda i: (jnp.maximum(i * hb - 1, 0), 0)),
                  tile,
                  pl.BlockSpec((CONF_HALO, d), lambda i: (jnp.minimum((i + 1) * hb, n_halo - 1), 0)),
                  _resident((1, d)), lay.mod_spec(tm, 0, d), lay.mod_spec(tm, 1, d),
                  _resident((d, 2 * d)), _resident((1, 2 * d)), _resident((CONF_CONV_W, d)), _resident((1, d)),
                  _resident((1, d)), _resident((1, d))],
        out_specs=tile,
        scratch_shapes=[pltpu.VMEM((d // LANES, tm + (CONF_CONV_W - 1) * SUBLANES, LANES), F32),
                        pltpu.VMEM((d // LANES, tm, LANES), F32)],
        compiler_params=_params(1),
        name="conformer",
    )(x, x, x, norm_mix, mod, mod, w_pw1.astype(BF16), b_pw1, dw_w, dw_b, ln_g, ln_b)


def kernel(x_prompt, x_sample, cache_mla_ckv, cache_mla_krope, state_rglru, c, c_ctx, ada_w, ada_b, norm_mix, norm_ffn, mla_w_dq, mla_q_norm, mla_w_uq, mla_w_dkv, mla_kv_norm, mla_w_ukv, mla_w_o, rg_w_x, rg_w_y, rg_conv_w, rg_conv_b, rg_w_a, rg_b_a, rg_w_i, rg_b_i, rg_lam, rg_w_o, fn_w_o, fn_b_o, cf_w_pw1, cf_b_pw1, cf_dw_w, cf_dw_b, cf_ln_g, cf_ln_b, cf_w_pw2, cf_b_pw2, ffn_w_gate, ffn_w_up, ffn_w_down, final_norm):
    bp, sp, d = x_prompt.shape
    bs, ss, _ = x_sample.shape
    depth = ada_w.shape[0]
    assert depth == 4 and bs < SUBLANES and sp == SEQ_TILE and ss % GRID_W == 0
    lay = _Layout(bp, sp, bs, ss)
    x_in = (x_prompt.reshape(lay.tp, d), x_sample.reshape(lay.ts, d))

    cvec = jnp.concatenate([c_ctx[None], c, jnp.zeros((SUBLANES - 1 - bs, d), F32)], axis=0)
    mods = _ada_tables(cvec, ada_w, ada_b).reshape(depth, SUBLANES, 1, -1)
    zero_bias = jnp.zeros((1, d), F32)
    ffn_stacks = (ffn_w_gate, ffn_w_up, ffn_w_down)

    def ffn(layer, weights, x, m, w_out, b_out, extra=(), **tail):
        cast_next = [(w, layer + 1) for w in ffn_stacks] if layer + 1 < depth else []
        out, converted = _ffn_layer(lay, x, m, w_out, b_out, mods[layer], norm_ffn[layer:layer + 1], weights,
                                    cast_next=cast_next + [(w, 0) for w in extra], **tail)
        return out, converted[:len(ffn_stacks)], converted[len(ffn_stacks):]

    m, ckv_state, krope_state, converted = _mla_layer(
        lay, *x_in, mods[0], norm_mix[0:1], cache_mla_ckv[:, 0], cache_mla_krope[:, 0], mla_w_dq[0], mla_q_norm[0:1],
        mla_w_uq[0], mla_w_dkv[0], mla_kv_norm[0:1], mla_w_ukv[0],
        cast_next=[(w, 0) for w in ffn_stacks] + [(mla_w_o, 0)])
    weights, w_o = converted[:3], converted[3]
    x, weights, (w_x, w_y, w_o) = ffn(0, weights, x_in, m, w_o, zero_bias, extra=(rg_w_x, rg_w_y, rg_w_o))

    m, rg_state = _rglru_layer(lay, x, mods[1], norm_mix[1:2], state_rglru[:, 0], w_x, w_y, rg_conv_w[0],
                               rg_conv_b[0:1], rg_w_a[0], rg_b_a[0], rg_w_i[0], rg_b_i[0], rg_lam[0])
    (x, zc, zs), weights, (w_o,) = ffn(1, weights, x, m, w_o, zero_bias, extra=(fn_w_o,),
                                       fnet_next=(norm_mix[2:3], mods[2]))

    m = _fnet_layer(lay, zc, zs)
    x, weights, (w_pw1, w_o) = ffn(2, weights, x, m, w_o, fn_b_o[0:1], extra=(cf_w_pw1, cf_w_pw2))

    m = _conformer_layer(lay, x, mods[3], norm_mix[3:4], w_pw1, cf_b_pw1[0:1], cf_dw_w[0], cf_dw_b[0:1],
                         cf_ln_g[0:1], cf_ln_b[0:1])
    (y_prompt, y_sample), _, _ = ffn(3, weights, x, m, w_o, cf_b_pw2[0:1], final_norm=final_norm[None])

    return (y_prompt.reshape(bp, sp, d), y_sample.reshape(bs, ss, d), ckv_state, krope_state, rg_state)
```

```python
import functools

import numpy as np
import jax
import jax.numpy as jnp
from jax import lax
from jax.experimental import pallas as pl
from jax.experimental.pallas import tpu as pltpu

F32 = jnp.float32
BF16 = jnp.bfloat16

EPS = 1e-6
GRID_W = 64
MLA_HEADS = 8
KV_LORA_RANK = 256
QK_NOPE_DIM = 128
QK_ROPE_DIM = 64
V_HEAD_DIM = 128
ROPE_THETA = 10000.0
RG_BLOCKS = 4
RG_CONV_W = 4
RG_C = 8.0
FNET_GROUPS = 4
CONF_CONV_W = 31

LANES = 128
SUBLANES = 8
BF16_SUBLANES = 16
HEAD_W = 2 * LANES
SEQ_TILE = 256
FFN_TILE = 512
MLA_ROW_TILE = 512
ATTN_Q_TILE = 2048
ATTN_SUB_TILE = 256
ATTN_SEQS_PER_STEP = 2
ATTN_MXU_SUM_MIN_KEYS = 1024
FNET_ROW_TILE = 512
FNET_PROMPT_GROUP = 4
FFN_CHUNK = 256
CONF_HALO = 16
RG_HALO = 16
CONF_CHAINS = 4
CONF_GLU_CHUNK = 256
VMEM_LIMIT = 52 * 1024 * 1024


def _dot(a, b):
    return jnp.dot(a, b, preferred_element_type=F32)


def _rms(x, g):
    return x * lax.rsqrt(jnp.mean(x * x, axis=-1, keepdims=True) + EPS) * g


def _norm_mod(x, g, shift, scale):
    return _rms(x, g) * (1.0 + scale) + shift


def _resident(shape):
    nd = len(shape)
    return pl.BlockSpec(shape, lambda *_: (0,) * nd, pipeline_mode=pl.Buffered(1))


def _params(n_axes, semantics="parallel"):
    return pltpu.CompilerParams(dimension_semantics=(semantics,) * n_axes, vmem_limit_bytes=VMEM_LIMIT)


class _Layout:
    def __init__(self, n_prompt_seq, prompt_len, n_sample_seq, sample_len):
        self.bp, self.sp, self.bs, self.ss = n_prompt_seq, prompt_len, n_sample_seq, sample_len
        self.tp = n_prompt_seq * prompt_len
        self.ts = n_sample_seq * sample_len
        self.t = self.tp + self.ts

    def prompt_tiles(self, tile):
        assert self.tp % tile == 0 and self.ss % tile == 0
        return self.tp // tile

    def sample_tiles_per_seq(self, tile):
        return self.ss // tile

    def mod_row(self, tile):
        npt, spb = self.prompt_tiles(tile), self.sample_tiles_per_seq(tile)
        return lambda i: jnp.where(i < npt, 0, 1 + (i - npt) // spb)

    def mod_spec(self, tile, chunk, d, order=lambda i: i):
        row = self.mod_row(tile)
        return pl.BlockSpec((None, 1, d), lambda i: (row(order(i)), 0, chunk))

    def row_spec(self, tile, order=lambda i: i):
        row = self.mod_row(tile)
        return lambda d: pl.BlockSpec((None, 1, d), lambda i: (row(order(i)), 0, 0))


def _ada_kernel(c_ref, w_ref, b_ref, o_ref):
    c = c_ref[...]
    s = (c * jax.nn.sigmoid(c)).astype(BF16)
    o_ref[...] = _dot(s, w_ref[...].astype(BF16)) + b_ref[...]


def _ada_tables(cvec, ada_w, ada_b):
    depth, d, n = ada_w.shape
    tn = n // 4
    return pl.pallas_call(
        _ada_kernel,
        out_shape=jax.ShapeDtypeStruct((depth, SUBLANES, n), F32),
        grid=(depth, n // tn),
        in_specs=[
            pl.BlockSpec((SUBLANES, d), lambda l, j: (0, 0)),
            pl.BlockSpec((None, d, tn), lambda l, j: (l, 0, j)),
            pl.BlockSpec((None, 1, tn), lambda l, j: (l, 0, j)),
        ],
        out_specs=pl.BlockSpec((None, SUBLANES, tn), lambda l, j: (l, 0, j)),
        compiler_params=_params(2),
        name="ada_tables",
    )(cvec, ada_w, ada_b.reshape(depth, 1, n))


def _read_tokens(refs, n_prompt_tiles):
    if len(refs) == 1:
        return refs[0][...]
    return jnp.where(pl.program_id(0) < n_prompt_tiles, refs[0][...], refs[1][...])


def _fnet_channel_dft(h, w_ref, zc_ref, zs_ref):
    gw = w_ref.shape[0]
    for g in range(FNET_GROUPS):
        sl = slice(g * gw, (g + 1) * gw)
        f = _dot(h[:, sl], w_ref[...])
        zc_ref[:, sl] = f[:, :gw].astype(BF16)
        zs_ref[:, sl] = f[:, gw:].astype(BF16)


def _ffn_kernel(*refs, n_chunks, n_x, n_m, tail, n_prompt_tiles, n_cast):
    x_refs, m_refs, refs = refs[:n_x], refs[n_x:n_x + n_m], refs[n_x + n_m:]
    wo_ref, bo_ref, g1_ref, sh_ref, sc_ref, g2_ref, nf_ref, wg_ref, wu_ref, wd_ref = refs[:10]
    refs, a_ref = refs[10:-1], refs[-1]
    if n_cast:
        for src, dst in zip(refs[:n_cast], refs[-n_cast:]):
            dst[...] = src[...].astype(BF16)
        refs = refs[n_cast:-n_cast]
    m = _read_tokens(m_refs, n_prompt_tiles)
    x1 = _read_tokens(x_refs, n_prompt_tiles) + g1_ref[...] * (_dot(m, wo_ref[...]) + bo_ref[...])
    h = _norm_mod(x1, nf_ref[...], sh_ref[...], sc_ref[...]).astype(BF16)
    for c in range(n_chunks):
        sl = slice(c * FFN_CHUNK, (c + 1) * FFN_CHUNK)
        g = _dot(h, wg_ref[:, sl])
        u = _dot(h, wu_ref[:, sl])
        a_ref[:, sl] = (g * jax.nn.sigmoid(g) * u).astype(BF16)
    x2 = x1 + g2_ref[...] * _dot(a_ref[...], wd_ref[...])
    if tail is None:
        refs[0][...] = x2
    elif tail == "fnet":
        nm_ref, sh1_ref, sc1_ref, dft_ref, o_ref, zc_ref, zs_ref = refs
        o_ref[...] = x2
        h_next = _norm_mod(x2, nm_ref[...], sh1_ref[...], sc1_ref[...]).astype(BF16)
        _fnet_channel_dft(h_next, dft_ref, zc_ref, zs_ref)
    else:
        fin_ref, op_ref, os_ref = refs
        y = _rms(x2, fin_ref[...])
        is_prompt = pl.program_id(0) < n_prompt_tiles

        @pl.when(is_prompt)
        def _():
            op_ref[...] = y

        @pl.when(jnp.logical_not(is_prompt))
        def _():
            os_ref[...] = y


def _cast_specs(stack, index, n_steps):
    _, rows, width = stack.shape
    span = 1
    while (rows * span) % n_steps or (rows * span // n_steps) % BF16_SUBLANES:
        span *= 2
    assert n_steps % span == 0
    block = rows * span // n_steps
    return (pl.BlockSpec((None, block, width), lambda i: (index, i // span, 0)),
            pl.BlockSpec((block, width), lambda i: (i // span, 0)),
            jax.ShapeDtypeStruct((rows, width), BF16))


def _ffn_layer(lay, x, m, w_out, b_out, mod, norm_ffn, weights, final_norm=None, fnet_next=None, cast_next=None):
    xs = x if isinstance(x, tuple) else (x,)
    ms = m if isinstance(m, tuple) else (m,)
    d = xs[0].shape[1]
    w_gate, w_up, w_down = weights
    dff = w_gate.shape[1]
    assert dff % FFN_CHUNK == 0
    tm = FFN_TILE
    n_steps = lay.t // tm
    npt = lay.prompt_tiles(tm)
    tile = pl.BlockSpec((tm, d), lambda i: (i, 0))
    split = [pl.BlockSpec((tm, d), lambda i: (jnp.minimum(i, npt - 1), 0)),
             pl.BlockSpec((tm, d), lambda i: (jnp.maximum(i - npt, 0), 0))]
    in_specs = (split if len(xs) == 2 else [tile]) + (split if len(ms) == 2 else [tile]) + [
        _resident((d, d)), _resident((1, d)),
        lay.mod_spec(tm, 2, d), lay.mod_spec(tm, 3, d), lay.mod_spec(tm, 4, d), lay.mod_spec(tm, 5, d),
        _resident((1, d)), _resident((d, dff)), _resident((d, dff)), _resident((dff, d))]
    args = list(xs) + list(ms) + [w_out, b_out, mod, mod, mod, mod, norm_ffn, w_gate, w_up, w_down]
    cast_out_specs, cast_out_shapes = [], []
    for stack, index in cast_next or ():
        src_spec, dst_spec, dst_shape = _cast_specs(stack, index, n_steps)
        in_specs.append(src_spec)
        args.append(stack)
        cast_out_specs.append(dst_spec)
        cast_out_shapes.append(dst_shape)
    out_shapes, out_specs, tail = [jax.ShapeDtypeStruct((lay.t, d), F32)], [tile], None
    if final_norm is not None:
        tail = "final"
        in_specs.append(_resident((1, d)))
        args.append(final_norm)
        out_shapes = [jax.ShapeDtypeStruct((lay.tp, d), F32), jax.ShapeDtypeStruct((lay.ts, d), F32)]
        out_specs = list(split)
    elif fnet_next is not None:
        tail = "fnet"
        norm_next, mod_next = fnet_next
        dft = _fnet_channel_table(d)
        in_specs += [_resident((1, d)), lay.mod_spec(tm, 0, d), lay.mod_spec(tm, 1, d), _resident(dft.shape)]
        args += [norm_next, mod_next, mod_next, dft]
        out_shapes += [jax.ShapeDtypeStruct((lay.t, d), BF16), jax.ShapeDtypeStruct((lay.t, d), BF16)]
        out_specs += [tile, tile]
    n_main = len(out_shapes)
    outs = pl.pallas_call(
        functools.partial(_ffn_kernel, n_chunks=dff // FFN_CHUNK, n_x=len(xs), n_m=len(ms), tail=tail,
                          n_prompt_tiles=npt, n_cast=len(cast_out_specs)),
        out_shape=tuple(out_shapes + cast_out_shapes),
        grid=(n_steps,),
        in_specs=in_specs,
        out_specs=tuple(out_specs + cast_out_specs),
        scratch_shapes=[pltpu.VMEM((tm, dff), BF16)],
        compiler_params=_params(1, "arbitrary"),
        name="ffn",
    )(*args)
    main = outs[0] if n_main == 1 else tuple(outs[:n_main])
    return main, tuple(outs[n_main:])


def _mla_qkv_kernel(*refs, scale, n_prompt_tiles, n_cast):
    (xp_ref, xs_ref, nm_ref, sh_ref, sc_ref, cos_ref, sin_ref, wdq_ref, qn_ref, wqn_ref, wqr_ref, wqx_ref,
     wdkv_ref, wdkvx_ref, kvn_ref, wk_ref, wv_ref) = refs[:17]
    q_ref, z_ref, k_ref, v_ref = refs[17 + n_cast:21 + n_cast]
    for src, dst in zip(refs[17:17 + n_cast], refs[21 + n_cast:]):
        dst[...] = src[...].astype(BF16)
    x_all = _read_tokens((xp_ref, xs_ref), n_prompt_tiles)
    for j in range(x_all.shape[0] // SEQ_TILE):
        rows = slice(j * SEQ_TILE, (j + 1) * SEQ_TILE)
        h = _norm_mod(x_all[rows], nm_ref[...], sh_ref[...], sc_ref[...]).astype(BF16)
        cos, sin = cos_ref[rows, :], sin_ref[rows, :]
        cq = _rms(_dot(h, wdq_ref[...]), qn_ref[...]).astype(BF16)
        q_nope = _dot(cq, wqn_ref[...])
        q_rope = _dot(cq, wqr_ref[...])
        q_swap = _dot(cq, wqx_ref[...])
        for hd in range(MLA_HEADS):
            sl = slice(hd * LANES, (hd + 1) * LANES)
            q_ref[rows, hd * HEAD_W:hd * HEAD_W + LANES] = (q_nope[:, sl] * scale).astype(BF16)
            q_ref[rows, hd * HEAD_W + LANES:(hd + 1) * HEAD_W] = (
                (q_rope[:, sl] * cos + q_swap[:, sl] * sin) * scale).astype(BF16)
        z = _dot(h, wdkv_ref[...])
        z_swap = _dot(h, wdkvx_ref[...])
        ckv = _rms(z[:, :KV_LORA_RANK], kvn_ref[...])
        k_rope = z[:, KV_LORA_RANK:] * cos + z_swap * sin
        z_ref[rows, :KV_LORA_RANK] = ckv
        z_ref[rows, KV_LORA_RANK:] = k_rope
        _expand_kv(ckv, k_rope, wk_ref, wv_ref, k_ref, v_ref, rows)


def _expand_kv(ckv, k_rope, wk_ref, wv_ref, k_ref, v_ref, rows=slice(None)):
    ckv = ckv.astype(BF16)
    k_rope = k_rope.astype(BF16)
    k_nope = _dot(ckv, wk_ref[...])
    for hd in range(MLA_HEADS):
        k_ref[rows, hd * HEAD_W:hd * HEAD_W + LANES] = k_nope[:, hd * LANES:(hd + 1) * LANES].astype(BF16)
        k_ref[rows, hd * HEAD_W + LANES:(hd + 1) * HEAD_W] = k_rope
    v_ref[rows, :] = _dot(ckv, wv_ref[...]).astype(BF16)


def _kv_expand_kernel(z_ref, wk_ref, wv_ref, k_ref, v_ref):
    z = z_ref[...]
    _expand_kv(z[:, :KV_LORA_RANK], z[:, KV_LORA_RANK:], wk_ref, wv_ref, k_ref, v_ref)


def _kv_expand(z, wk, wv, n=None):
    zw = z.shape[1]
    n = z.shape[0] if n is None else n
    tm = 2 * SEQ_TILE if n % (2 * SEQ_TILE) == 0 else SEQ_TILE
    return pl.pallas_call(
        _kv_expand_kernel,
        out_shape=(jax.ShapeDtypeStruct((n, MLA_HEADS * HEAD_W), BF16),
                   jax.ShapeDtypeStruct((n, MLA_HEADS * V_HEAD_DIM), BF16)),
        grid=(n // tm,),
        in_specs=[pl.BlockSpec((tm, zw), lambda i: (i, 0)), _resident(wk.shape), _resident(wv.shape)],
        out_specs=(pl.BlockSpec((tm, MLA_HEADS * HEAD_W), lambda i: (i, 0)),
                   pl.BlockSpec((tm, MLA_HEADS * V_HEAD_DIM), lambda i: (i, 0))),
        compiler_params=_params(1),
        name="mla_kv_expand",
    )(z, wk, wv)


def _attn_kernel(*refs, heads, sub, n_kv, mxu_sum, seqs):
    q_ref, kv_refs, o_ref = refs[0], refs[1:1 + 2 * n_kv], refs[-1]
    tq = q_ref.shape[0] // seqs
    for i, hd in [(i, hd) for i in range(seqs) for hd in range(heads)]:
        keys = [slice(i * (r.shape[0] // seqs), (i + 1) * (r.shape[0] // seqs)) for r in kv_refs[0::2]]
        ks = [r[kr, hd * HEAD_W:(hd + 1) * HEAD_W] for r, kr in zip(kv_refs[0::2], keys)]
        vs = [r[kr, hd * V_HEAD_DIM:(hd + 1) * V_HEAD_DIM] for r, kr in zip(kv_refs[1::2], keys)]
        if mxu_sum:
            vs = [jnp.concatenate([v, jnp.ones_like(v)], axis=-1) for v in vs]
        for j in range(tq // sub):
            rows = slice(i * tq + j * sub, i * tq + (j + 1) * sub)
            q = q_ref[rows, hd * HEAD_W:(hd + 1) * HEAD_W]
            ss = [lax.dot_general(q, k, (((1,), (1,)), ((), ())), preferred_element_type=F32) for k in ks]
            top = functools.reduce(jnp.maximum, [jnp.max(s, axis=-1, keepdims=True) for s in ss])
            ps = [jnp.exp(s - top) for s in ss]
            acc = sum(_dot(p.astype(BF16), v) for p, v in zip(ps, vs))
            if mxu_sum:
                den = acc[:, V_HEAD_DIM:V_HEAD_DIM + 1]
            else:
                den = sum(jnp.sum(p, axis=-1, keepdims=True) for p in ps)
            o_ref[rows, hd * V_HEAD_DIM:(hd + 1) * V_HEAD_DIM] = (acc[:, :V_HEAD_DIM] / den).astype(BF16)


def _attention(q, kv, *, n_seq, q_len, q_row0, tq, heads):
    hg = MLA_HEADS // heads
    seqs = ATTN_SEQS_PER_STEP if tq == q_len and q_len <= ATTN_SUB_TILE and n_seq % ATTN_SEQS_PER_STEP == 0 else 1
    assert q_row0 % (seqs * tq) == 0 and q_len % tq == 0
    qb0, qt = q_row0 // (seqs * tq), q_len // tq
    in_specs = [pl.BlockSpec((seqs * tq, heads * HEAD_W), lambda b, g, j: (qb0 + b * qt + j, g))]
    args = [q]
    for k, v, k_len, row0 in kv:
        assert row0 % (seqs * k_len) == 0
        kb0 = row0 // (seqs * k_len)
        in_specs += [pl.BlockSpec((seqs * k_len, heads * HEAD_W), lambda b, g, j, kb0=kb0: (kb0 + b, g)),
                     pl.BlockSpec((seqs * k_len, heads * V_HEAD_DIM), lambda b, g, j, kb0=kb0: (kb0 + b, g))]
        args += [k, v]
    return pl.pallas_call(
        functools.partial(_attn_kernel, heads=heads, sub=min(ATTN_SUB_TILE, tq), n_kv=len(kv),
                          mxu_sum=sum(src[2] for src in kv) >= ATTN_MXU_SUM_MIN_KEYS, seqs=seqs),
        out_shape=jax.ShapeDtypeStruct((n_seq * q_len, MLA_HEADS * V_HEAD_DIM), BF16),
        grid=(n_seq // seqs, hg, qt),
        in_specs=in_specs,
        out_specs=pl.BlockSpec((seqs * tq, heads * V_HEAD_DIM), lambda b, g, j: (b * qt + j, g)),
        compiler_params=_params(3),
        name="mla_attention",
    )(*args)


def _rope_tables(lay, tile):
    pos = np.arange(lay.ss)
    n_freq = QK_ROPE_DIM // 4
    inv_freq = ROPE_THETA ** (-np.arange(n_freq, dtype=np.float64) / n_freq)
    ang = np.concatenate([(pos // GRID_W)[:, None] * inv_freq, (pos % GRID_W)[:, None] * inv_freq], axis=-1)
    pad = np.zeros((lay.ss, LANES - QK_ROPE_DIM))
    cos = np.concatenate([np.cos(ang), np.cos(ang), pad], axis=-1)
    sin = np.concatenate([np.sin(ang), np.sin(ang), pad], axis=-1)
    ident_cos = np.concatenate([np.ones((tile, QK_ROPE_DIM)), np.zeros((tile, LANES - QK_ROPE_DIM))], axis=-1)
    cos = np.concatenate([ident_cos, cos], axis=0)
    sin = np.concatenate([np.zeros((tile, LANES)), sin], axis=0)
    return jnp.asarray(cos, F32), jnp.asarray(sin, F32)


def _swap_halves(w):
    half = QK_ROPE_DIM // 2
    return jnp.concatenate([-w[..., half:], w[..., :half]], axis=-1)


def _mla_layer(lay, x_prompt, x_sample, mod, norm_mix, cache_ckv, cache_krope, w_dq, q_norm, w_uq, w_dkv, kv_norm,
               w_ukv, cast_next=()):
    t, d = lay.t, x_prompt.shape[1]
    tm = MLA_ROW_TILE
    rq = w_dq.shape[1]
    qk = QK_NOPE_DIM + QK_ROPE_DIM
    zw = KV_LORA_RANK + LANES
    rope_pad = [(0, 0)] * 2 + [(0, LANES - QK_ROPE_DIM)]

    wq = w_uq.reshape(rq, MLA_HEADS, qk)
    wq_nope = wq[:, :, :QK_NOPE_DIM].reshape(rq, -1).astype(BF16)
    wq_rope = jnp.pad(wq[:, :, QK_NOPE_DIM:], rope_pad).reshape(rq, -1).astype(BF16)
    wq_swap = jnp.pad(_swap_halves(wq[:, :, QK_NOPE_DIM:]), rope_pad).reshape(rq, -1).astype(BF16)
    wdkv = jnp.pad(w_dkv, [(0, 0), (0, LANES - QK_ROPE_DIM)]).astype(BF16)
    wdkv_swap = jnp.pad(_swap_halves(w_dkv[:, KV_LORA_RANK:]), [(0, 0), (0, LANES - QK_ROPE_DIM)]).astype(BF16)
    wkv = w_ukv.reshape(KV_LORA_RANK, MLA_HEADS, QK_NOPE_DIM + V_HEAD_DIM)
    wk = wkv[:, :, :QK_NOPE_DIM].reshape(KV_LORA_RANK, -1).astype(BF16)
    wv = wkv[:, :, QK_NOPE_DIM:].reshape(KV_LORA_RANK, -1).astype(BF16)

    cos, sin = _rope_tables(lay, tm)
    npt, spb = lay.prompt_tiles(tm), lay.sample_tiles_per_seq(tm)
    rope_spec = pl.BlockSpec((tm, LANES), lambda i: (jnp.where(i < npt, 0, 1 + (i - npt) % spb), 0))
    def rows(width):
        return pl.BlockSpec((tm, width), lambda i: (i, 0))

    casts = [_cast_specs(stack, index, t // tm) for stack, index in cast_next]
    q, z, k_new, v_new, *converted = pl.pallas_call(
        functools.partial(_mla_qkv_kernel, scale=qk ** -0.5, n_prompt_tiles=npt, n_cast=len(casts)),
        out_shape=(jax.ShapeDtypeStruct((t, MLA_HEADS * HEAD_W), BF16), jax.ShapeDtypeStruct((t, zw), F32),
                   jax.ShapeDtypeStruct((t, MLA_HEADS * HEAD_W), BF16),
                   jax.ShapeDtypeStruct((t, MLA_HEADS * V_HEAD_DIM), BF16)) + tuple(c[2] for c in casts),
        grid=(t // tm,),
        in_specs=[pl.BlockSpec((tm, d), lambda i: (jnp.minimum(i, npt - 1), 0)),
                  pl.BlockSpec((tm, d), lambda i: (jnp.maximum(i - npt, 0), 0)), _resident((1, d)),
                  lay.mod_spec(tm, 0, d), lay.mod_spec(tm, 1, d), rope_spec, rope_spec,
                  _resident(w_dq.shape), _resident((1, rq)), _resident(wq_nope.shape), _resident(wq_rope.shape),
                  _resident(wq_swap.shape), _resident(wdkv.shape), _resident(wdkv_swap.shape),
                  _resident((1, KV_LORA_RANK)), _resident(wk.shape), _resident(wv.shape)] + [c[0] for c in casts],
        out_specs=(rows(MLA_HEADS * HEAD_W), rows(zw), rows(MLA_HEADS * HEAD_W), rows(MLA_HEADS * V_HEAD_DIM))
        + tuple(c[1] for c in casts),
        compiler_params=_params(1, "arbitrary"),
        name="mla_qkv",
    )(x_prompt, x_sample, norm_mix, mod, mod, cos, sin, w_dq.astype(BF16), q_norm, wq_nope, wq_rope, wq_swap, wdkv,
      wdkv_swap, kv_norm, wk, wv, *[stack for stack, _ in cast_next])

    past = cache_ckv.shape[1]
    z_cache = jnp.concatenate(
        [cache_ckv, cache_krope, jnp.zeros((lay.bs, past, LANES - QK_ROPE_DIM), F32)], axis=-1)
    k_past, v_past = _kv_expand(z_cache.reshape(lay.bs * past, zw), wk, wv)

    o_prompt = _attention(q, [(k_new, v_new, lay.sp, 0)], n_seq=lay.bp, q_len=lay.sp, q_row0=0, tq=lay.sp,
                          heads=MLA_HEADS)
    o_sample = _attention(q, [(k_past, v_past, past, 0), (k_new, v_new, lay.ss, lay.tp)], n_seq=lay.bs, q_len=lay.ss,
                          q_row0=lay.tp, tq=min(ATTN_Q_TILE, lay.ss), heads=1)
    ckv_state = z[:lay.tp, :KV_LORA_RANK].reshape(lay.bp, 1, lay.sp, KV_LORA_RANK)
    krope_state = z[:lay.tp, KV_LORA_RANK:KV_LORA_RANK + QK_ROPE_DIM].reshape(lay.bp, 1, lay.sp, QK_ROPE_DIM)
    return (o_prompt, o_sample), ckv_state, krope_state, tuple(converted)


def _seq_position(lay, tile, t):
    npt, spb = lay.prompt_tiles(tile), lay.sample_tiles_per_seq(tile)
    ppb = lay.sp // tile
    r = jnp.where(t < npt, t % ppb, (t - npt) % spb)
    n = jnp.where(t < npt, ppb, spb)
    return r, n


def _softplus(x):
    return jnp.maximum(x, 0.0) + jnp.log1p(jnp.exp(-jnp.abs(x)))


def _rglru_kernel(*refs, lay, n_tiles, reverse):
    if reverse:
        (xc_ref, hc_ref, yf_ref, h0_ref, wa_ref, ba_ref, wi_ref, bi_ref, lam_ref, wy_ref,
         o_ref, hl_ref, a_scr, u_scr, perm_scr, carry_scr) = refs
    else:
        (xp_ref, x_ref, xn_ref, nm_ref, sh_ref, sc_ref, h0_ref, wx_ref, cw_ref, cb_ref,
         wa_ref, ba_ref, wi_ref, bi_ref, lam_ref,
         o_ref, xc_ref, hc_ref, hl_ref, xw_scr, a_scr, u_scr, perm_scr, carry_scr) = refs
    tm, d = xc_ref.shape
    groups = tm // SUBLANES
    bw = d // RG_BLOCKS
    i = pl.program_id(0)
    t = n_tiles - 1 - i if reverse else i
    r, n = _seq_position(lay, tm, t)

    @pl.when(i == 0)
    def _():
        carry_scr[...] = jnp.zeros_like(carry_scr)

    if reverse:
        xc = xc_ref[...]
    else:
        has_prev, has_next = r > 0, r < n - 1

        def pre(x):
            return _norm_mod(x, nm_ref[...], sh_ref[...], sc_ref[...]).astype(BF16)

        x_nat = x_ref[...]
        for c in range(d // LANES):
            for s in range(SUBLANES):
                perm_scr[c, pl.ds(s, groups, stride=SUBLANES), :] = x_nat[s * groups:(s + 1) * groups,
                                                                         c * LANES:(c + 1) * LANES]
        hc = pre(jnp.concatenate([perm_scr[c] for c in range(d // LANES)], axis=-1))
        hc_ref[...] = hc
        xw = _dot(jnp.concatenate([pre(xp_ref[...]), hc, pre(xn_ref[...])], axis=0), wx_ref[...])
        sub = lax.broadcasted_iota(jnp.int32, (SUBLANES, d), 0)

        def tile_group(j):
            return xw[RG_HALO + j * SUBLANES:RG_HALO + (j + 1) * SUBLANES]

        def halo_row(k, valid):
            return jnp.broadcast_to(jnp.where(valid, xw[k:k + 1], 0.0), (SUBLANES, d))

        before = jnp.where(sub == 0, halo_row(RG_HALO - 1, has_prev), pltpu.roll(tile_group(groups - 1), 1, 0))
        after = [jnp.where(sub == SUBLANES - 1, halo_row(RG_HALO + tm + j, has_next),
                           pltpu.roll(tile_group(j), SUBLANES - 1, 0)) for j in range(RG_CONV_W - 2)]
        xw_scr[0:SUBLANES] = before
        xw_scr[SUBLANES:SUBLANES + tm] = xw[RG_HALO:RG_HALO + tm]
        for j, grp in enumerate(after):
            xw_scr[(groups + 1 + j) * SUBLANES:(groups + 2 + j) * SUBLANES] = grp
        xc = cb_ref[...]
        for k in range(RG_CONV_W):
            xc = xc + cw_ref[k:k + 1, :] * xw_scr[k * SUBLANES:k * SUBLANES + tm]
        xc_ref[...] = xc

    for nb in range(RG_BLOCKS):
        sl = slice(nb * bw, (nb + 1) * bw)
        xcn = xc[:, sl]
        xcb = xcn.astype(BF16)
        rg = jax.nn.sigmoid(_dot(xcb, wa_ref[nb]) + ba_ref[:, sl])
        ig = jax.nn.sigmoid(_dot(xcb, wi_ref[nb]) + bi_ref[:, sl])
        log_a = -RG_C * rg * _softplus(-lam_ref[:, sl])
        a = jnp.exp(log_a)
        u = jnp.sqrt(-jnp.tanh(log_a) * (a * a + 1.0)) * (ig * xcn)
        a_scr[:, :, sl] = a.reshape(groups, SUBLANES, bw)
        u_scr[:, :, sl] = u.reshape(groups, SUBLANES, bw)

    def local_step(g, carry):
        h, p = carry
        j = groups - 1 - g if reverse else g
        a = a_scr[j]
        h = a * h + u_scr[j]
        p = a * p
        u_scr[j] = h
        a_scr[j] = p
        return h, p

    h_seg, p_seg = lax.fori_loop(0, groups, local_step, (jnp.zeros((SUBLANES, d), F32), jnp.ones((SUBLANES, d), F32)))
    is_start = r == n - 1 if reverse else r == 0
    state = jnp.where(is_start, h0_ref[...], carry_scr[...])
    entering = [None] * SUBLANES
    for s in (range(SUBLANES - 1, -1, -1) if reverse else range(SUBLANES)):
        entering[s] = state
        state = h_seg[s:s + 1] + p_seg[s:s + 1] * state
    carry_scr[...] = state
    hl_ref[...] = state
    y = u_scr[...] + a_scr[...] * jnp.concatenate(entering, axis=0)[None]
    y = y.reshape(tm, d)
    if reverse:
        gate = jax.nn.gelu(_dot(hc_ref[...], wy_ref[...]))
        mixed = (yf_ref[...] + y) * gate
        for c in range(d // LANES):
            perm_scr[c] = mixed[:, c * LANES:(c + 1) * LANES]
        for s in range(SUBLANES):
            for c in range(d // LANES):
                o_ref[s * groups:(s + 1) * groups, c * LANES:(c + 1) * LANES] = (
                    perm_scr[c, pl.ds(s, groups, stride=SUBLANES), :].astype(BF16))
    else:
        o_ref[...] = y


def _rglru_layer(lay, x, mod, norm_mix, state, w_x, w_y, conv_w, conv_b, w_a, b_a, w_i, b_i, lam):
    t, d = x.shape
    tm = SEQ_TILE
    n_tiles = t // tm
    hb = tm // SUBLANES
    halo_blocks, n_halo = tm // RG_HALO, t // RG_HALO
    bw = d // RG_BLOCKS

    def run(reverse, *fwd_out):
        dr = int(reverse)
        order = (lambda i: n_tiles - 1 - i) if reverse else (lambda i: i)
        h0 = jnp.concatenate([jnp.zeros((1, d), F32), state[:, dr], jnp.zeros((SUBLANES - 1 - lay.bs, d), F32)])
        h0_spec = lay.row_spec(tm, order)(d)
        tile = pl.BlockSpec((tm, d), lambda i: (order(i), 0))
        gate_specs = [_resident((RG_BLOCKS, bw, bw)), _resident((1, d)), _resident((RG_BLOCKS, bw, bw)),
                      _resident((1, d)), _resident((1, d))]
        gate_args = [w_a[dr].astype(BF16), b_a[dr:dr + 1], w_i[dr].astype(BF16), b_i[dr:dr + 1], lam[dr:dr + 1]]
        scan_scratch = [pltpu.VMEM((hb, SUBLANES, d), F32), pltpu.VMEM((hb, SUBLANES, d), F32),
                        pltpu.VMEM((d // LANES, tm, LANES), F32)]
        last_spec = pl.BlockSpec((None, 1, d), lambda i: (order(i), 0, 0))
        last_shape = jax.ShapeDtypeStruct((n_tiles, 1, d), F32)
        if reverse:
            xc, hc, y_fwd = fwd_out
            in_specs = [tile, tile, tile, h0_spec] + gate_specs + [_resident((d, d))]
            args = [xc, hc, y_fwd, h0.reshape(SUBLANES, 1, d)] + gate_args + [w_y.astype(BF16)]
            out_shape = (jax.ShapeDtypeStruct((t, d), BF16), last_shape)
            out_specs = (tile, last_spec)
            scratch = list(scan_scratch)
        else:
            in_specs = [pl.BlockSpec((RG_HALO, d), lambda i: (jnp.maximum(i * halo_blocks - 1, 0), 0)),
                        tile,
                        pl.BlockSpec((RG_HALO, d), lambda i: (jnp.minimum((i + 1) * halo_blocks, n_halo - 1), 0)),
                        _resident((1, d)), lay.mod_spec(tm, 0, d), lay.mod_spec(tm, 1, d), h0_spec,
                        _resident((d, d)), _resident((RG_CONV_W, d)), _resident((1, d))] + gate_specs
            args = [x, x, x, norm_mix, mod, mod, h0.reshape(SUBLANES, 1, d), w_x.astype(BF16), conv_w, conv_b] + gate_args
            out_shape = (jax.ShapeDtypeStruct((t, d), F32), jax.ShapeDtypeStruct((t, d), F32),
                         jax.ShapeDtypeStruct((t, d), BF16), last_shape)
            out_specs = (tile, tile, tile, last_spec)
            scratch = [pltpu.VMEM((tm + 2 * RG_HALO, d), F32)] + scan_scratch
        scratch.append(pltpu.VMEM((1, d), F32))
        return pl.pallas_call(
            functools.partial(_rglru_kernel, lay=lay, n_tiles=n_tiles, reverse=reverse),
            out_shape=out_shape,
            grid=(n_tiles,),
            in_specs=in_specs,
            out_specs=out_specs,
            scratch_shapes=scratch,
            compiler_params=_params(1, "arbitrary"),
            name="rglru_bwd" if reverse else "rglru_fwd",
        )(*args)

    y_fwd, xc, hc, last_f = run(False)
    m, last_b = run(True, xc, hc, y_fwd)
    ppb = lay.sp // tm
    tail = last_f[:lay.prompt_tiles(tm)].reshape(lay.bp, ppb, d)[:, ppb - 1]
    head = last_b[:lay.prompt_tiles(tm)].reshape(lay.bp, ppb, d)[:, 0]
    return m, jnp.stack([tail, head], axis=1)[:, None]


def _dft_tables(n, scale):
    jk = np.outer(np.arange(n), np.arange(n)) % n
    ang = 2.0 * np.pi * jk / n
    return np.cos(ang) * scale, np.sin(ang) * scale


def _fnet_channel_table(d):
    gw = d // FNET_GROUPS
    cos, sin = _dft_tables(gw, gw ** -0.5)
    return jnp.asarray(np.concatenate([cos, sin], axis=1), F32).astype(BF16)


def _fnet_position_kernel(wc_ref, ws_ref, zc_ref, zs_ref, o_ref, *, seq_len):
    for i in range(zc_ref.shape[0] // seq_len):
        rows = slice(i * seq_len, (i + 1) * seq_len)
        o_ref[i * wc_ref.shape[0]:(i + 1) * wc_ref.shape[0], :] = (
            _dot(wc_ref[...], zc_ref[rows, :]) - _dot(ws_ref[...], zs_ref[rows, :])).astype(BF16)


def _fnet_position(zc, zs, *, n_seq, seq_len, row0, tm, seqs_per_step=1):
    d = zc.shape[1]
    g = seqs_per_step
    assert row0 % (g * seq_len) == 0 and seq_len % tm == 0 and n_seq % g == 0 and (g == 1 or tm == seq_len)
    cos, sin = _dft_tables(seq_len, seq_len ** -0.5)
    wc, ws = jnp.asarray(cos, F32).astype(BF16), jnp.asarray(sin, F32).astype(BF16)
    sb0, rt = row0 // (g * seq_len), seq_len // tm
    w_spec = pl.BlockSpec((tm, seq_len), lambda b, j: (j, 0))
    z_spec = pl.BlockSpec((g * seq_len, d), lambda b, j: (sb0 + b, 0))
    return pl.pallas_call(
        functools.partial(_fnet_position_kernel, seq_len=seq_len),
        out_shape=jax.ShapeDtypeStruct((n_seq * seq_len, d), BF16),
        grid=(n_seq // g, rt),
        in_specs=[w_spec, w_spec, z_spec, z_spec],
        out_specs=pl.BlockSpec((g * tm, d), lambda b, j: (b * rt + j, 0)),
        compiler_params=_params(2),
        name="fnet_position",
    )(wc, ws, zc, zs)


def _fnet_layer(lay, zc, zs):
    group = FNET_PROMPT_GROUP if lay.bp % FNET_PROMPT_GROUP == 0 else 1
    return (_fnet_position(zc, zs, n_seq=lay.bp, seq_len=lay.sp, row0=0, tm=lay.sp, seqs_per_step=group),
            _fnet_position(zc, zs, n_seq=lay.bs, seq_len=lay.ss, row0=lay.tp, tm=min(FNET_ROW_TILE, lay.ss)))


def _conformer_kernel(xp_ref, xc_ref, xn_ref, nm_ref, sh_ref, sc_ref, w1_ref, b1_ref, dw_ref, db_ref, lg_ref, lb_ref,
                      o_ref, src_scr, acc_scr, *, lay):
    tm, d = xc_ref.shape
    r, n = _seq_position(lay, tm, pl.program_id(0))
    has_prev, has_next = r > 0, r < n - 1

    def pre(ref):
        return _norm_mod(ref[...], nm_ref[...], sh_ref[...], sc_ref[...]).astype(BF16)

    h = jnp.concatenate([pre(xp_ref), pre(xc_ref), pre(xn_ref)], axis=0)
    groups = tm // SUBLANES
    n_before = (CONF_CONV_W - 1) // 2
    n_after = CONF_CONV_W - 1 - n_before
    sub = lax.broadcasted_iota(jnp.int32, (SUBLANES, LANES), 0)

    def group_rows(g):
        return slice(g * SUBLANES, (g + 1) * SUBLANES)

    for j in range(d // CONF_GLU_CHUNK):
        val = slice(j * CONF_GLU_CHUNK, (j + 1) * CONF_GLU_CHUNK)
        gate = slice(d + j * CONF_GLU_CHUNK, d + (j + 1) * CONF_GLU_CHUNK)
        glu = (_dot(h, w1_ref[:, val]) + b1_ref[:, val]) * jax.nn.sigmoid(_dot(h, w1_ref[:, gate]) + b1_ref[:, gate])
        for i in range(CONF_GLU_CHUNK // LANES):
            c = j * CONF_GLU_CHUNK // LANES + i
            col = glu[:, i * LANES:(i + 1) * LANES]
            for s in range(SUBLANES):
                src_scr[c, pl.ds(n_before * SUBLANES + s, groups, stride=SUBLANES), :] = (
                    col[CONF_HALO + s * groups:CONF_HALO + (s + 1) * groups])
            for k in range(1, n_before + 1):
                edge = jnp.where(has_prev, col[CONF_HALO - k:CONF_HALO - k + 1], 0.0)
                inner = pltpu.roll(src_scr[c, group_rows(n_before + groups - k), :], 1, 0)
                src_scr[c, group_rows(n_before - k), :] = jnp.where(sub == 0, edge, inner)
            for k in range(n_after):
                edge = jnp.where(has_next, col[CONF_HALO + tm + k:CONF_HALO + tm + k + 1], 0.0)
                inner = pltpu.roll(src_scr[c, group_rows(n_before + k), :], SUBLANES - 1, 0)
                src_scr[c, group_rows(n_before + groups + k), :] = jnp.where(sub == SUBLANES - 1, edge, inner)
    for c in range(d // LANES):
        lanes = slice(c * LANES, (c + 1) * LANES)
        w = [jnp.broadcast_to(dw_ref[k:k + 1, lanes], (SUBLANES, LANES)) for k in range(CONF_CONV_W)]
        bias = jnp.broadcast_to(db_ref[:, lanes], (SUBLANES, LANES))

        def conv_rows(g, carry, c=c, w=w, bias=bias):
            row0 = pl.multiple_of(g * SUBLANES, SUBLANES)
            part = [bias] + [None] * (CONF_CHAINS - 1)
            for k in range(CONF_CONV_W):
                term = w[k] * src_scr[c, pl.ds(row0 + k * SUBLANES, SUBLANES), :]
                j = k % CONF_CHAINS
                part[j] = term if part[j] is None else part[j] + term
            while len(part) > 1:
                part = [a + b for a, b in zip(part[0::2], part[1::2])]
            acc_scr[c, pl.ds(row0, SUBLANES), :] = part[0]
            return carry

        lax.fori_loop(0, groups, conv_rows, 0, unroll=8)
    acc = jnp.concatenate([acc_scr[c] for c in range(d // LANES)], axis=-1)
    mu = jnp.mean(acc, axis=-1, keepdims=True)
    cen = acc - mu
    var = jnp.mean(cen * cen, axis=-1, keepdims=True)
    y = cen * lax.rsqrt(var + EPS) * lg_ref[...] + lb_ref[...]
    y = y * jax.nn.sigmoid(y)
    for c in range(d // LANES):
        acc_scr[c] = y[:, c * LANES:(c + 1) * LANES]
    for s in range(SUBLANES):
        for c in range(d // LANES):
            o_ref[s * groups:(s + 1) * groups, c * LANES:(c + 1) * LANES] = (
                acc_scr[c, pl.ds(s, groups, stride=SUBLANES), :].astype(BF16))


def _conformer_layer(lay, x, mod, norm_mix, w_pw1, b_pw1, dw_w, dw_b, ln_g, ln_b):
    t, d = x.shape
    tm = SEQ_TILE
    hb = tm // CONF_HALO
    n_halo = t // CONF_HALO
    tile = pl.BlockSpec((tm, d), lambda i: (i, 0))
    return pl.pallas_call(
        functools.partial(_conformer_kernel, lay=lay),
        out_shape=jax.ShapeDtypeStruct((t, d), BF16),
        grid=(t // tm,),
        in_specs=[pl.BlockSpec((CONF_HALO, d), lambda i: (jnp.maximum(i * hb - 1, 0), 0)),
                  tile,
                  pl.BlockSpec((CONF_HALO, d), lambda i: (jnp.minimum((i + 1) * hb, n_halo - 1), 0)),
                  _resident((1, d)), lay.mod_spec(tm, 0, d), lay.mod_spec(tm, 1, d),
                  _resident((d, 2 * d)), _resident((1, 2 * d)), _resident((CONF_CONV_W, d)), _resident((1, d)),
                  _resident((1, d)), _resident((1, d))],
        out_specs=tile,
        scratch_shapes=[pltpu.VMEM((d // LANES, tm + (CONF_CONV_W - 1) * SUBLANES, LANES), F32),
                        pltpu.VMEM((d // LANES, tm, LANES), F32)],
        compiler_params=_params(1),
        name="conformer",
    )(x, x, x, norm_mix, mod, mod, w_pw1.astype(BF16), b_pw1, dw_w, dw_b, ln_g, ln_b)


def kernel(x_prompt, x_sample, cache_mla_ckv, cache_mla_krope, state_rglru, c, c_ctx, ada_w, ada_b, norm_mix, norm_ffn, mla_w_dq, mla_q_norm, mla_w_uq, mla_w_dkv, mla_kv_norm, mla_w_ukv, mla_w_o, rg_w_x, rg_w_y, rg_conv_w, rg_conv_b, rg_w_a, rg_b_a, rg_w_i, rg_b_i, rg_lam, rg_w_o, fn_w_o, fn_b_o, cf_w_pw1, cf_b_pw1, cf_dw_w, cf_dw_b, cf_ln_g, cf_ln_b, cf_w_pw2, cf_b_pw2, ffn_w_gate, ffn_w_up, ffn_w_down, final_norm):
    bp, sp, d = x_prompt.shape
    bs, ss, _ = x_sample.shape
    depth = ada_w.shape[0]
    assert depth == 4 and bs < SUBLANES and sp == SEQ_TILE and ss % GRID_W == 0
    lay = _Layout(bp, sp, bs, ss)
    x_in = (x_prompt.reshape(lay.tp, d), x_sample.reshape(lay.ts, d))

    cvec = jnp.concatenate([c_ctx[None], c, jnp.zeros((SUBLANES - 1 - bs, d), F32)], axis=0)
    mods = _ada_tables(cvec, ada_w, ada_b).reshape(depth, SUBLANES, 1, -1)
    zero_bias = jnp.zeros((1, d), F32)
    ffn_stacks = (ffn_w_gate, ffn_w_up, ffn_w_down)

    def ffn(layer, weights, x, m, w_out, b_out, extra=(), **tail):
        cast_next = [(w, layer + 1) for w in ffn_stacks] if layer + 1 < depth else []
        out, converted = _ffn_layer(lay, x, m, w_out, b_out, mods[layer], norm_ffn[layer:layer + 1], weights,
                                    cast_next=cast_next + [(w, 0) for w in extra], **tail)
        return out, converted[:len(ffn_stacks)], converted[len(ffn_stacks):]

    m, ckv_state, krope_state, converted = _mla_layer(
        lay, *x_in, mods[0], norm_mix[0:1], cache_mla_ckv[:, 0], cache_mla_krope[:, 0], mla_w_dq[0], mla_q_norm[0:1],
        mla_w_uq[0], mla_w_dkv[0], mla_kv_norm[0:1], mla_w_ukv[0],
        cast_next=[(w, 0) for w in ffn_stacks] + [(mla_w_o, 0)])
    weights, w_o = converted[:3], converted[3]
    x, weights, (w_x, w_y, w_o) = ffn(0, weights, x_in, m, w_o, zero_bias, extra=(rg_w_x, rg_w_y, rg_w_o))

    m, rg_state = _rglru_layer(lay, x, mods[1], norm_mix[1:2], state_rglru[:, 0], w_x, w_y, rg_conv_w[0],
                               rg_conv_b[0:1], rg_w_a[0], rg_b_a[0], rg_w_i[0], rg_b_i[0], rg_lam[0])
    (x, zc, zs), weights, (w_o,) = ffn(1, weights, x, m, w_o, zero_bias, extra=(fn_w_o,),
                                       fnet_next=(norm_mix[2:3], mods[2]))

    m = _fnet_layer(lay, zc, zs)
    x, weights, (w_pw1, w_o) = ffn(2, weights, x, m, w_o, fn_b_o[0:1], extra=(cf_w_pw1, cf_w_pw2))

    m = _conformer_layer(lay, x, mods[3], norm_mix[3:4], w_pw1, cf_b_pw1[0:1], cf_dw_w[0], cf_dw_b[0:1],
                         cf_ln_g[0:1], cf_ln_b[0:1])
    (y_prompt, y_sample), _, _ = ffn(3, weights, x, m, w_o, cf_b_pw2[0:1], final_norm=final_norm[None])

    return (y_prompt.reshape(bp, sp, d), y_sample.reshape(bs, ss, d), ckv_state, krope_state, rg_state)
```
